```python
import jax, jax.numpy as jnp
from jax import lax
import numpy as np

D_MODEL = 1024
BATCH = 8
SEQ = 8192
DEPTH = 1
DEC_BATCH = 16
DEC_SEQ = 32
PAST_LEN = 2048

CHUNK = 64
N_Q_HEADS = 16
N_KV_HEADS = 4
HEAD_DIM = 64
Q_PER_KV = N_Q_HEADS // N_KV_HEADS
ATT_WIDTH = N_Q_HEADS * HEAD_DIM
KV_WIDTH = N_KV_HEADS * HEAD_DIM
WINDOW = 128
WINDOW_CHUNKS = WINDOW // CHUNK
ROT_DIM = HEAD_DIM // 4
ROPE_THETA = 500000.0
SSM_WIDTH = 2 * D_MODEL
SSM_HEAD_DIM = 64
SSM_HEADS = SSM_WIDTH // SSM_HEAD_DIM
SSM_GROUPS = 4
HEADS_PER_GROUP = SSM_HEADS // SSM_GROUPS
SSM_STATE = 128
BC_WIDTH = SSM_GROUPS * SSM_STATE
CONV_W = 4
CONV_CH = SSM_WIDTH + 2 * BC_WIDTH
EPS = 1e-6
COL_SIZES = (ATT_WIDTH, KV_WIDTH, KV_WIDTH, ATT_WIDTH, SSM_WIDTH, CONV_CH, SSM_HEADS, D_MODEL, D_MODEL)
IN_WIDTH = ATT_WIDTH + 2 * KV_WIDTH + ATT_WIDTH + SSM_WIDTH + CONV_CH + SSM_HEADS + 2 * D_MODEL

kernel_name = 'hybrid_swa_sink_ssd_streaming_step'


def _rms_norm(x, w):
    xf = x.astype(jnp.float32)
    y = xf * lax.rsqrt(jnp.mean(xf * xf, axis=-1, keepdims=True) + EPS)
    return (y * w.astype(jnp.float32)).astype(x.dtype)


def _split_cols(proj):
    parts, off = [], 0
    for size in COL_SIZES:
        parts.append(proj[..., off:off + size])
        off += size
    return parts


def _partial_rope(x, pos):
    half = ROT_DIM // 2
    inv = ROPE_THETA ** (-(jnp.arange(half, dtype=jnp.float32) * 2.0 / ROT_DIM))
    ang = pos.astype(jnp.float32)[:, None] * inv[None, :]
    cos = jnp.cos(ang)[None, :, None, :]
    sin = jnp.sin(ang)[None, :, None, :]
    xf = x.astype(jnp.float32)
    x1, x2, rest = xf[..., :half], xf[..., half:ROT_DIM], xf[..., ROT_DIM:]
    return jnp.concatenate([x1 * cos - x2 * sin, x2 * cos + x1 * sin, rest], axis=-1).astype(x.dtype)


def _sink_softmax(s, sinks):
    sk = sinks.astype(jnp.float32).reshape(N_KV_HEADS, Q_PER_KV, 1, 1)
    m = jnp.maximum(jnp.max(s, axis=-1, keepdims=True), sk)
    p = jnp.exp(s - m)
    return p / (jnp.sum(p, axis=-1, keepdims=True) + jnp.exp(sk - m))


def _banded_window_attention(q, k, v, sinks):
    bsz, L = q.shape[:2]
    n_c = L // CHUNK
    qc = q.reshape(bsz, n_c, CHUNK, N_KV_HEADS, Q_PER_KV, HEAD_DIM)
    pad = ((0, 0), (WINDOW_CHUNKS, 0), (0, 0), (0, 0), (0, 0))
    kp = jnp.pad(k.reshape(bsz, n_c, CHUNK, N_KV_HEADS, HEAD_DIM), pad)
    vp = jnp.pad(v.reshape(bsz, n_c, CHUNK, N_KV_HEADS, HEAD_DIM), pad)
    k_band = jnp.concatenate([kp[:, j:j + n_c] for j in range(WINDOW_CHUNKS + 1)], axis=2)
    v_band = jnp.concatenate([vp[:, j:j + n_c] for j in range(WINDOW_CHUNKS + 1)], axis=2)
    key_chunk = (jnp.arange(n_c)[:, None] - WINDOW_CHUNKS
                 + (jnp.arange((WINDOW_CHUNKS + 1) * CHUNK) // CHUNK)[None, :])
    valid = key_chunk >= 0
    s = jnp.einsum('bcqhgd,bckhd->bchgqk', qc, k_band).astype(jnp.float32) * (HEAD_DIM ** -0.5)
    s = jnp.where(valid[None, :, None, None, None, :], s, -jnp.inf)
    p = _sink_softmax(s, sinks).astype(v.dtype)
    out = jnp.einsum('bchgqk,bckhd->bcqhgd', p, v_band)
    return out.reshape(bsz, L, ATT_WIDTH)


def _cached_window_attention(q, k, v, cache_k, cache_v, sinks):
    bsz, L = q.shape[:2]
    k_all = jnp.concatenate([cache_k.astype(k.dtype), k], axis=1)
    v_all = jnp.concatenate([cache_v.astype(v.dtype), v], axis=1)
    qg = q.reshape(bsz, L, N_KV_HEADS, Q_PER_KV, HEAD_DIM)
    s = jnp.einsum('bqhgd,bkhd->bhgqk', qg, k_all).astype(jnp.float32) * (HEAD_DIM ** -0.5)
    p = _sink_softmax(s, sinks).astype(v.dtype)
    out = jnp.einsum('bhgqk,bkhd->bqhgd', p, v_all)
    return out.reshape(bsz, L, ATT_WIDTH)


def _causal_dwconv(x, buf, w, b):
    L = x.shape[1]
    xpad = jnp.concatenate([buf.astype(x.dtype), x], axis=1)
    y = b[None, None, :]
    for tap in range(CONV_W):
        y = y + xpad[:, tap:tap + L] * w[tap][None, None, :]
    return y, xpad[:, L:]


def _ssd_chunk(h0, xs, dt, bm, cm, a):
    bsz, Q = xs.shape[:2]
    x5 = xs.reshape(bsz, Q, SSM_GROUPS, HEADS_PER_GROUP, SSM_HEAD_DIM).astype(jnp.float32)
    dt4 = dt.reshape(bsz, Q, SSM_GROUPS, HEADS_PER_GROUP)
    cum = jnp.cumsum(dt4 * a.reshape(SSM_GROUPS, HEADS_PER_GROUP), axis=1)
    seg = cum[:, :, None] - cum[:, None, :]
    causal = jnp.tril(jnp.ones((Q, Q), dtype=bool))[None, :, :, None, None]
    decay = jnp.exp(jnp.where(causal, seg, -jnp.inf))
    bf, cf = bm.astype(jnp.float32), cm.astype(jnp.float32)
    cb = jnp.einsum('bign,bjgn->bijg', cf, bf)
    y = jnp.einsum('bijg,bijgh,bjgh,bjghp->bighp', cb, decay, dt4, x5)
    h0g = h0.reshape(bsz, SSM_GROUPS, HEADS_PER_GROUP, SSM_HEAD_DIM, SSM_STATE)
    y = y + jnp.einsum('bign,bghpn,bigh->bighp', cf, h0g, jnp.exp(cum))
    to_end = jnp.exp(cum[:, -1:] - cum) * dt4
    h_new = (jnp.exp(cum[:, -1])[..., None, None] * h0g
             + jnp.einsum('bjgn,bjgh,bjghp->bghpn', bf, to_end, x5))
    return (y.reshape(bsz, Q, SSM_HEADS, SSM_HEAD_DIM),
            h_new.reshape(bsz, SSM_HEADS, SSM_HEAD_DIM, SSM_STATE))


def _layer(x, pos, is_prompt, cache_k, cache_v, conv_buf, ssm_h,
           pre_norm_w, w_in, conv_w, conv_b, dt_bias, a_log, d_skip, sinks,
           ssm_norm_w, w_attn_o, w_ssm_o, w_out, post_norm_w):
    bsz, L, _ = x.shape
    h = _rms_norm(x, pre_norm_w)
    proj = h @ w_in
    q, k, v, g_att, z, xbc, dt_raw, gate_a, gate_s = _split_cols(proj)

    q = _partial_rope(q.reshape(bsz, L, N_Q_HEADS, HEAD_DIM), pos)
    k = _partial_rope(k.reshape(bsz, L, N_KV_HEADS, HEAD_DIM), pos)
    v = v.reshape(bsz, L, N_KV_HEADS, HEAD_DIM)
    if is_prompt:
        att = _banded_window_attention(q, k, v, sinks)
        n_keep = min(WINDOW, L)
        new_k, new_v = k[:, L - n_keep:], v[:, L - n_keep:]
    else:
        att = _cached_window_attention(q, k, v, cache_k, cache_v, sinks)
        new_k, new_v = k, v
    branch_a = (att * jax.nn.silu(g_att)) @ w_attn_o

    xbc_c, new_buf = _causal_dwconv(xbc, conv_buf, conv_w, conv_b)
    xbc_c = jax.nn.silu(xbc_c)
    xs = xbc_c[..., :SSM_WIDTH].reshape(bsz, L, SSM_HEADS, SSM_HEAD_DIM)
    bm = xbc_c[..., SSM_WIDTH:SSM_WIDTH + BC_WIDTH].reshape(bsz, L, SSM_GROUPS, SSM_STATE)
    cm = xbc_c[..., SSM_WIDTH + BC_WIDTH:].reshape(bsz, L, SSM_GROUPS, SSM_STATE)
    dt = jax.nn.softplus(dt_raw.astype(jnp.float32) + dt_bias.astype(jnp.float32))
    a = -jnp.exp(a_log.astype(jnp.float32))
    h0 = ssm_h.astype(jnp.float32)
    if is_prompt:
        n_c = L // CHUNK
        to_chunks = lambda t: jnp.swapaxes(t.reshape((bsz, n_c, CHUNK) + t.shape[2:]), 0, 1)

        def step(carry, inp):
            y_c, carry = _ssd_chunk(carry, inp[0], inp[1], inp[2], inp[3], a)
            return carry, y_c

        h_new, ys = lax.scan(step, h0, (to_chunks(xs), to_chunks(dt), to_chunks(bm), to_chunks(cm)))
        y = jnp.swapaxes(ys, 0, 1).reshape(bsz, L, SSM_HEADS, SSM_HEAD_DIM)
    else:
        y, h_new = _ssd_chunk(h0, xs, dt, bm, cm, a)
    y = y + xs.astype(jnp.float32) * d_skip.astype(jnp.float32)[None, None, :, None]
    yz = y.reshape(bsz, L, SSM_WIDTH).astype(x.dtype) * jax.nn.silu(z)
    yg = yz.reshape(bsz, L, SSM_GROUPS, SSM_WIDTH // SSM_GROUPS).astype(jnp.float32)
    yg = yg * lax.rsqrt(jnp.mean(yg * yg, axis=-1, keepdims=True) + EPS)
    yz = (yg.reshape(bsz, L, SSM_WIDTH) * ssm_norm_w.astype(jnp.float32)).astype(x.dtype)
    branch_s = yz @ w_ssm_o

    merged = jax.nn.sigmoid(gate_a) * branch_a + jax.nn.sigmoid(gate_s) * branch_s
    out = merged @ w_out
    return x + _rms_norm(out, post_norm_w), new_k, new_v, new_buf, h_new


def setup_inputs(seed: int = 0) -> dict:
    key = jax.random.key(seed)
    ks = jax.random.split(key, 24)
    f32 = jnp.float32
    n = lambda k_, shape, s: jax.random.normal(k_, shape, f32) * s
    win_cache = min(WINDOW, PAST_LEN)
    dt0 = jnp.exp(jax.random.uniform(ks[10], (DEPTH, SSM_HEADS), f32, np.log(1e-3), np.log(1e-1)))
    return {
        'x_prompt': n(ks[0], (BATCH, SEQ, D_MODEL), 1.0),
        'x_sample': n(ks[1], (DEC_BATCH, DEC_SEQ, D_MODEL), 1.0),
        'cache_k': n(ks[2], (DEPTH, DEC_BATCH, win_cache, N_KV_HEADS, HEAD_DIM), 1.0),
        'cache_v': n(ks[3], (DEPTH, DEC_BATCH, win_cache, N_KV_HEADS, HEAD_DIM), 1.0),
        'state_conv': n(ks[4], (DEPTH, DEC_BATCH, CONV_W - 1, CONV_CH), 1.0),
        'state_ssm': n(ks[5], (DEPTH, DEC_BATCH, SSM_HEADS, SSM_HEAD_DIM, SSM_STATE), 0.5),
        'pre_norm_w': 1.0 + n(ks[6], (DEPTH, D_MODEL), 0.01),
        'w_in': n(ks[7], (DEPTH, D_MODEL, IN_WIDTH), D_MODEL ** -0.5),
        'conv_w': n(ks[8], (DEPTH, CONV_W, CONV_CH), CONV_W ** -0.5),
        'conv_b': n(ks[9], (DEPTH, CONV_CH), 0.01),
        'dt_bias': dt0 + jnp.log(-jnp.expm1(-dt0)),
        'a_log': jnp.log(jax.random.uniform(ks[11], (DEPTH, SSM_HEADS), f32, 1.0, 16.0)),
        'd_skip': 1.0 + n(ks[12], (DEPTH, SSM_HEADS), 0.1),
        'sinks': n(ks[13], (DEPTH, N_Q_HEADS), 0.5),
        'ssm_norm_w': 1.0 + n(ks[14], (DEPTH, SSM_WIDTH), 0.01),
        'w_attn_o': n(ks[15], (DEPTH, ATT_WIDTH, D_MODEL), ATT_WIDTH ** -0.5),
        'w_ssm_o': n(ks[16], (DEPTH, SSM_WIDTH, D_MODEL), SSM_WIDTH ** -0.5),
        'w_out': n(ks[17], (DEPTH, D_MODEL, D_MODEL), D_MODEL ** -0.5),
        'post_norm_w': 1.0 + n(ks[18], (DEPTH, D_MODEL), 0.01),
    }


def reference(x_prompt, x_sample, cache_k, cache_v, state_conv, state_ssm,
              pre_norm_w, w_in, conv_w, conv_b, dt_bias, a_log, d_skip, sinks,
              ssm_norm_w, w_attn_o, w_ssm_o, w_out, post_norm_w):
    bp, lp = x_prompt.shape[:2]
    ls = x_sample.shape[1]
    pos_p = jnp.arange(lp, dtype=jnp.float32)
    pos_s = PAST_LEN + jnp.arange(ls, dtype=jnp.float32)
    zero_buf = jnp.zeros((bp, CONV_W - 1, CONV_CH), x_prompt.dtype)
    zero_h = jnp.zeros((bp, SSM_HEADS, SSM_HEAD_DIM, SSM_STATE), jnp.float32)
    yp, ys = x_prompt, x_sample
    kp_l, vp_l, cp_l, sp_l, ks_l, vs_l, cs_l, ss_l = [], [], [], [], [], [], [], []
    for layer in range(DEPTH):
        wts = (pre_norm_w[layer], w_in[layer], conv_w[layer], conv_b[layer], dt_bias[layer],
               a_log[layer], d_skip[layer], sinks[layer], ssm_norm_w[layer], w_attn_o[layer],
               w_ssm_o[layer], w_out[layer], post_norm_w[layer])
        yp, kp, vp, cp, sp = _layer(yp, pos_p, True, None, None, zero_buf, zero_h, *wts)
        ys, ks_, vs, cs, ss = _layer(ys, pos_s, False, cache_k[layer], cache_v[layer],
                                     state_conv[layer], state_ssm[layer], *wts)
        kp_l.append(kp); vp_l.append(vp); cp_l.append(cp); sp_l.append(sp)
        ks_l.append(ks_); vs_l.append(vs); cs_l.append(cs); ss_l.append(ss)
    return (yp, ys,
            jnp.stack(kp_l), jnp.stack(vp_l), jnp.stack(cp_l), jnp.stack(sp_l),
            jnp.stack(ks_l), jnp.stack(vs_l), jnp.stack(cs_l), jnp.stack(ss_l))
```

```python
import functools

import jax
import jax.numpy as jnp
from jax import lax
from jax.experimental import pallas as pl
from jax.experimental.pallas import tpu as pltpu

F32 = jnp.float32
BF16 = jnp.bfloat16

D_MODEL = 1024
PAST_LEN = 2048
CHUNK = 64
N_Q_HEADS = 16
N_KV_HEADS = 4
HEAD_DIM = 64
Q_PER_KV = N_Q_HEADS // N_KV_HEADS
ATT_WIDTH = N_Q_HEADS * HEAD_DIM
KV_WIDTH = N_KV_HEADS * HEAD_DIM
WINDOW = 128
ROT_DIM = HEAD_DIM // 4
ROPE_THETA = 500000.0
SSM_WIDTH = 2 * D_MODEL
SSM_HEAD_DIM = 64
SSM_HEADS = SSM_WIDTH // SSM_HEAD_DIM
SSM_GROUPS = 4
SSM_STATE = 128
GROUP_WIDTH = SSM_WIDTH // SSM_GROUPS
BC_WIDTH = SSM_GROUPS * SSM_STATE
CONV_W = 4
CONV_CH = SSM_WIDTH + 2 * BC_WIDTH
EPS = 1e-6

LANES = 128
SUBLANES = 8
VMEM_LIMIT = 56 * 1024 * 1024

OFF_Q = 0
OFF_KV = OFF_Q + ATT_WIDTH
OFF_G = OFF_KV + 2 * KV_WIDTH
OFF_Z = OFF_G + ATT_WIDTH
OFF_XBC = OFF_Z + SSM_WIDTH
OFF_GATES = OFF_XBC + CONV_CH
OFF_DT = OFF_GATES + 2 * D_MODEL
IN_WIDTH_PADDED = OFF_DT + LANES
MM_COLS = 512


def _sigmoid(x):
    return 1.0 / (1.0 + jnp.exp(-x))


def _silu(x):
    return x * _sigmoid(x)


def _softplus(x):
    return jnp.maximum(x, 0.0) + jnp.log1p(jnp.exp(-jnp.abs(x)))


def _compiler_params(semantics):
    return pltpu.CompilerParams(dimension_semantics=semantics, vmem_limit_bytes=VMEM_LIMIT)


def _resident(shape):
    zeros = (0,) * len(shape)
    return pl.BlockSpec(shape, lambda *_: zeros, pipeline_mode=pl.Buffered(1))


def _inproj_kernel(x_ref, nw_ref, w_ref, dtb_ref, q_ref, kv_ref, g_ref, z_ref, xbc_ref, gates_ref, dt_ref):
    x = x_ref[...]
    ms = jnp.mean(x * x, axis=-1, keepdims=True)
    h = (x * lax.rsqrt(ms + EPS) * nw_ref[...]).astype(BF16)

    def section(out_ref, off, width, act):
        step = min(MM_COLS, width)
        for n0 in range(0, width, step):
            r = jnp.dot(h, w_ref[:, off + n0:off + n0 + step], preferred_element_type=F32)
            out_ref[:, n0:n0 + step] = act(r)

    ident = lambda r: r
    section(q_ref, OFF_Q, ATT_WIDTH, ident)
    section(kv_ref, OFF_KV, 2 * KV_WIDTH, ident)
    section(g_ref, OFF_G, ATT_WIDTH, _silu)
    section(z_ref, OFF_Z, SSM_WIDTH, _silu)
    section(xbc_ref, OFF_XBC, CONV_CH, ident)
    section(gates_ref, OFF_GATES, 2 * D_MODEL, _sigmoid)
    lane = lax.broadcasted_iota(jnp.int32, (1, LANES), 1)
    section(dt_ref, OFF_DT, LANES,
            lambda r: jnp.where(lane < SSM_HEADS, _softplus(r + dtb_ref[...]), 0.0))


def _inproj(x2d, nw, w_cat, dtb, tm):
    m = x2d.shape[0]
    widths = (ATT_WIDTH, 2 * KV_WIDTH, ATT_WIDTH, SSM_WIDTH, CONV_CH, 2 * D_MODEL, LANES)
    return pl.pallas_call(
        _inproj_kernel,
        grid=(m // tm,),
        in_specs=[pl.BlockSpec((tm, D_MODEL), lambda i: (i, 0)),
                  _resident((1, D_MODEL)),
                  _resident((D_MODEL, IN_WIDTH_PADDED)),
                  _resident((1, LANES))],
        out_specs=[pl.BlockSpec((tm, w), lambda i: (i, 0)) for w in widths],
        out_shape=[jax.ShapeDtypeStruct((m, w), F32) for w in widths],
        compiler_params=_compiler_params(("arbitrary",)),
        name="inproj",
    )(x2d, nw, w_cat, dtb)


def _attn_kernel(q_ref, kv_ref, g_ref, ck_ref, cv_ref, cos_ref, s1_ref, s2_ref, sink_ref, wo_ref,
                 o_ref, nk_ref, nv_ref, qbuf, kbuf, vbuf, att, *, tb, ch, nblk, mask_first, n_keep):
    blk = pl.program_id(1)

    @pl.when(blk == 0)
    def _():
        kbuf[0:WINDOW, :] = ck_ref[0]
        vbuf[0:WINDOW, :] = cv_ref[0]

    cos = cos_ref[...]
    s1 = s1_ref[...]
    s2 = s2_ref[...]

    def rope(x):
        return x * cos + pltpu.roll(x, LANES - ROT_DIM // 2, 1) * s1 + pltpu.roll(x, ROT_DIM // 2, 1) * s2

    scale = HEAD_DIM ** -0.5
    for s in range(ATT_WIDTH // LANES):
        sl = slice(s * LANES, (s + 1) * LANES)
        qbuf[:, sl] = (rope(q_ref[:, sl]) * scale).astype(BF16)
    for s in range(KV_WIDTH // LANES):
        sl = slice(s * LANES, (s + 1) * LANES)
        kbuf[WINDOW:WINDOW + tb, sl] = rope(kv_ref[:, sl])
    vbuf[WINDOW:WINDOW + tb, :] = kv_ref[:, KV_WIDTH:2 * KV_WIDTH]

    nkeys = WINDOW + ch
    for c in range(tb // ch):
        r0 = c * ch
        if mask_first:
            col_chunk = lax.broadcasted_iota(jnp.int32, (1, nkeys), 1) // CHUNK
            valid = (blk * (tb // ch) + c - WINDOW // CHUNK + col_chunk) >= 0
        for h in range(N_KV_HEADS):
            hs = slice(h * HEAD_DIM, (h + 1) * HEAD_DIM)
            qs = jnp.concatenate(
                [qbuf[r0:r0 + ch, (h * Q_PER_KV + g) * HEAD_DIM:(h * Q_PER_KV + g + 1) * HEAD_DIM]
                 for g in range(Q_PER_KV)], axis=0)
            kb = kbuf[r0:r0 + nkeys, hs].astype(BF16)
            vb = vbuf[r0:r0 + nkeys, hs].astype(BF16)
            s = lax.dot_general(qs, kb, (((1,), (1,)), ((), ())), preferred_element_type=F32)
            if mask_first:
                s = jnp.where(valid, s, -jnp.inf)
            sk = sink_ref[h][:, 0:1]
            m = jnp.maximum(jnp.max(s, axis=-1, keepdims=True), sk)
            p = jnp.exp(s - m)
            denom = jnp.sum(p, axis=-1, keepdims=True) + jnp.exp(sk - m)
            o = jnp.dot(p.astype(BF16), vb, preferred_element_type=F32) / denom
            for g in range(Q_PER_KV):
                head = h * Q_PER_KV + g
                att[r0:r0 + ch, head * HEAD_DIM:(head + 1) * HEAD_DIM] = o[g * ch:(g + 1) * ch, :]

    a = (att[...] * g_ref[...]).astype(BF16)
    o_ref[...] = jnp.dot(a, wo_ref[...], preferred_element_type=F32)

    @pl.when(blk == nblk - 1)
    def _():
        nk_ref[0] = kbuf[WINDOW + tb - n_keep:WINDOW + tb, :]
        nv_ref[0] = vbuf[WINDOW + tb - n_keep:WINDOW + tb, :]

    if nblk > 1:
        kbuf[0:WINDOW, :] = kbuf[tb:tb + WINDOW, :]
        vbuf[0:WINDOW, :] = vbuf[tb:tb + WINDOW, :]


def _attention(q, kv, g, cache_k, cache_v, tables, sink_cols, wo, *, bsz, seq, tb, ch, mask_first):
    nblk = seq // tb
    n_keep = min(WINDOW, seq)
    cos, s1, s2 = tables
    row = lambda b, j: (b * nblk + j, 0)
    kern = functools.partial(_attn_kernel, tb=tb, ch=ch, nblk=nblk, mask_first=mask_first, n_keep=n_keep)
    return pl.pallas_call(
        kern,
        grid=(bsz, nblk),
        in_specs=[pl.BlockSpec((tb, ATT_WIDTH), row),
                  pl.BlockSpec((tb, 2 * KV_WIDTH), row),
                  pl.BlockSpec((tb, ATT_WIDTH), row),
                  pl.BlockSpec((1, WINDOW, KV_WIDTH), lambda b, j: (b, 0, 0)),
                  pl.BlockSpec((1, WINDOW, KV_WIDTH), lambda b, j: (b, 0, 0)),
                  pl.BlockSpec((tb, LANES), lambda b, j: (j, 0)),
                  pl.BlockSpec((tb, LANES), lambda b, j: (j, 0)),
                  pl.BlockSpec((tb, LANES), lambda b, j: (j, 0)),
                  _resident((N_KV_HEADS, Q_PER_KV * ch, LANES)),
                  _resident((ATT_WIDTH, D_MODEL))],
        out_specs=[pl.BlockSpec((tb, D_MODEL), row),
                   pl.BlockSpec((1, n_keep, KV_WIDTH), lambda b, j: (b, 0, 0)),
                   pl.BlockSpec((1, n_keep, KV_WIDTH), lambda b, j: (b, 0, 0))],
        out_shape=[jax.ShapeDtypeStruct((bsz * seq, D_MODEL), F32),
                   jax.ShapeDtypeStruct((bsz, n_keep, KV_WIDTH), F32),
                   jax.ShapeDtypeStruct((bsz, n_keep, KV_WIDTH), F32)],
        scratch_shapes=[pltpu.VMEM((tb, ATT_WIDTH), BF16),
                        pltpu.VMEM((WINDOW + tb, KV_WIDTH), F32),
                        pltpu.VMEM((WINDOW + tb, KV_WIDTH), F32),
                        pltpu.VMEM((tb, ATT_WIDTH), F32)],
        compiler_params=_compiler_params(("arbitrary", "arbitrary")),
        name="attention",
    )(q, kv, g, cache_k, cache_v, cos, s1, s2, sink_cols, wo)


SSD_Q = 64
PIECE = 32
QUAD = 4 * SSM_HEAD_DIM


def _split3(x):
    hi = x.astype(BF16).astype(F32)
    r1 = x - hi
    mid = r1.astype(BF16).astype(F32)
    lo = (r1 - mid).astype(BF16).astype(F32)
    return (hi + pltpu.roll(mid, PIECE, 1) + pltpu.roll(lo, 2 * PIECE, 1)).astype(BF16)


def _ssd_kernel(xbc_ref, z_ref, dt_ref, cs0_ref, h0_ref, cw_ref, cb_ref, a_ref, e3_ref, dx_ref, nw_ref, wso_ref,
                o_ref, nconv_ref, nssm_ref, xpad, xc, ht, ybuf, *, tc, nblk, out_blk, out_row):
    blk = pl.program_id(1)
    pad0 = SUBLANES - (CONV_W - 1)

    @pl.when(blk == 0)
    def _():
        xpad[pad0:SUBLANES, :] = cs0_ref[0]
        ht[...] = h0_ref[0].T

    xpad[SUBLANES:SUBLANES + tc, :] = xbc_ref[...]
    acc = cb_ref[...] + xpad[pad0:pad0 + tc, :] * cw_ref[0:1, :]
    for tap in range(1, CONV_W):
        acc = acc + xpad[pad0 + tap:pad0 + tap + tc, :] * cw_ref[tap:tap + 1, :]
    xc[...] = _silu(acc)

    @pl.when(blk == out_blk)
    def _():
        nconv_ref[0] = xpad[SUBLANES + out_row - (CONV_W - 1):SUBLANES + out_row, :]

    xpad[pad0:SUBLANES, :] = xpad[pad0 + tc:SUBLANES + tc, :]

    lane = lax.broadcasted_iota(jnp.int32, (SSD_Q, LANES), 1)
    ri = lax.broadcasted_iota(jnp.int32, (SSD_Q, SSD_Q), 0)
    rj = lax.broadcasted_iota(jnp.int32, (SSD_Q, SSD_Q), 1)
    tri = (ri >= rj).astype(BF16)
    ii = lax.broadcasted_iota(jnp.int32, (SSD_Q, SSM_WIDTH), 0)
    jj = lax.broadcasted_iota(jnp.int32, (SSD_Q, SSM_WIDTH), 1) % SSD_Q
    bi = lax.broadcasted_iota(jnp.int32, (QUAD, QUAD), 0) // SSM_HEAD_DIM
    bj = lax.broadcasted_iota(jnp.int32, (QUAD, QUAD), 1) // SSM_HEAD_DIM
    a_row = a_ref[...]
    dx = dx_ref[...]
    nw = nw_ref[...]

    def expand(v):
        return jnp.dot(_split3(v), e3_ref[...], preferred_element_type=F32)

    def chunk(c, carry):
        r0 = pl.multiple_of(c * SSD_Q, SSD_Q)
        xs = xc[pl.ds(r0, SSD_Q), 0:SSM_WIDTH]
        bm = xc[pl.ds(r0, SSD_Q), SSM_WIDTH:SSM_WIDTH + BC_WIDTH].astype(BF16)
        cm = xc[pl.ds(r0, SSD_Q), SSM_WIDTH + BC_WIDTH:CONV_CH].astype(BF16)
        dt = dt_ref[pl.ds(r0, SSD_Q), :]
        cum3 = jnp.dot(tri, _split3(dt * a_row), preferred_element_type=F32)
        cum = jnp.where(lane < PIECE,
                        cum3 + pltpu.roll(cum3, LANES - PIECE, 1) + pltpu.roll(cum3, LANES - 2 * PIECE, 1), 0.0)
        dt_x = expand(dt)
        cum_x = expand(cum)
        ecum_x = jnp.exp(cum_x)
        cum_last = cum_x[SSD_Q - 1:SSD_Q, :]
        xdt = xs * dt_x
        xw = (xdt * jnp.exp(cum_last - cum_x)).astype(BF16)
        cum_j = jnp.sum(jnp.where(ii == jj, cum_x, 0.0), axis=0, keepdims=True)
        decay = jnp.exp(jnp.where(ii >= jj, cum_x - cum_j, -jnp.inf))

        y_parts = []
        for g in range(SSM_GROUPS):
            cg = cm[:, g * SSM_STATE:(g + 1) * SSM_STATE]
            bg = bm[:, g * SSM_STATE:(g + 1) * SSM_STATE]
            cb4 = lax.dot_general(cg, jnp.concatenate([bg] * 4, axis=0), (((1,), (1,)), ((), ())),
                                  preferred_element_type=F32)
            gs = slice(g * GROUP_WIDTH, (g + 1) * GROUP_WIDTH)
            y_state = jnp.dot(cg, ht[:, gs].astype(BF16), preferred_element_type=F32) * ecum_x[:, gs]
            for half in range(GROUP_WIDTH // QUAD):
                qs = slice(g * GROUP_WIDTH + half * QUAD, g * GROUP_WIDTH + (half + 1) * QUAD)
                m4 = (cb4 * decay[:, qs]).astype(BF16)
                x4 = xdt[:, qs]
                bd = jnp.where(bi == bj, jnp.concatenate([x4] * 4, axis=0), 0.0).astype(BF16)
                y_parts.append(jnp.dot(m4, bd, preferred_element_type=F32)
                               + y_state[:, half * QUAD:(half + 1) * QUAD])
            ht[:, gs] = (ht[:, gs] * ecum_x[SSD_Q - 1:SSD_Q, gs]
                         + jnp.dot(bg.T, xw[:, gs], preferred_element_type=F32))
        y = jnp.concatenate(y_parts, axis=1) + xs * dx
        yz = y * z_ref[pl.ds(r0, SSD_Q), :]
        for g in range(SSM_GROUPS):
            gs = slice(g * GROUP_WIDTH, (g + 1) * GROUP_WIDTH)
            yg = yz[:, gs]
            ms = jnp.mean(yg * yg, axis=-1, keepdims=True)
            ybuf[pl.ds(r0, SSD_Q), gs] = (yg * lax.rsqrt(ms + EPS) * nw[:, gs]).astype(BF16)
        return carry

    lax.fori_loop(0, tc // SSD_Q, chunk, 0)
    o_ref[...] = jnp.dot(ybuf[...], wso_ref[...], preferred_element_type=F32)

    @pl.when(blk == nblk - 1)
    def _():
        nssm_ref[0] = ht[...].T


def _ssd(xbc, zs, dt, conv0, ssm0, cw, cb, a_row, e3, dx, nw, wso, *, bsz, seq, tc, l_real):
    nblk = seq // tc
    out_blk = (l_real - 1) // tc
    out_row = l_real - out_blk * tc
    row = lambda b, j: (b * nblk + j, 0)
    hp = SSM_HEADS * SSM_HEAD_DIM
    kern = functools.partial(_ssd_kernel, tc=tc, nblk=nblk, out_blk=out_blk, out_row=out_row)
    return pl.pallas_call(
        kern,
        grid=(bsz, nblk),
        in_specs=[pl.BlockSpec((tc, CONV_CH), row),
                  pl.BlockSpec((tc, SSM_WIDTH), row),
                  pl.BlockSpec((tc, LANES), row),
                  pl.BlockSpec((1, CONV_W - 1, CONV_CH), lambda b, j: (b, 0, 0)),
                  pl.BlockSpec((1, hp, SSM_STATE), lambda b, j: (b, 0, 0)),
                  _resident((CONV_W, CONV_CH)),
                  _resident((1, CONV_CH)),
                  _resident((1, LANES)),
                  _resident((LANES, SSM_WIDTH)),
                  _resident((1, SSM_WIDTH)),
                  _resident((1, SSM_WIDTH)),
                  _resident((SSM_WIDTH, D_MODEL))],
        out_specs=[pl.BlockSpec((tc, D_MODEL), row),
                   pl.BlockSpec((1, CONV_W - 1, CONV_CH), lambda b, j: (b, 0, 0)),
                   pl.BlockSpec((1, hp, SSM_STATE), lambda b, j: (b, 0, 0))],
        out_shape=[jax.ShapeDtypeStruct((bsz * seq, D_MODEL), F32),
                   jax.ShapeDtypeStruct((bsz, CONV_W - 1, CONV_CH), F32),
                   jax.ShapeDtypeStruct((bsz, hp, SSM_STATE), F32)],
        scratch_shapes=[pltpu.VMEM((SUBLANES + tc, CONV_CH), F32),
                        pltpu.VMEM((tc, CONV_CH), F32),
                        pltpu.VMEM((SSM_STATE, SSM_WIDTH), F32),
                        pltpu.VMEM((tc, SSM_WIDTH), BF16)],
        compiler_params=_compiler_params(("arbitrary", "arbitrary")),
        name="ssd",
    )(xbc, zs, dt, conv0, ssm0, cw, cb, a_row, e3, dx, nw, wso)


def _merge_kernel(x_ref, ba_ref, bs_ref, gates_ref, wout_ref, nw_ref, y_ref):
    merged = gates_ref[:, 0:D_MODEL] * ba_ref[...] + gates_ref[:, D_MODEL:2 * D_MODEL] * bs_ref[...]
    out = jnp.dot(merged.astype(BF16), wout_ref[...], preferred_element_type=F32)
    ms = jnp.mean(out * out, axis=-1, keepdims=True)
    y_ref[...] = x_ref[...] + out * lax.rsqrt(ms + EPS) * nw_ref[...]


def _merge(x2d, ba, bs, gates, wout, nw, tm):
    m = x2d.shape[0]
    row = lambda i: (i, 0)
    return pl.pallas_call(
        _merge_kernel,
        grid=(m // tm,),
        in_specs=[pl.BlockSpec((tm, D_MODEL), row),
                  pl.BlockSpec((tm, D_MODEL), row),
                  pl.BlockSpec((tm, D_MODEL), row),
                  pl.BlockSpec((tm, 2 * D_MODEL), row),
                  _resident((D_MODEL, D_MODEL)),
                  _resident((1, D_MODEL))],
        out_specs=pl.BlockSpec((tm, D_MODEL), row),
        out_shape=jax.ShapeDtypeStruct((m, D_MODEL), F32),
        compiler_params=_compiler_params(("arbitrary",)),
        name="merge",
    )(x2d, ba, bs, gates, wout, nw)


def _rope_tables(pos):
    half = ROT_DIM // 2
    inv = ROPE_THETA ** (-(jnp.arange(half, dtype=F32) * 2.0 / ROT_DIM))
    ang = pos.astype(F32)[:, None] * inv[None, :]
    cos, sin = jnp.cos(ang), jnp.sin(ang)
    n = pos.shape[0]
    rest = HEAD_DIM - ROT_DIM
    c = jnp.concatenate([cos, cos, jnp.ones((n, rest), F32)], axis=1)
    s1 = jnp.concatenate([-sin, jnp.zeros((n, half + rest), F32)], axis=1)
    s2 = jnp.concatenate([jnp.zeros((n, half), F32), sin, jnp.zeros((n, rest), F32)], axis=1)
    rep = LANES // HEAD_DIM
    return tuple(jnp.tile(t, (1, rep)) for t in (c, s1, s2))


def _layer(x, pos, cache_k, cache_v, conv0, ssm0, wts, *, mask_first, tm, tb, ch, tc):
    bsz, seq, _ = x.shape
    x2d = x.reshape(bsz * seq, D_MODEL)
    q, kv, g, zs, xbc, gates, dt = _inproj(x2d, wts["pre_norm_w"], wts["w_cat"], wts["dt_bias"], tm)

    sink_cols = jnp.broadcast_to(
        jnp.repeat(wts["sinks"].reshape(N_KV_HEADS, Q_PER_KV), ch, axis=1)[:, :, None],
        (N_KV_HEADS, Q_PER_KV * ch, LANES))
    branch_a, new_k, new_v = _attention(
        q, kv, g, cache_k, cache_v, _rope_tables(pos), sink_cols, wts["w_attn_o"],
        bsz=bsz, seq=seq, tb=tb, ch=ch, mask_first=mask_first)

    seq_p = -(-seq // SSD_Q) * SSD_Q
    if seq_p != seq:
        padrows = lambda t: jnp.pad(t.reshape(bsz, seq, -1), ((0, 0), (0, seq_p - seq), (0, 0))).reshape(
            bsz * seq_p, -1)
        xbc_p, zs_p, dt_p = padrows(xbc), padrows(zs), padrows(dt)
    else:
        xbc_p, zs_p, dt_p = xbc, zs, dt
    branch_s, new_conv, new_ssm = _ssd(
        xbc_p, zs_p, dt_p, conv0, ssm0, wts["conv_w"], wts["conv_b"], wts["a_row"], wts["e3"], wts["d_x"],
        wts["ssm_norm_w"], wts["w_ssm_o"], bsz=bsz, seq=seq_p, tc=min(tc, seq_p), l_real=seq)
    if seq_p != seq:
        branch_s = branch_s.reshape(bsz, seq_p, D_MODEL)[:, :seq].reshape(bsz * seq, D_MODEL)

    y = _merge(x2d, branch_a, branch_s, gates, wts["w_out"], wts["post_norm_w"], tm)
    n_keep = new_k.shape[1]
    return (y.reshape(bsz, seq, D_MODEL),
            new_k.reshape(bsz, n_keep, N_KV_HEADS, HEAD_DIM),
            new_v.reshape(bsz, n_keep, N_KV_HEADS, HEAD_DIM),
            new_conv,
            new_ssm.reshape(bsz, SSM_HEADS, SSM_HEAD_DIM, SSM_STATE))


def _prep_weights(pre_norm_w, w_in, conv_w, conv_b, dt_bias, a_log, d_skip, sinks, ssm_norm_w, w_attn_o,
                  w_ssm_o, w_out, post_norm_w):
    dt_off = 2 * ATT_WIDTH + 2 * KV_WIDTH + SSM_WIDTH + CONV_CH
    w_cat = jnp.concatenate(
        [w_in[:, :dt_off], w_in[:, dt_off + SSM_HEADS:], w_in[:, dt_off:dt_off + SSM_HEADS],
         jnp.zeros((D_MODEL, LANES - SSM_HEADS), w_in.dtype)], axis=1).astype(BF16)
    pad_heads = lambda v: jnp.pad(v.astype(F32), (0, LANES - SSM_HEADS)).reshape(1, LANES)
    k_idx = jnp.arange(LANES)[:, None]
    c_idx = jnp.arange(SSM_WIDTH)[None, :]
    e3 = ((k_idx < 3 * PIECE) & (k_idx % PIECE == c_idx // SSM_HEAD_DIM)).astype(BF16)
    return dict(
        pre_norm_w=pre_norm_w.reshape(1, D_MODEL), w_cat=w_cat, dt_bias=pad_heads(dt_bias),
        conv_w=conv_w, conv_b=conv_b.reshape(1, CONV_CH), a_row=pad_heads(-jnp.exp(a_log.astype(F32))),
        e3=e3, d_x=jnp.repeat(d_skip.astype(F32), SSM_HEAD_DIM).reshape(1, SSM_WIDTH), sinks=sinks.astype(F32),
        ssm_norm_w=ssm_norm_w.reshape(1, SSM_WIDTH), w_attn_o=w_attn_o.astype(BF16),
        w_ssm_o=w_ssm_o.astype(BF16), w_out=w_out.astype(BF16), post_norm_w=post_norm_w.reshape(1, D_MODEL))


def kernel(x_prompt, x_sample, cache_k, cache_v, state_conv, state_ssm, pre_norm_w, w_in, conv_w, conv_b, dt_bias,
           a_log, d_skip, sinks, ssm_norm_w, w_attn_o, w_ssm_o, w_out, post_norm_w):
    depth = w_in.shape[0]
    bp, lp, _ = x_prompt.shape
    bs, ls, _ = x_sample.shape
    pos_p = jnp.arange(lp, dtype=F32)
    pos_s = PAST_LEN + jnp.arange(ls, dtype=F32)
    hp = SSM_HEADS * SSM_HEAD_DIM
    zero_kv = jnp.zeros((bp, WINDOW, KV_WIDTH), F32)
    zero_conv = jnp.zeros((bp, CONV_W - 1, CONV_CH), F32)
    zero_ssm = jnp.zeros((bp, hp, SSM_STATE), F32)
    yp, ys = x_prompt, x_sample
    outs = [[] for _ in range(8)]
    for layer in range(depth):
        wts = _prep_weights(pre_norm_w[layer], w_in[layer], conv_w[layer], conv_b[layer], dt_bias[layer],
                            a_log[layer], d_skip[layer], sinks[layer], ssm_norm_w[layer], w_attn_o[layer],
                            w_ssm_o[layer], w_out[layer], post_norm_w[layer])
        yp, kp, vp, cp, sp = _layer(yp, pos_p, zero_kv, zero_kv, zero_conv, zero_ssm, wts,
                                    mask_first=True, tm=256, tb=256, ch=CHUNK, tc=256)
        ys, ks, vs, cs, ss = _layer(ys, pos_s,
                                    cache_k[layer].reshape(bs, WINDOW, KV_WIDTH),
                                    cache_v[layer].reshape(bs, WINDOW, KV_WIDTH),
                                    state_conv[layer], state_ssm[layer].reshape(bs, hp, SSM_STATE), wts,
                                    mask_first=False, tm=256, tb=ls, ch=ls, tc=SSD_Q)
        for lst, val in zip(outs, (kp, vp, cp, sp, ks, vs, cs, ss)):
            lst.append(val)
    return (yp, ys) + tuple(jnp.stack(lst) for lst in outs)
```

```python
import functools

import jax
import jax.numpy as jnp
from jax import lax
from jax.experimental import pallas as pl
from jax.experimental.pallas import tpu as pltpu

F32 = jnp.float32
BF16 = jnp.bfloat16

D_MODEL = 1024
PAST_LEN = 2048
CHUNK = 64
N_Q_HEADS = 16
N_KV_HEADS = 4
HEAD_DIM = 64
Q_PER_KV = N_Q_HEADS // N_KV_HEADS
ATT_WIDTH = N_Q_HEADS * HEAD_DIM
KV_WIDTH = N_KV_HEADS * HEAD_DIM
WINDOW = 128
ROT_DIM = HEAD_DIM // 4
ROPE_THETA = 500000.0
SSM_WIDTH = 2 * D_MODEL
SSM_HEAD_DIM = 64
SSM_HEADS = SSM_WIDTH // SSM_HEAD_DIM
SSM_GROUPS = 4
SSM_STATE = 128
GROUP_WIDTH = SSM_WIDTH // SSM_GROUPS
BC_WIDTH = SSM_GROUPS * SSM_STATE
CONV_W = 4
CONV_CH = SSM_WIDTH + 2 * BC_WIDTH
EPS = 1e-6

LANES = 128
SUBLANES = 8
VMEM_LIMIT = 56 * 1024 * 1024

OFF_Q = 0
OFF_KV = OFF_Q + ATT_WIDTH
OFF_G = OFF_KV + 2 * KV_WIDTH
OFF_Z = OFF_G + ATT_WIDTH
OFF_XBC = OFF_Z + SSM_WIDTH
OFF_GATES = OFF_XBC + CONV_CH
OFF_DT = OFF_GATES + 2 * D_MODEL
IN_WIDTH_PADDED = OFF_DT + LANES
MM_COLS = 512


def _sigmoid(x):
    return 1.0 / (1.0 + jnp.exp(-x))


def _silu(x):
    return x * _sigmoid(x)


def _softplus(x):
    return jnp.maximum(x, 0.0) + jnp.log1p(jnp.exp(-jnp.abs(x)))


def _compiler_params(semantics):
    return pltpu.CompilerParams(dimension_semantics=semantics, vmem_limit_bytes=VMEM_LIMIT)


def _resident(shape):
    zeros = (0,) * len(shape)
    return pl.BlockSpec(shape, lambda *_: zeros, pipeline_mode=pl.Buffered(1))


def _inproj_kernel(x_ref, nw_ref, w_ref, dtb_ref, q_ref, kv_ref, g_ref, z_ref, xbc_ref, gates_ref, dt_ref):
    x = x_ref[...]
    ms = jnp.mean(x * x, axis=-1, keepdims=True)
    h = (x * lax.rsqrt(ms + EPS) * nw_ref[...]).astype(BF16)

    def section(out_ref, off, width, act):
        step = min(MM_COLS, width)
        for n0 in range(0, width, step):
            r = jnp.dot(h, w_ref[:, off + n0:off + n0 + step], preferred_element_type=F32)
            out_ref[:, n0:n0 + step] = act(r)

    ident = lambda r: r
    section(q_ref, OFF_Q, ATT_WIDTH, ident)
    section(kv_ref, OFF_KV, 2 * KV_WIDTH, ident)
    section(g_ref, OFF_G, ATT_WIDTH, _silu)
    section(z_ref, OFF_Z, SSM_WIDTH, _silu)
    section(xbc_ref, OFF_XBC, CONV_CH, ident)
    section(gates_ref, OFF_GATES, 2 * D_MODEL, _sigmoid)
    lane = lax.broadcasted_iota(jnp.int32, (1, LANES), 1)
    section(dt_ref, OFF_DT, LANES,
            lambda r: jnp.where(lane < SSM_HEADS, _softplus(r + dtb_ref[...]), 0.0))


def _inproj(x2d, nw, w_cat, dtb, tm):
    m = x2d.shape[0]
    widths = (ATT_WIDTH, 2 * KV_WIDTH, ATT_WIDTH, SSM_WIDTH, CONV_CH, 2 * D_MODEL, LANES)
    return pl.pallas_call(
        _inproj_kernel,
        grid=(m // tm,),
        in_specs=[pl.BlockSpec((tm, D_MODEL), lambda i: (i, 0)),
                  _resident((1, D_MODEL)),
                  _resident((D_MODEL, IN_WIDTH_PADDED)),
                  _resident((1, LANES))],
        out_specs=[pl.BlockSpec((tm, w), lambda i: (i, 0)) for w in widths],
        out_shape=[jax.ShapeDtypeStruct((m, w), F32) for w in widths],
        compiler_params=_compiler_params(("arbitrary",)),
        name="inproj",
    )(x2d, nw, w_cat, dtb)


def _attn_kernel(q_ref, kv_ref, g_ref, ck_ref, cv_ref, cos_ref, s1_ref, s2_ref, sink_ref, wo_ref,
                 o_ref, nk_ref, nv_ref, qbuf, kbuf, vbuf, att, *, tb, ch, nblk, mask_first, n_keep):
    blk = pl.program_id(1)

    @pl.when(blk == 0)
    def _():
        kbuf[0:WINDOW, :] = ck_ref[0]
        vbuf[0:WINDOW, :] = cv_ref[0]

    cos = cos_ref[...]
    s1 = s1_ref[...]
    s2 = s2_ref[...]

    def rope(x):
        return x * cos + pltpu.roll(x, LANES - ROT_DIM // 2, 1) * s1 + pltpu.roll(x, ROT_DIM // 2, 1) * s2

    scale = HEAD_DIM ** -0.5
    for s in range(ATT_WIDTH // LANES):
        sl = slice(s * LANES, (s + 1) * LANES)
        qbuf[:, sl] = (rope(q_ref[:, sl]) * scale).astype(BF16)
    for s in range(KV_WIDTH // LANES):
        sl = slice(s * LANES, (s + 1) * LANES)
        kbuf[WINDOW:WINDOW + tb, sl] = rope(kv_ref[:, sl])
    vbuf[WINDOW:WINDOW + tb, :] = kv_ref[:, KV_WIDTH:2 * KV_WIDTH]

    nkeys = WINDOW + ch
    for c in range(tb // ch):
        r0 = c * ch
        if mask_first:
            col_chunk = lax.broadcasted_iota(jnp.int32, (1, nkeys), 1) // CHUNK
            valid = (blk * (tb // ch) + c - WINDOW // CHUNK + col_chunk) >= 0
        for h in range(N_KV_HEADS):
            hs = slice(h * HEAD_DIM, (h + 1) * HEAD_DIM)
            qs = jnp.concatenate(
                [qbuf[r0:r0 + ch, (h * Q_PER_KV + g) * HEAD_DIM:(h * Q_PER_KV + g + 1) * HEAD_DIM]
                 for g in range(Q_PER_KV)], axis=0)
            kb = kbuf[r0:r0 + nkeys, hs].astype(BF16)
            vb = vbuf[r0:r0 + nkeys, hs].astype(BF16)
            s = lax.dot_general(qs, kb, (((1,), (1,)), ((), ())), preferred_element_type=F32)
            if mask_first:
                s = jnp.where(valid, s, -jnp.inf)
            sk = sink_ref[h][:, 0:1]
            m = jnp.maximum(jnp.max(s, axis=-1, keepdims=True), sk)
            p = jnp.exp(s - m)
            denom = jnp.sum(p, axis=-1, keepdims=True) + jnp.exp(sk - m)
            o = jnp.dot(p.astype(BF16), vb, preferred_element_type=F32) / denom
            for g in range(Q_PER_KV):
                head = h * Q_PER_KV + g
                att[r0:r0 + ch, head * HEAD_DIM:(head + 1) * HEAD_DIM] = o[g * ch:(g + 1) * ch, :]

    a = (att[...] * g_ref[...]).astype(BF16)
    o_ref[...] = jnp.dot(a, wo_ref[...], preferred_element_type=F32)

    @pl.when(blk == nblk - 1)
    def _():
        nk_ref[0] = kbuf[WINDOW + tb - n_keep:WINDOW + tb, :]
        nv_ref[0] = vbuf[WINDOW + tb - n_keep:WINDOW + tb, :]

    if nblk > 1:
        kbuf[0:WINDOW, :] = kbuf[tb:tb + WINDOW, :]
        vbuf[0:WINDOW, :] = vbuf[tb:tb + WINDOW, :]


def _attention(q, kv, g, cache_k, cache_v, tables, sink_cols, wo, *, bsz, seq, tb, ch, mask_first):
    nblk = seq // tb
    n_keep = min(WINDOW, seq)
    cos, s1, s2 = tables
    row = lambda b, j: (b * nblk + j, 0)
    kern = functools.partial(_attn_kernel, tb=tb, ch=ch, nblk=nblk, mask_first=mask_first, n_keep=n_keep)
    return pl.pallas_call(
        kern,
        grid=(bsz, nblk),
        in_specs=[pl.BlockSpec((tb, ATT_WIDTH), row),
                  pl.BlockSpec((tb, 2 * KV_WIDTH), row),
                  pl.BlockSpec((tb, ATT_WIDTH), row),
                  pl.BlockSpec((1, WINDOW, KV_WIDTH), lambda b, j: (b, 0, 0)),
                  pl.BlockSpec((1, WINDOW, KV_WIDTH), lambda b, j: (b, 0, 0)),
                  pl.BlockSpec((tb, LANES), lambda b, j: (j, 0)),
                  pl.BlockSpec((tb, LANES), lambda b, j: (j, 0)),
                  pl.BlockSpec((tb, LANES), lambda b, j: (j, 0)),
                  _resident((N_KV_HEADS, Q_PER_KV * ch, LANES)),
                  _resident((ATT_WIDTH, D_MODEL))],
        out_specs=[pl.BlockSpec((tb, D_MODEL), row),
                   pl.BlockSpec((1, n_keep, KV_WIDTH), lambda b, j: (b, 0, 0)),
                   pl.BlockSpec((1, n_keep, KV_WIDTH), lambda b, j: (b, 0, 0))],
        out_shape=[jax.ShapeDtypeStruct((bsz * seq, D_MODEL), F32),
                   jax.ShapeDtypeStruct((bsz, n_keep, KV_WIDTH), F32),
                   jax.ShapeDtypeStruct((bsz, n_keep, KV_WIDTH), F32)],
        scratch_shapes=[pltpu.VMEM((tb, ATT_WIDTH), BF16),
                        pltpu.VMEM((WINDOW + tb, KV_WIDTH), F32),
                        pltpu.VMEM((WINDOW + tb, KV_WIDTH), F32),
                        pltpu.VMEM((tb, ATT_WIDTH), F32)],
        compiler_params=_compiler_params(("arbitrary", "arbitrary")),
        name="attention",
    )(q, kv, g, cache_k, cache_v, cos, s1, s2, sink_cols, wo)


PN_K = 0
PN_V = PN_K + KV_WIDTH
PN_Z = PN_V + KV_WIDTH
PN_XBC = PN_Z + SSM_WIDTH
PN_GATES = PN_XBC + CONV_CH
PN_DT = PN_GATES + 2 * D_MODEL
PN_WIDTH = PN_DT + LANES
PT_Q = 0
PT_G = PT_Q + ATT_WIDTH
PT_V = PT_G + ATT_WIDTH
PT_WIDTH = PT_V + KV_WIDTH
HALF_ROT = ROT_DIM // 2
PAIR = 2 * CHUNK
KEY_WIN = WINDOW + PAIR


def _inproj_t_kernel(x_ref, nw_ref, w_ref, wt_ref, dtb_ref, cosq_ref, sinq_ref, cosk_ref, s1k_ref, s2k_ref,
                     qt_ref, gt_ref, vt_ref, k_ref, v_ref, z_ref, xbc_ref, gates_ref, dt_ref):
    x = x_ref[...]
    ms = jnp.mean(x * x, axis=-1, keepdims=True)
    h = (x * lax.rsqrt(ms + EPS) * nw_ref[...]).astype(BF16)

    def mm_t(off, rows):
        return lax.dot_general(wt_ref[off:off + rows, :], h, (((1,), (1,)), ((), ())),
                               preferred_element_type=F32)

    cosq = cosq_ref[...]
    sinq = sinq_ref[...]
    scale = HEAD_DIM ** -0.5
    for n0 in range(0, ATT_WIDTH, MM_COLS):
        r = mm_t(PT_Q + n0, MM_COLS) * scale
        for a in range(MM_COLS // HEAD_DIM):
            base = a * HEAD_DIM
            x1 = r[base:base + HALF_ROT]
            x2 = r[base + HALF_ROT:base + ROT_DIM]
            rot = jnp.concatenate([x1 * cosq - x2 * sinq, x2 * cosq + x1 * sinq], axis=0)
            qt_ref[n0 + base:n0 + base + ROT_DIM, :] = rot.astype(BF16)
            qt_ref[n0 + base + ROT_DIM:n0 + base + HEAD_DIM, :] = r[base + ROT_DIM:base + HEAD_DIM].astype(BF16)
    for n0 in range(0, ATT_WIDTH, MM_COLS):
        gt_ref[n0:n0 + MM_COLS, :] = _silu(mm_t(PT_G + n0, MM_COLS)).astype(BF16)
    vt_ref[...] = mm_t(PT_V, KV_WIDTH).astype(BF16)

    def section(out_ref, off, width, act):
        step = min(MM_COLS, width)
        for n0 in range(0, width, step):
            r = jnp.dot(h, w_ref[:, off + n0:off + n0 + step], preferred_element_type=F32)
            out_ref[:, n0:n0 + step] = act(r)

    ident = lambda r: r
    for s in range(KV_WIDTH // LANES):
        kk = jnp.dot(h, w_ref[:, PN_K + s * LANES:PN_K + (s + 1) * LANES], preferred_element_type=F32)
        k_ref[:, s * LANES:(s + 1) * LANES] = (
            kk * cosk_ref[...] + pltpu.roll(kk, LANES - HALF_ROT, 1) * s1k_ref[...]
            + pltpu.roll(kk, HALF_ROT, 1) * s2k_ref[...])
    section(v_ref, PN_V, KV_WIDTH, ident)
    section(z_ref, PN_Z, SSM_WIDTH, _silu)
    section(xbc_ref, PN_XBC, CONV_CH, ident)
    section(gates_ref, PN_GATES, 2 * D_MODEL, _sigmoid)
    lane = lax.broadcasted_iota(jnp.int32, (1, LANES), 1)
    section(dt_ref, PN_DT, LANES,
            lambda r: jnp.where(lane < SSM_HEADS, _softplus(r + dtb_ref[...]), 0.0))


def _inproj_t(x2d, nw, w_nat, w_t, dtb, qtabs, ktabs, seq, tm):
    m = x2d.shape[0]
    nper = seq // tm
    row = lambda i: (i, 0)
    col = lambda i: (0, i)
    nat_widths = (KV_WIDTH, KV_WIDTH, SSM_WIDTH, CONV_CH, 2 * D_MODEL, LANES)
    return pl.pallas_call(
        _inproj_t_kernel,
        grid=(m // tm,),
        in_specs=[pl.BlockSpec((tm, D_MODEL), row),
                  _resident((1, D_MODEL)),
                  _resident((D_MODEL, PN_WIDTH)),
                  _resident((PT_WIDTH, D_MODEL)),
                  _resident((1, LANES)),
                  pl.BlockSpec((HALF_ROT, tm), lambda i: (0, i % nper)),
                  pl.BlockSpec((HALF_ROT, tm), lambda i: (0, i % nper)),
                  pl.BlockSpec((tm, LANES), lambda i: (i % nper, 0)),
                  pl.BlockSpec((tm, LANES), lambda i: (i % nper, 0)),
                  pl.BlockSpec((tm, LANES), lambda i: (i % nper, 0))],
        out_specs=[pl.BlockSpec((ATT_WIDTH, tm), col),
                   pl.BlockSpec((ATT_WIDTH, tm), col),
                   pl.BlockSpec((KV_WIDTH, tm), col)]
                  + [pl.BlockSpec((tm, w), row) for w in nat_widths],
        out_shape=[jax.ShapeDtypeStruct((ATT_WIDTH, m), BF16),
                   jax.ShapeDtypeStruct((ATT_WIDTH, m), BF16),
                   jax.ShapeDtypeStruct((KV_WIDTH, m), BF16)]
                  + [jax.ShapeDtypeStruct((m, w), F32) for w in nat_widths],
        compiler_params=_compiler_params(("arbitrary",)),
        name="inproj_t",
    )(x2d, nw, w_nat, w_t, dtb, *qtabs, *ktabs)


def _attn_t_kernel(qt_ref, k_ref, vt_ref, gt_ref, bias_ref, sink_ref, wo_ref, o_ref, kbuf, vtbuf, att_t, *, tb):
    blk = pl.program_id(1)
    cur = blk % 2
    prv = 1 - cur

    @pl.when(blk == 0)
    def _():
        kbuf[1, tb - WINDOW:tb, :] = jnp.zeros((WINDOW, KV_WIDTH), BF16)
        vtbuf[1, :, tb - WINDOW:tb] = jnp.zeros((KV_WIDTH, WINDOW), BF16)

    kbuf[cur] = k_ref[...].astype(BF16)
    vtbuf[cur] = vt_ref[...]
    seq_start = jnp.where(blk == 0, -jnp.inf, 0.0)

    for pr in range(tb // PAIR):
        c0 = pr * PAIR
        bias = bias_ref[...]
        if pr == 0:
            bias = jnp.concatenate([bias[0:WINDOW] + seq_start, bias[WINDOW:KEY_WIN]], axis=0)
        for h in range(N_KV_HEADS):
            hs = slice(h * HEAD_DIM, (h + 1) * HEAD_DIM)
            heads = [h * Q_PER_KV + g for g in range(Q_PER_KV)]
            if pr == 0:
                k_win = jnp.concatenate([kbuf[prv, tb - WINDOW:tb, hs], kbuf[cur, 0:PAIR, hs]], axis=0)
                vt_win = jnp.concatenate([vtbuf[prv, hs, tb - WINDOW:tb], vtbuf[cur, hs, 0:PAIR]], axis=1)
            else:
                k_win = kbuf[cur, c0 - WINDOW:c0 + PAIR, hs]
                vt_win = vtbuf[cur, hs, c0 - WINDOW:c0 + PAIR]
            q4 = jnp.concatenate([qt_ref[hd * HEAD_DIM:(hd + 1) * HEAD_DIM, c0:c0 + PAIR] for hd in heads], axis=1)
            s = jnp.dot(k_win, q4, preferred_element_type=F32) + bias
            sk = sink_ref[h:h + 1, :]
            m = jnp.maximum(jnp.max(s, axis=0, keepdims=True), sk)
            p = jnp.exp(s - m)
            denom = jnp.sum(p, axis=0, keepdims=True) + jnp.exp(sk - m)
            o = jnp.dot(vt_win, p.astype(BF16), preferred_element_type=F32) / denom
            for g, hd in enumerate(heads):
                rows = slice(hd * HEAD_DIM, (hd + 1) * HEAD_DIM)
                att_t[rows, c0:c0 + PAIR] = (
                    o[:, g * PAIR:(g + 1) * PAIR] * gt_ref[rows, c0:c0 + PAIR].astype(F32)).astype(BF16)

    o_ref[...] = lax.dot_general(att_t[...], wo_ref[...], (((0,), (0,)), ((), ())), preferred_element_type=F32)


def _attention_t(qt, k, vt, gt, bias, sink_rows, wo, *, bsz, seq, tb):
    nblk = seq // tb
    row = lambda b, j: (b * nblk + j, 0)
    col = lambda b, j: (0, b * nblk + j)
    return pl.pallas_call(
        functools.partial(_attn_t_kernel, tb=tb),
        grid=(bsz, nblk),
        in_specs=[pl.BlockSpec((ATT_WIDTH, tb), col),
                  pl.BlockSpec((tb, KV_WIDTH), row),
                  pl.BlockSpec((KV_WIDTH, tb), col),
                  pl.BlockSpec((ATT_WIDTH, tb), col),
                  _resident((KEY_WIN, Q_PER_KV * PAIR)),
                  _resident((N_KV_HEADS, Q_PER_KV * PAIR)),
                  _resident((ATT_WIDTH, D_MODEL))],
        out_specs=pl.BlockSpec((tb, D_MODEL), row),
        out_shape=jax.ShapeDtypeStruct((bsz * seq, D_MODEL), F32),
        scratch_shapes=[pltpu.VMEM((2, tb, KV_WIDTH), BF16),
                        pltpu.VMEM((2, KV_WIDTH, tb), BF16),
                        pltpu.VMEM((ATT_WIDTH, tb), BF16)],
        compiler_params=_compiler_params(("arbitrary", "arbitrary")),
        name="attention_t",
    )(qt, k, vt, gt, bias, sink_rows, wo)


def _window_bias():
    key_chunk = jnp.arange(KEY_WIN)[:, None] // CHUNK
    q_chunk = (jnp.arange(Q_PER_KV * PAIR)[None, :] % PAIR) // CHUNK
    ok = (key_chunk >= q_chunk) & (key_chunk <= q_chunk + WINDOW // CHUNK)
    return jnp.where(ok, 0.0, -jnp.inf).astype(F32)


SSD_Q = 64
PIECE = 32
QUAD = 4 * SSM_HEAD_DIM


def _split3(x):
    hi = x.astype(BF16).astype(F32)
    r1 = x - hi
    mid = r1.astype(BF16).astype(F32)
    lo = (r1 - mid).astype(BF16).astype(F32)
    return (hi + pltpu.roll(mid, PIECE, 1) + pltpu.roll(lo, 2 * PIECE, 1)).astype(BF16)


def _ssd_kernel(xbc_ref, z_ref, dt_ref, cs0_ref, h0_ref, cw_ref, cb_ref, a_ref, e3_ref, dx_ref, nw_ref, wso_ref,
                o_ref, nconv_ref, nssm_ref, xpad, xc, ht, ybuf, *, tc, nblk, out_blk, out_row):
    blk = pl.program_id(1)
    pad0 = SUBLANES - (CONV_W - 1)

    @pl.when(blk == 0)
    def _():
        xpad[pad0:SUBLANES, :] = cs0_ref[0]
        ht[...] = h0_ref[0].T

    xpad[SUBLANES:SUBLANES + tc, :] = xbc_ref[...]
    acc = cb_ref[...] + xpad[pad0:pad0 + tc, :] * cw_ref[0:1, :]
    for tap in range(1, CONV_W):
        acc = acc + xpad[pad0 + tap:pad0 + tap + tc, :] * cw_ref[tap:tap + 1, :]
    xc[...] = _silu(acc)

    @pl.when(blk == out_blk)
    def _():
        nconv_ref[0] = xpad[SUBLANES + out_row - (CONV_W - 1):SUBLANES + out_row, :]

    xpad[pad0:SUBLANES, :] = xpad[pad0 + tc:SUBLANES + tc, :]

    lane = lax.broadcasted_iota(jnp.int32, (SSD_Q, LANES), 1)
    ri = lax.broadcasted_iota(jnp.int32, (SSD_Q, SSD_Q), 0)
    rj = lax.broadcasted_iota(jnp.int32, (SSD_Q, SSD_Q), 1)
    tri = (ri >= rj).astype(BF16)
    ii = lax.broadcasted_iota(jnp.int32, (SSD_Q, SSM_WIDTH), 0)
    jj = lax.broadcasted_iota(jnp.int32, (SSD_Q, SSM_WIDTH), 1) % SSD_Q
    bi = lax.broadcasted_iota(jnp.int32, (QUAD, QUAD), 0) // SSM_HEAD_DIM
    bj = lax.broadcasted_iota(jnp.int32, (QUAD, QUAD), 1) // SSM_HEAD_DIM
    a_row = a_ref[...]
    dx = dx_ref[...]
    nw = nw_ref[...]

    def expand(v):
        return jnp.dot(_split3(v), e3_ref[...], preferred_element_type=F32)

    def chunk(c, carry):
        r0 = pl.multiple_of(c * SSD_Q, SSD_Q)
        xs = xc[pl.ds(r0, SSD_Q), 0:SSM_WIDTH]
        bm = xc[pl.ds(r0, SSD_Q), SSM_WIDTH:SSM_WIDTH + BC_WIDTH].astype(BF16)
        cm = xc[pl.ds(r0, SSD_Q), SSM_WIDTH + BC_WIDTH:CONV_CH].astype(BF16)
        dt = dt_ref[pl.ds(r0, SSD_Q), :]
        cum3 = jnp.dot(tri, _split3(dt * a_row), preferred_element_type=F32)
        cum = jnp.where(lane < PIECE,
                        cum3 + pltpu.roll(cum3, LANES - PIECE, 1) + pltpu.roll(cum3, LANES - 2 * PIECE, 1), 0.0)
        dt_x = expand(dt)
        cum_x = expand(cum)
        ecum_x = jnp.exp(cum_x)
        cum_last = cum_x[SSD_Q - 1:SSD_Q, :]
        xdt = xs * dt_x
        xw = (xdt * jnp.exp(cum_last - cum_x)).astype(BF16)
        cum_j = jnp.sum(jnp.where(ii == jj, cum_x, 0.0), axis=0, keepdims=True)
        decay = jnp.exp(jnp.where(ii >= jj, cum_x - cum_j, -jnp.inf))

        y_parts = []
        for g in range(SSM_GROUPS):
            cg = cm[:, g * SSM_STATE:(g + 1) * SSM_STATE]
            bg = bm[:, g * SSM_STATE:(g + 1) * SSM_STATE]
            cb4 = lax.dot_general(cg, jnp.concatenate([bg] * 4, axis=0), (((1,), (1,)), ((), ())),
                                  preferred_element_type=F32)
            gs = slice(g * GROUP_WIDTH, (g + 1) * GROUP_WIDTH)
            y_state = jnp.dot(cg, ht[:, gs].astype(BF16), preferred_element_type=F32) * ecum_x[:, gs]
            for half in range(GROUP_WIDTH // QUAD):
                qs = slice(g * GROUP_WIDTH + half * QUAD, g * GROUP_WIDTH + (half + 1) * QUAD)
                m4 = (cb4 * decay[:, qs]).astype(BF16)
                x4 = xdt[:, qs]
                bd = jnp.where(bi == bj, jnp.concatenate([x4] * 4, axis=0), 0.0).astype(BF16)
                y_parts.append(jnp.dot(m4, bd, preferred_element_type=F32)
                               + y_state[:, half * QUAD:(half + 1) * QUAD])
            ht[:, gs] = (ht[:, gs] * ecum_x[SSD_Q - 1:SSD_Q, gs]
                         + jnp.dot(bg.T, xw[:, gs], preferred_element_type=F32))
        y = jnp.concatenate(y_parts, axis=1) + xs * dx
        yz = y * z_ref[pl.ds(r0, SSD_Q), :]
        for g in range(SSM_GROUPS):
            gs = slice(g * GROUP_WIDTH, (g + 1) * GROUP_WIDTH)
            yg = yz[:, gs]
            ms = jnp.mean(yg * yg, axis=-1, keepdims=True)
            ybuf[pl.ds(r0, SSD_Q), gs] = (yg * lax.rsqrt(ms + EPS) * nw[:, gs]).astype(BF16)
        return carry

    lax.fori_loop(0, tc // SSD_Q, chunk, 0)
    o_ref[...] = jnp.dot(ybuf[...], wso_ref[...], preferred_element_type=F32)

    @pl.when(blk == nblk - 1)
    def _():
        nssm_ref[0] = ht[...].T


def _ssd(xbc, zs, dt, conv0, ssm0, cw, cb, a_row, e3, dx, nw, wso, *, bsz, seq, tc, l_real):
    nblk = seq // tc
    out_blk = (l_real - 1) // tc
    out_row = l_real - out_blk * tc
    row = lambda b, j: (b * nblk + j, 0)
    hp = SSM_HEADS * SSM_HEAD_DIM
    kern = functools.partial(_ssd_kernel, tc=tc, nblk=nblk, out_blk=out_blk, out_row=out_row)
    return pl.pallas_call(
        kern,
        grid=(bsz, nblk),
        in_specs=[pl.BlockSpec((tc, CONV_CH), row),
                  pl.BlockSpec((tc, SSM_WIDTH), row),
                  pl.BlockSpec((tc, LANES), row),
                  pl.BlockSpec((1, CONV_W - 1, CONV_CH), lambda b, j: (b, 0, 0)),
                  pl.BlockSpec((1, hp, SSM_STATE), lambda b, j: (b, 0, 0)),
                  _resident((CONV_W, CONV_CH)),
                  _resident((1, CONV_CH)),
                  _resident((1, LANES)),
                  _resident((LANES, SSM_WIDTH)),
                  _resident((1, SSM_WIDTH)),
                  _resident((1, SSM_WIDTH)),
                  _resident((SSM_WIDTH, D_MODEL))],
        out_specs=[pl.BlockSpec((tc, D_MODEL), row),
                   pl.BlockSpec((1, CONV_W - 1, CONV_CH), lambda b, j: (b, 0, 0)),
                   pl.BlockSpec((1, hp, SSM_STATE), lambda b, j: (b, 0, 0))],
        out_shape=[jax.ShapeDtypeStruct((bsz * seq, D_MODEL), F32),
                   jax.ShapeDtypeStruct((bsz, CONV_W - 1, CONV_CH), F32),
                   jax.ShapeDtypeStruct((bsz, hp, SSM_STATE), F32)],
        scratch_shapes=[pltpu.VMEM((SUBLANES + tc, CONV_CH), F32),
                        pltpu.VMEM((tc, CONV_CH), F32),
                        pltpu.VMEM((SSM_STATE, SSM_WIDTH), F32),
                        pltpu.VMEM((tc, SSM_WIDTH), BF16)],
        compiler_params=_compiler_params(("arbitrary", "arbitrary")),
        name="ssd",
    )(xbc, zs, dt, conv0, ssm0, cw, cb, a_row, e3, dx, nw, wso)


def _merge_kernel(x_ref, ba_ref, bs_ref, gates_ref, wout_ref, nw_ref, y_ref):
    merged = gates_ref[:, 0:D_MODEL] * ba_ref[...] + gates_ref[:, D_MODEL:2 * D_MODEL] * bs_ref[...]
    out = jnp.dot(merged.astype(BF16), wout_ref[...], preferred_element_type=F32)
    ms = jnp.mean(out * out, axis=-1, keepdims=True)
    y_ref[...] = x_ref[...] + out * lax.rsqrt(ms + EPS) * nw_ref[...]


def _merge(x2d, ba, bs, gates, wout, nw, tm):
    m = x2d.shape[0]
    row = lambda i: (i, 0)
    return pl.pallas_call(
        _merge_kernel,
        grid=(m // tm,),
        in_specs=[pl.BlockSpec((tm, D_MODEL), row),
                  pl.BlockSpec((tm, D_MODEL), row),
                  pl.BlockSpec((tm, D_MODEL), row),
                  pl.BlockSpec((tm, 2 * D_MODEL), row),
                  _resident((D_MODEL, D_MODEL)),
                  _resident((1, D_MODEL))],
        out_specs=pl.BlockSpec((tm, D_MODEL), row),
        out_shape=jax.ShapeDtypeStruct((m, D_MODEL), F32),
        compiler_params=_compiler_params(("arbitrary",)),
        name="merge",
    )(x2d, ba, bs, gates, wout, nw)


def _rope_angles(pos):
    half = ROT_DIM // 2
    inv = ROPE_THETA ** (-(jnp.arange(half, dtype=F32) * 2.0 / ROT_DIM))
    ang = pos.astype(F32)[:, None] * inv[None, :]
    return jnp.cos(ang), jnp.sin(ang)


def _rope_tables(pos):
    cos, sin = _rope_angles(pos)
    half = ROT_DIM // 2
    n = pos.shape[0]
    rest = HEAD_DIM - ROT_DIM
    c = jnp.concatenate([cos, cos, jnp.ones((n, rest), F32)], axis=1)
    s1 = jnp.concatenate([-sin, jnp.zeros((n, half + rest), F32)], axis=1)
    s2 = jnp.concatenate([jnp.zeros((n, half), F32), sin, jnp.zeros((n, rest), F32)], axis=1)
    rep = LANES // HEAD_DIM
    return tuple(jnp.tile(t, (1, rep)) for t in (c, s1, s2))


def _ssd_and_merge(x2d, branch_a, xbc, zs, dt, gates, conv0, ssm0, wts, *, bsz, seq, tm, tc):
    seq_p = -(-seq // SSD_Q) * SSD_Q
    if seq_p != seq:
        padrows = lambda t: jnp.pad(t.reshape(bsz, seq, -1), ((0, 0), (0, seq_p - seq), (0, 0))).reshape(
            bsz * seq_p, -1)
        xbc, zs, dt = padrows(xbc), padrows(zs), padrows(dt)
    branch_s, new_conv, new_ssm = _ssd(
        xbc, zs, dt, conv0, ssm0, wts["conv_w"], wts["conv_b"], wts["a_row"], wts["e3"], wts["d_x"],
        wts["ssm_norm_w"], wts["w_ssm_o"], bsz=bsz, seq=seq_p, tc=min(tc, seq_p), l_real=seq)
    if seq_p != seq:
        branch_s = branch_s.reshape(bsz, seq_p, D_MODEL)[:, :seq].reshape(bsz * seq, D_MODEL)
    y = _merge(x2d, branch_a, branch_s, gates, wts["w_out"], wts["post_norm_w"], tm)
    return (y.reshape(bsz, seq, D_MODEL), new_conv,
            new_ssm.reshape(bsz, SSM_HEADS, SSM_HEAD_DIM, SSM_STATE))


def _layer_prompt(x, pos, wts, *, tm, tb, tc):
    bsz, seq, _ = x.shape
    x2d = x.reshape(bsz * seq, D_MODEL)
    cos, sin = _rope_angles(pos)
    qt, gt, vt, k, v, zs, xbc, gates, dt = _inproj_t(
        x2d, wts["pre_norm_w"], wts["w_nat"], wts["w_t"], wts["dt_bias"], (cos.T, sin.T), _rope_tables(pos),
        seq, tm)
    sink_rows = jnp.repeat(wts["sinks"].reshape(N_KV_HEADS, Q_PER_KV), PAIR, axis=1)
    branch_a = _attention_t(qt, k, vt, gt, _window_bias(), sink_rows, wts["w_attn_o"], bsz=bsz, seq=seq, tb=tb)
    n_keep = min(WINDOW, seq)
    keep = lambda t: t.reshape(bsz, seq, N_KV_HEADS, HEAD_DIM)[:, seq - n_keep:]
    zero_conv = jnp.zeros((bsz, CONV_W - 1, CONV_CH), F32)
    zero_ssm = jnp.zeros((bsz, SSM_HEADS * SSM_HEAD_DIM, SSM_STATE), F32)
    y, new_conv, new_ssm = _ssd_and_merge(x2d, branch_a, xbc, zs, dt, gates, zero_conv, zero_ssm, wts,
                                          bsz=bsz, seq=seq, tm=tm, tc=tc)
    return y, keep(k), keep(v), new_conv, new_ssm


def _layer_sample(x, pos, cache_k, cache_v, conv0, ssm0, wts, *, tm):
    bsz, seq, _ = x.shape
    x2d = x.reshape(bsz * seq, D_MODEL)
    q, kv, g, zs, xbc, gates, dt = _inproj(x2d, wts["pre_norm_w"], wts["w_cat"], wts["dt_bias"], tm)
    sink_cols = jnp.broadcast_to(
        jnp.repeat(wts["sinks"].reshape(N_KV_HEADS, Q_PER_KV), seq, axis=1)[:, :, None],
        (N_KV_HEADS, Q_PER_KV * seq, LANES))
    branch_a, new_k, new_v = _attention(
        q, kv, g, cache_k, cache_v, _rope_tables(pos), sink_cols, wts["w_attn_o"],
        bsz=bsz, seq=seq, tb=seq, ch=seq, mask_first=False)
    y, new_conv, new_ssm = _ssd_and_merge(x2d, branch_a, xbc, zs, dt, gates, conv0, ssm0, wts,
                                          bsz=bsz, seq=seq, tm=tm, tc=SSD_Q)
    n_keep = new_k.shape[1]
    return (y, new_k.reshape(bsz, n_keep, N_KV_HEADS, HEAD_DIM), new_v.reshape(bsz, n_keep, N_KV_HEADS, HEAD_DIM),
            new_conv, new_ssm)


def _prep_weights(pre_norm_w, w_in, conv_w, conv_b, dt_bias, a_log, d_skip, sinks, ssm_norm_w, w_attn_o,
                  w_ssm_o, w_out, post_norm_w):
    g_off = ATT_WIDTH + 2 * KV_WIDTH
    z_off = g_off + ATT_WIDTH
    dt_off = z_off + SSM_WIDTH + CONV_CH
    wb = w_in.astype(BF16)
    dt_cols = jnp.concatenate([wb[:, dt_off:dt_off + SSM_HEADS],
                               jnp.zeros((D_MODEL, LANES - SSM_HEADS), BF16)], axis=1)
    w_cat = jnp.concatenate([wb[:, :dt_off], wb[:, dt_off + SSM_HEADS:], dt_cols], axis=1)
    w_nat = jnp.concatenate([wb[:, ATT_WIDTH:g_off], wb[:, z_off:dt_off], wb[:, dt_off + SSM_HEADS:], dt_cols],
                            axis=1)
    w_t = jnp.concatenate([wb[:, :ATT_WIDTH], wb[:, g_off:z_off], wb[:, ATT_WIDTH + KV_WIDTH:g_off]], axis=1).T
    pad_heads = lambda v: jnp.pad(v.astype(F32), (0, LANES - SSM_HEADS)).reshape(1, LANES)
    k_idx = jnp.arange(LANES)[:, None]
    c_idx = jnp.arange(SSM_WIDTH)[None, :]
    e3 = ((k_idx < 3 * PIECE) & (k_idx % PIECE == c_idx // SSM_HEAD_DIM)).astype(BF16)
    return dict(
        pre_norm_w=pre_norm_w.reshape(1, D_MODEL), w_cat=w_cat, w_nat=w_nat, w_t=w_t, dt_bias=pad_heads(dt_bias),
        conv_w=conv_w, conv_b=conv_b.reshape(1, CONV_CH), a_row=pad_heads(-jnp.exp(a_log.astype(F32))),
        e3=e3, d_x=jnp.repeat(d_skip.astype(F32), SSM_HEAD_DIM).reshape(1, SSM_WIDTH), sinks=sinks.astype(F32),
        ssm_norm_w=ssm_norm_w.reshape(1, SSM_WIDTH), w_attn_o=w_attn_o.astype(BF16),
        w_ssm_o=w_ssm_o.astype(BF16), w_out=w_out.astype(BF16), post_norm_w=post_norm_w.reshape(1, D_MODEL))


def kernel(x_prompt, x_sample, cache_k, cache_v, state_conv, state_ssm, pre_norm_w, w_in, conv_w, conv_b, dt_bias,
           a_log, d_skip, sinks, ssm_norm_w, w_attn_o, w_ssm_o, w_out, post_norm_w):
    depth = w_in.shape[0]
    lp = x_prompt.shape[1]
    bs, ls, _ = x_sample.shape
    pos_p = jnp.arange(lp, dtype=F32)
    pos_s = PAST_LEN + jnp.arange(ls, dtype=F32)
    hp = SSM_HEADS * SSM_HEAD_DIM
    yp, ys = x_prompt, x_sample
    outs = [[] for _ in range(8)]
    for layer in range(depth):
        wts = _prep_weights(pre_norm_w[layer], w_in[layer], conv_w[layer], conv_b[layer], dt_bias[layer],
                            a_log[layer], d_skip[layer], sinks[layer], ssm_norm_w[layer], w_attn_o[layer],
                            w_ssm_o[layer], w_out[layer], post_norm_w[layer])
        yp, kp, vp, cp, sp = _layer_prompt(yp, pos_p, wts, tm=256, tb=256, tc=256)
        ys, ks, vs, cs, ss = _layer_sample(ys, pos_s,
                                           cache_k[layer].reshape(bs, WINDOW, KV_WIDTH),
                                           cache_v[layer].reshape(bs, WINDOW, KV_WIDTH),
                                           state_conv[layer], state_ssm[layer].reshape(bs, hp, SSM_STATE), wts,
                                           tm=256)
        for lst, val in zip(outs, (kp, vp, cp, sp, ks, vs, cs, ss)):
            lst.append(val)
    return (yp, ys) + tuple(jnp.stack(lst) for lst in outs)
```

```python
import functools

import jax
import jax.numpy as jnp
from jax import lax
from jax.experimental import pallas as pl
from jax.experimental.pallas import tpu as pltpu

F32 = jnp.float32
BF16 = jnp.bfloat16

D_MODEL = 1024
PAST_LEN = 2048
CHUNK = 64
N_Q_HEADS = 16
N_KV_HEADS = 4
HEAD_DIM = 64
Q_PER_KV = N_Q_HEADS // N_KV_HEADS
ATT_WIDTH = N_Q_HEADS * HEAD_DIM
KV_WIDTH = N_KV_HEADS * HEAD_DIM
WINDOW = 128
ROT_DIM = HEAD_DIM // 4
ROPE_THETA = 500000.0
SSM_WIDTH = 2 * D_MODEL
SSM_HEAD_DIM = 64
SSM_HEADS = SSM_WIDTH // SSM_HEAD_DIM
SSM_GROUPS = 4
SSM_STATE = 128
GROUP_WIDTH = SSM_WIDTH // SSM_GROUPS
BC_WIDTH = SSM_GROUPS * SSM_STATE
CONV_W = 4
CONV_CH = SSM_WIDTH + 2 * BC_WIDTH
EPS = 1e-6

LANES = 128
SUBLANES = 8
VMEM_LIMIT = 56 * 1024 * 1024

OFF_Q = 0
OFF_KV = OFF_Q + ATT_WIDTH
OFF_G = OFF_KV + 2 * KV_WIDTH
OFF_Z = OFF_G + ATT_WIDTH
OFF_XBC = OFF_Z + SSM_WIDTH
OFF_GATES = OFF_XBC + CONV_CH
OFF_DT = OFF_GATES + 2 * D_MODEL
IN_WIDTH_PADDED = OFF_DT + LANES
MM_COLS = 512


def _sigmoid(x):
    return 1.0 / (1.0 + jnp.exp(-x))


def _silu(x):
    return x * _sigmoid(x)


def _softplus(x):
    return jnp.maximum(x, 0.0) + jnp.log1p(jnp.exp(-jnp.abs(x)))


def _compiler_params(semantics):
    return pltpu.CompilerParams(dimension_semantics=semantics, vmem_limit_bytes=VMEM_LIMIT)


def _resident(shape):
    zeros = (0,) * len(shape)
    return pl.BlockSpec(shape, lambda *_: zeros, pipeline_mode=pl.Buffered(1))


def _inproj_kernel(x_ref, nw_ref, w_ref, dtb_ref, q_ref, kv_ref, g_ref, z_ref, xbc_ref, gates_ref, dt_ref):
    x = x_ref[...]
    ms = jnp.mean(x * x, axis=-1, keepdims=True)
    h = (x * lax.rsqrt(ms + EPS) * nw_ref[...]).astype(BF16)

    def section(out_ref, off, width, act):
        step = min(MM_COLS, width)
        for n0 in range(0, width, step):
            r = jnp.dot(h, w_ref[:, off + n0:off + n0 + step], preferred_element_type=F32)
            out_ref[:, n0:n0 + step] = act(r)

    ident = lambda r: r
    section(q_ref, OFF_Q, ATT_WIDTH, ident)
    section(kv_ref, OFF_KV, 2 * KV_WIDTH, ident)
    section(g_ref, OFF_G, ATT_WIDTH, _silu)
    section(z_ref, OFF_Z, SSM_WIDTH, _silu)
    section(xbc_ref, OFF_XBC, CONV_CH, ident)
    section(gates_ref, OFF_GATES, 2 * D_MODEL, _sigmoid)
    lane = lax.broadcasted_iota(jnp.int32, (1, LANES), 1)
    section(dt_ref, OFF_DT, LANES,
            lambda r: jnp.where(lane < SSM_HEADS, _softplus(r + dtb_ref[...]), 0.0))


def _inproj(x2d, nw, w_cat, dtb, tm):
    m = x2d.shape[0]
    widths = (ATT_WIDTH, 2 * KV_WIDTH, ATT_WIDTH, SSM_WIDTH, CONV_CH, 2 * D_MODEL, LANES)
    return pl.pallas_call(
        _inproj_kernel,
        grid=(m // tm,),
        in_specs=[pl.BlockSpec((tm, D_MODEL), lambda i: (i, 0)),
                  _resident((1, D_MODEL)),
                  _resident((D_MODEL, IN_WIDTH_PADDED)),
                  _resident((1, LANES))],
        out_specs=[pl.BlockSpec((tm, w), lambda i: (i, 0)) for w in widths],
        out_shape=[jax.ShapeDtypeStruct((m, w), F32) for w in widths],
        compiler_params=_compiler_params(("arbitrary",)),
        name="inproj",
    )(x2d, nw, w_cat, dtb)


def _attn_kernel(q_ref, kv_ref, g_ref, ck_ref, cv_ref, cos_ref, s1_ref, s2_ref, sink_ref, wo_ref,
                 o_ref, nk_ref, nv_ref, qbuf, kbuf, vbuf, att, *, tb, ch, nblk, mask_first, n_keep):
    blk = pl.program_id(1)

    @pl.when(blk == 0)
    def _():
        kbuf[0:WINDOW, :] = ck_ref[0]
        vbuf[0:WINDOW, :] = cv_ref[0]

    cos = cos_ref[...]
    s1 = s1_ref[...]
    s2 = s2_ref[...]

    def rope(x):
        return x * cos + pltpu.roll(x, LANES - ROT_DIM // 2, 1) * s1 + pltpu.roll(x, ROT_DIM // 2, 1) * s2

    scale = HEAD_DIM ** -0.5
    for s in range(ATT_WIDTH // LANES):
        sl = slice(s * LANES, (s + 1) * LANES)
        qbuf[:, sl] = (rope(q_ref[:, sl]) * scale).astype(BF16)
    for s in range(KV_WIDTH // LANES):
        sl = slice(s * LANES, (s + 1) * LANES)
        kbuf[WINDOW:WINDOW + tb, sl] = rope(kv_ref[:, sl])
    vbuf[WINDOW:WINDOW + tb, :] = kv_ref[:, KV_WIDTH:2 * KV_WIDTH]

    nkeys = WINDOW + ch
    for c in range(tb // ch):
        r0 = c * ch
        if mask_first:
            col_chunk = lax.broadcasted_iota(jnp.int32, (1, nkeys), 1) // CHUNK
            valid = (blk * (tb // ch) + c - WINDOW // CHUNK + col_chunk) >= 0
        for h in range(N_KV_HEADS):
            hs = slice(h * HEAD_DIM, (h + 1) * HEAD_DIM)
            qs = jnp.concatenate(
                [qbuf[r0:r0 + ch, (h * Q_PER_KV + g) * HEAD_DIM:(h * Q_PER_KV + g + 1) * HEAD_DIM]
                 for g in range(Q_PER_KV)], axis=0)
            kb = kbuf[r0:r0 + nkeys, hs].astype(BF16)
            vb = vbuf[r0:r0 + nkeys, hs].astype(BF16)
            s = lax.dot_general(qs, kb, (((1,), (1,)), ((), ())), preferred_element_type=F32)
            if mask_first:
                s = jnp.where(valid, s, -jnp.inf)
            sk = sink_ref[h][:, 0:1]
            m = jnp.maximum(jnp.max(s, axis=-1, keepdims=True), sk)
            p = jnp.exp(s - m)
            denom = jnp.sum(p, axis=-1, keepdims=True) + jnp.exp(sk - m)
            o = jnp.dot(p.astype(BF16), vb, preferred_element_type=F32) / denom
            for g in range(Q_PER_KV):
                head = h * Q_PER_KV + g
                att[r0:r0 + ch, head * HEAD_DIM:(head + 1) * HEAD_DIM] = o[g * ch:(g + 1) * ch, :]

    a = (att[...] * g_ref[...]).astype(BF16)
    o_ref[...] = jnp.dot(a, wo_ref[...], preferred_element_type=F32)

    @pl.when(blk == nblk - 1)
    def _():
        nk_ref[0] = kbuf[WINDOW + tb - n_keep:WINDOW + tb, :]
        nv_ref[0] = vbuf[WINDOW + tb - n_keep:WINDOW + tb, :]

    if nblk > 1:
        kbuf[0:WINDOW, :] = kbuf[tb:tb + WINDOW, :]
        vbuf[0:WINDOW, :] = vbuf[tb:tb + WINDOW, :]


def _attention(q, kv, g, cache_k, cache_v, tables, sink_cols, wo, *, bsz, seq, tb, ch, mask_first):
    nblk = seq // tb
    n_keep = min(WINDOW, seq)
    cos, s1, s2 = tables
    row = lambda b, j: (b * nblk + j, 0)
    kern = functools.partial(_attn_kernel, tb=tb, ch=ch, nblk=nblk, mask_first=mask_first, n_keep=n_keep)
    return pl.pallas_call(
        kern,
        grid=(bsz, nblk),
        in_specs=[pl.BlockSpec((tb, ATT_WIDTH), row),
                  pl.BlockSpec((tb, 2 * KV_WIDTH), row),
                  pl.BlockSpec((tb, ATT_WIDTH), row),
                  pl.BlockSpec((1, WINDOW, KV_WIDTH), lambda b, j: (b, 0, 0)),
                  pl.BlockSpec((1, WINDOW, KV_WIDTH), lambda b, j: (b, 0, 0)),
                  pl.BlockSpec((tb, LANES), lambda b, j: (j, 0)),
                  pl.BlockSpec((tb, LANES), lambda b, j: (j, 0)),
                  pl.BlockSpec((tb, LANES), lambda b, j: (j, 0)),
                  _resident((N_KV_HEADS, Q_PER_KV * ch, LANES)),
                  _resident((ATT_WIDTH, D_MODEL))],
        out_specs=[pl.BlockSpec((tb, D_MODEL), row),
                   pl.BlockSpec((1, n_keep, KV_WIDTH), lambda b, j: (b, 0, 0)),
                   pl.BlockSpec((1, n_keep, KV_WIDTH), lambda b, j: (b, 0, 0))],
        out_shape=[jax.ShapeDtypeStruct((bsz * seq, D_MODEL), F32),
                   jax.ShapeDtypeStruct((bsz, n_keep, KV_WIDTH), F32),
                   jax.ShapeDtypeStruct((bsz, n_keep, KV_WIDTH), F32)],
        scratch_shapes=[pltpu.VMEM((tb, ATT_WIDTH), BF16),
                        pltpu.VMEM((WINDOW + tb, KV_WIDTH), F32),
                        pltpu.VMEM((WINDOW + tb, KV_WIDTH), F32),
                        pltpu.VMEM((tb, ATT_WIDTH), F32)],
        compiler_params=_compiler_params(("arbitrary", "arbitrary")),
        name="attention",
    )(q, kv, g, cache_k, cache_v, cos, s1, s2, sink_cols, wo)


PN_K = 0
PN_V = PN_K + KV_WIDTH
PN_Z = PN_V + KV_WIDTH
PN_XBC = PN_Z + SSM_WIDTH
PN_GATES = PN_XBC + CONV_CH
PN_DT = PN_GATES + 2 * D_MODEL
PN_WIDTH = PN_DT + LANES
PT_Q = 0
PT_G = PT_Q + ATT_WIDTH
PT_V = PT_G + ATT_WIDTH
PT_WIDTH = PT_V + KV_WIDTH
HALF_ROT = ROT_DIM // 2
PAIR = 2 * CHUNK
KEY_WIN = WINDOW + PAIR


def _inproj_t_kernel(x_ref, nw_ref, w_ref, wt_ref, dtb_ref, cosq_ref, sinq_ref, cosk_ref, s1k_ref, s2k_ref,
                     cw_ref, cb_ref,
                     qt_ref, gt_ref, vt_ref, k_ref, v_ref, z_ref, xc_ref, gates_ref, dt_ref, ctail_ref,
                     cprev, *, tm, nper):
    step = pl.program_id(0)
    seq_start = step % nper == 0
    cur = step % 2
    x = x_ref[...]
    ms = jnp.mean(x * x, axis=-1, keepdims=True)
    h = (x * lax.rsqrt(ms + EPS) * nw_ref[...]).astype(BF16)

    def mm_t(off, rows):
        return lax.dot_general(wt_ref[off:off + rows, :], h, (((1,), (1,)), ((), ())),
                               preferred_element_type=F32)

    cosq = cosq_ref[...]
    sinq = sinq_ref[...]
    scale = HEAD_DIM ** -0.5
    for n0 in range(0, ATT_WIDTH, MM_COLS):
        r = mm_t(PT_Q + n0, MM_COLS) * scale
        for a in range(MM_COLS // HEAD_DIM):
            base = a * HEAD_DIM
            x1 = r[base:base + HALF_ROT]
            x2 = r[base + HALF_ROT:base + ROT_DIM]
            rot = jnp.concatenate([x1 * cosq - x2 * sinq, x2 * cosq + x1 * sinq], axis=0)
            qt_ref[n0 + base:n0 + base + ROT_DIM, :] = rot.astype(BF16)
            qt_ref[n0 + base + ROT_DIM:n0 + base + HEAD_DIM, :] = r[base + ROT_DIM:base + HEAD_DIM].astype(BF16)
    for n0 in range(0, ATT_WIDTH, MM_COLS):
        gt_ref[n0:n0 + MM_COLS, :] = _silu(mm_t(PT_G + n0, MM_COLS)).astype(BF16)
    vt_ref[...] = mm_t(PT_V, KV_WIDTH).astype(BF16)

    def section(out_ref, off, width, act):
        step = min(MM_COLS, width)
        for n0 in range(0, width, step):
            r = jnp.dot(h, w_ref[:, off + n0:off + n0 + step], preferred_element_type=F32)
            out_ref[:, n0:n0 + step] = act(r)

    ident = lambda r: r
    for s in range(KV_WIDTH // LANES):
        kk = jnp.dot(h, w_ref[:, PN_K + s * LANES:PN_K + (s + 1) * LANES], preferred_element_type=F32)
        k_ref[:, s * LANES:(s + 1) * LANES] = (
            kk * cosk_ref[...] + pltpu.roll(kk, LANES - HALF_ROT, 1) * s1k_ref[...]
            + pltpu.roll(kk, HALF_ROT, 1) * s2k_ref[...])
    section(v_ref, PN_V, KV_WIDTH, ident)
    section(z_ref, PN_Z, SSM_WIDTH, _silu)
    row8 = lax.broadcasted_iota(jnp.int32, (SUBLANES, MM_COLS), 0)
    for n0 in range(0, CONV_CH, MM_COLS):
        cols = slice(n0, n0 + MM_COLS)
        r = jnp.dot(h, w_ref[:, PN_XBC + n0:PN_XBC + n0 + MM_COLS], preferred_element_type=F32)
        prev = jnp.where(seq_start, 0.0, cprev[cur, :, cols])
        acc = cb_ref[:, cols] + r * cw_ref[CONV_W - 1:CONV_W, cols]
        for k in range(1, CONV_W):
            rolled = pltpu.roll(r, k, 0)
            top = jnp.where(row8 < k, pltpu.roll(prev, k, 0), rolled[0:SUBLANES])
            shifted = jnp.concatenate([top, rolled[SUBLANES:tm]], axis=0)
            acc = acc + shifted * cw_ref[CONV_W - 1 - k:CONV_W - k, cols]
        xc_ref[:, cols] = _silu(acc)
        cprev[1 - cur, :, cols] = r[tm - SUBLANES:tm]
        ctail_ref[0, :, cols] = r[tm - SUBLANES:tm]
    section(gates_ref, PN_GATES, 2 * D_MODEL, _sigmoid)
    lane = lax.broadcasted_iota(jnp.int32, (1, LANES), 1)
    section(dt_ref, PN_DT, LANES,
            lambda r: jnp.where(lane < SSM_HEADS, _softplus(r + dtb_ref[...]), 0.0))


def _inproj_t(x2d, nw, w_nat, w_t, dtb, qtabs, ktabs, cw, cb, seq, tm):
    m = x2d.shape[0]
    nper = seq // tm
    row = lambda i: (i, 0)
    col = lambda i: (0, i)
    nat_widths = (KV_WIDTH, KV_WIDTH, SSM_WIDTH, CONV_CH, 2 * D_MODEL, LANES)
    return pl.pallas_call(
        functools.partial(_inproj_t_kernel, tm=tm, nper=nper),
        grid=(m // tm,),
        in_specs=[pl.BlockSpec((tm, D_MODEL), row),
                  _resident((1, D_MODEL)),
                  _resident((D_MODEL, PN_WIDTH)),
                  _resident((PT_WIDTH, D_MODEL)),
                  _resident((1, LANES)),
                  pl.BlockSpec((HALF_ROT, tm), lambda i: (0, i % nper)),
                  pl.BlockSpec((HALF_ROT, tm), lambda i: (0, i % nper)),
                  pl.BlockSpec((tm, LANES), lambda i: (i % nper, 0)),
                  pl.BlockSpec((tm, LANES), lambda i: (i % nper, 0)),
                  pl.BlockSpec((tm, LANES), lambda i: (i % nper, 0)),
                  _resident((CONV_W, CONV_CH)),
                  _resident((1, CONV_CH))],
        out_specs=[pl.BlockSpec((ATT_WIDTH, tm), col),
                   pl.BlockSpec((ATT_WIDTH, tm), col),
                   pl.BlockSpec((KV_WIDTH, tm), col)]
                  + [pl.BlockSpec((tm, w), row) for w in nat_widths]
                  + [pl.BlockSpec((1, SUBLANES, CONV_CH), lambda i: (i // nper, 0, 0))],
        out_shape=[jax.ShapeDtypeStruct((ATT_WIDTH, m), BF16),
                   jax.ShapeDtypeStruct((ATT_WIDTH, m), BF16),
                   jax.ShapeDtypeStruct((KV_WIDTH, m), BF16)]
                  + [jax.ShapeDtypeStruct((m, w), F32) for w in nat_widths]
                  + [jax.ShapeDtypeStruct((m // seq, SUBLANES, CONV_CH), F32)],
        scratch_shapes=[pltpu.VMEM((2, SUBLANES, CONV_CH), F32)],
        compiler_params=_compiler_params(("arbitrary",)),
        name="inproj_t",
    )(x2d, nw, w_nat, w_t, dtb, *qtabs, *ktabs, cw, cb)


def _attn_t_kernel(qt_ref, k_ref, vt_ref, gt_ref, bias_ref, sink_ref, wo_ref, o_ref, kbuf, vtbuf, att_t, *, tb):
    blk = pl.program_id(1)
    cur = blk % 2
    prv = 1 - cur

    @pl.when(blk == 0)
    def _():
        kbuf[1, tb - WINDOW:tb, :] = jnp.zeros((WINDOW, KV_WIDTH), BF16)
        vtbuf[1, :, tb - WINDOW:tb] = jnp.zeros((KV_WIDTH, WINDOW), BF16)

    kbuf[cur] = k_ref[...].astype(BF16)
    vtbuf[cur] = vt_ref[...]
    seq_start = jnp.where(blk == 0, -jnp.inf, 0.0)

    for pr in range(tb // PAIR):
        c0 = pr * PAIR
        bias = bias_ref[...]
        if pr == 0:
            bias = jnp.concatenate([bias[0:WINDOW] + seq_start, bias[WINDOW:KEY_WIN]], axis=0)
        for h in range(N_KV_HEADS):
            hs = slice(h * HEAD_DIM, (h + 1) * HEAD_DIM)
            heads = [h * Q_PER_KV + g for g in range(Q_PER_KV)]
            if pr == 0:
                k_win = jnp.concatenate([kbuf[prv, tb - WINDOW:tb, hs], kbuf[cur, 0:PAIR, hs]], axis=0)
                vt_win = jnp.concatenate([vtbuf[prv, hs, tb - WINDOW:tb], vtbuf[cur, hs, 0:PAIR]], axis=1)
            else:
                k_win = kbuf[cur, c0 - WINDOW:c0 + PAIR, hs]
                vt_win = vtbuf[cur, hs, c0 - WINDOW:c0 + PAIR]
            q4 = jnp.concatenate([qt_ref[hd * HEAD_DIM:(hd + 1) * HEAD_DIM, c0:c0 + PAIR] for hd in heads], axis=1)
            s = jnp.dot(k_win, q4, preferred_element_type=F32) + bias
            sk = sink_ref[h:h + 1, :]
            m = jnp.maximum(jnp.max(s, axis=0, keepdims=True), sk)
            p = jnp.exp(s - m)
            denom = jnp.sum(p, axis=0, keepdims=True) + jnp.exp(sk - m)
            o = jnp.dot(vt_win, p.astype(BF16), preferred_element_type=F32) / denom
            for g, hd in enumerate(heads):
                rows = slice(hd * HEAD_DIM, (hd + 1) * HEAD_DIM)
                att_t[rows, c0:c0 + PAIR] = (
                    o[:, g * PAIR:(g + 1) * PAIR] * gt_ref[rows, c0:c0 + PAIR].astype(F32)).astype(BF16)

    o_ref[...] = lax.dot_general(att_t[...], wo_ref[...], (((0,), (0,)), ((), ())), preferred_element_type=F32)


def _attention_t(qt, k, vt, gt, bias, sink_rows, wo, *, bsz, seq, tb):
    nblk = seq // tb
    row = lambda b, j: (b * nblk + j, 0)
    col = lambda b, j: (0, b * nblk + j)
    return pl.pallas_call(
        functools.partial(_attn_t_kernel, tb=tb),
        grid=(bsz, nblk),
        in_specs=[pl.BlockSpec((ATT_WIDTH, tb), col),
                  pl.BlockSpec((tb, KV_WIDTH), row),
                  pl.BlockSpec((KV_WIDTH, tb), col),
                  pl.BlockSpec((ATT_WIDTH, tb), col),
                  _resident((KEY_WIN, Q_PER_KV * PAIR)),
                  _resident((N_KV_HEADS, Q_PER_KV * PAIR)),
                  _resident((ATT_WIDTH, D_MODEL))],
        out_specs=pl.BlockSpec((tb, D_MODEL), row),
        out_shape=jax.ShapeDtypeStruct((bsz * seq, D_MODEL), F32),
        scratch_shapes=[pltpu.VMEM((2, tb, KV_WIDTH), BF16),
                        pltpu.VMEM((2, KV_WIDTH, tb), BF16),
                        pltpu.VMEM((ATT_WIDTH, tb), BF16)],
        compiler_params=_compiler_params(("arbitrary", "arbitrary")),
        name="attention_t",
    )(qt, k, vt, gt, bias, sink_rows, wo)


def _window_bias():
    key_chunk = jnp.arange(KEY_WIN)[:, None] // CHUNK
    q_chunk = (jnp.arange(Q_PER_KV * PAIR)[None, :] % PAIR) // CHUNK
    ok = (key_chunk >= q_chunk) & (key_chunk <= q_chunk + WINDOW // CHUNK)
    return jnp.where(ok, 0.0, -jnp.inf).astype(F32)


SSD_Q = 64
PIECE = 32
QUAD = 4 * SSM_HEAD_DIM


def _split3(x):
    hi = x.astype(BF16).astype(F32)
    r1 = x - hi
    mid = r1.astype(BF16).astype(F32)
    lo = (r1 - mid).astype(BF16).astype(F32)
    return (hi + pltpu.roll(mid, PIECE, 1) + pltpu.roll(lo, 2 * PIECE, 1)).astype(BF16)


def _ssd_kernel(*refs, tc, out_row, conv_in_kernel):
    if conv_in_kernel:
        (xbc_ref, z_ref, dt_ref, x_ref, ba_ref, gates_ref, h0_ref, cs0_ref, cw_ref, cb_ref,
         a_ref, e3_ref, dx_ref, nw_ref, wso_ref, wout_ref, pnw_ref, eye_ref, tril_ref, qmask_ref,
         y_ref, nssm_ref, nconv_ref, ht, ybuf, xpad, xc) = refs
    else:
        (xc, z_ref, dt_ref, x_ref, ba_ref, gates_ref, h0_ref,
         a_ref, e3_ref, dx_ref, nw_ref, wso_ref, wout_ref, pnw_ref, eye_ref, tril_ref, qmask_ref,
         y_ref, nssm_ref, ht, ybuf) = refs
    blk = pl.program_id(1)
    nblk = pl.num_programs(1)

    @pl.when(blk == 0)
    def _():
        ht[...] = h0_ref[0].T

    if conv_in_kernel:
        pad0 = SUBLANES - (CONV_W - 1)
        xpad[pad0:SUBLANES, :] = cs0_ref[0]
        xpad[SUBLANES:SUBLANES + tc, :] = xbc_ref[...]
        acc = cb_ref[...] + xpad[pad0:pad0 + tc, :] * cw_ref[0:1, :]
        for tap in range(1, CONV_W):
            acc = acc + xpad[pad0 + tap:pad0 + tap + tc, :] * cw_ref[tap:tap + 1, :]
        xc[...] = _silu(acc)
        nconv_ref[0] = xpad[SUBLANES + out_row - (CONV_W - 1):SUBLANES + out_row, :]

    lane = lax.broadcasted_iota(jnp.int32, (SSD_Q, LANES), 1)
    ri = lax.broadcasted_iota(jnp.int32, (SSD_Q, SSD_Q), 0)
    rj = lax.broadcasted_iota(jnp.int32, (SSD_Q, SSD_Q), 1)
    tri = (ri >= rj).astype(BF16)
    a_row = a_ref[...]
    dx = dx_ref[...]
    nw = nw_ref[...]

    def expand(v):
        return jnp.dot(_split3(v), e3_ref[...], preferred_element_type=F32)

    def chunk(c, carry):
        r0 = pl.multiple_of(c * SSD_Q, SSD_Q)
        xs = xc[pl.ds(r0, SSD_Q), 0:SSM_WIDTH]
        bm = xc[pl.ds(r0, SSD_Q), SSM_WIDTH:SSM_WIDTH + BC_WIDTH].astype(BF16)
        cm = xc[pl.ds(r0, SSD_Q), SSM_WIDTH + BC_WIDTH:CONV_CH].astype(BF16)
        dt = dt_ref[pl.ds(r0, SSD_Q), :]
        cum3 = jnp.dot(tri, _split3(dt * a_row), preferred_element_type=F32)
        cum = jnp.where(lane < PIECE,
                        cum3 + pltpu.roll(cum3, LANES - PIECE, 1) + pltpu.roll(cum3, LANES - 2 * PIECE, 1), 0.0)
        dt_x = expand(dt)
        cum_x = expand(cum)
        ecum_x = jnp.exp(cum_x)
        cum_last = cum_x[SSD_Q - 1:SSD_Q, :]
        xdt = xs * dt_x
        xdt_b = xdt.astype(BF16)
        xw = (xdt * jnp.exp(cum_last - cum_x)).astype(BF16)
        cum_j = jnp.sum(cum_x * eye_ref[...], axis=0, keepdims=True)
        decay = jnp.exp(cum_x - cum_j + tril_ref[...])

        y_parts = []
        for g in range(SSM_GROUPS):
            cg = cm[:, g * SSM_STATE:(g + 1) * SSM_STATE]
            bg = bm[:, g * SSM_STATE:(g + 1) * SSM_STATE]
            cb4 = lax.dot_general(cg, jnp.concatenate([bg] * 4, axis=0), (((1,), (1,)), ((), ())),
                                  preferred_element_type=F32)
            gs = slice(g * GROUP_WIDTH, (g + 1) * GROUP_WIDTH)
            y_state = jnp.dot(cg, ht[:, gs].astype(BF16), preferred_element_type=F32) * ecum_x[:, gs]
            for half in range(GROUP_WIDTH // QUAD):
                qs = slice(g * GROUP_WIDTH + half * QUAD, g * GROUP_WIDTH + (half + 1) * QUAD)
                m4 = (cb4 * decay[:, qs]).astype(BF16)
                x4 = xdt_b[:, qs]
                bd = jnp.concatenate([x4 * qmask_ref[a:a + 1, :] for a in range(4)], axis=0)
                y_parts.append(jnp.dot(m4, bd, preferred_element_type=F32)
                               + y_state[:, half * QUAD:(half + 1) * QUAD])
            ht[:, gs] = (ht[:, gs] * ecum_x[SSD_Q - 1:SSD_Q, gs]
                         + jnp.dot(bg.T, xw[:, gs], preferred_element_type=F32))
        y = jnp.concatenate(y_parts, axis=1) + xs * dx
        yz = y * z_ref[pl.ds(r0, SSD_Q), :]
        for g in range(SSM_GROUPS):
            gs = slice(g * GROUP_WIDTH, (g + 1) * GROUP_WIDTH)
            yg = yz[:, gs]
            ms = jnp.mean(yg * yg, axis=-1, keepdims=True)
            ybuf[pl.ds(r0, SSD_Q), gs] = (yg * lax.rsqrt(ms + EPS) * nw[:, gs]).astype(BF16)
        return carry

    lax.fori_loop(0, tc // SSD_Q, chunk, 0, unroll=2)
    branch_s = jnp.dot(ybuf[...], wso_ref[...], preferred_element_type=F32)
    merged = gates_ref[:, 0:D_MODEL] * ba_ref[...] + gates_ref[:, D_MODEL:2 * D_MODEL] * branch_s
    out = jnp.dot(merged.astype(BF16), wout_ref[...], preferred_element_type=F32)
    ms = jnp.mean(out * out, axis=-1, keepdims=True)
    y_ref[...] = x_ref[...] + out * lax.rsqrt(ms + EPS) * pnw_ref[...]

    @pl.when(blk == nblk - 1)
    def _():
        nssm_ref[0] = ht[...].T


def _ssd_masks():
    i = jnp.arange(SSD_Q)[:, None]
    j = jnp.arange(SSM_WIDTH)[None, :] % SSD_Q
    eye = (i == j).astype(F32)
    tril = jnp.where(i >= j, 0.0, -jnp.inf).astype(F32)
    qmask = (jnp.arange(QUAD)[None, :] // SSM_HEAD_DIM == jnp.arange(4)[:, None]).astype(BF16)
    return eye, tril, qmask


def _ssd(xc, zs, dt, x2d, branch_a, gates, ssm0, conv0, wts, *, bsz, seq, tc, l_real, conv_in_kernel):
    nblk = seq // tc
    assert not conv_in_kernel or nblk == 1
    row = lambda b, j: (b * nblk + j, 0)
    per_seq = lambda b, j: (b, 0, 0)
    hp = SSM_HEADS * SSM_HEAD_DIM
    kern = functools.partial(_ssd_kernel, tc=tc, out_row=l_real, conv_in_kernel=conv_in_kernel)
    operands = [xc, zs, dt, x2d, branch_a, gates, ssm0]
    in_specs = [pl.BlockSpec((tc, CONV_CH), row),
                pl.BlockSpec((tc, SSM_WIDTH), row),
                pl.BlockSpec((tc, LANES), row),
                pl.BlockSpec((tc, D_MODEL), row),
                pl.BlockSpec((tc, D_MODEL), row),
                pl.BlockSpec((tc, 2 * D_MODEL), row),
                pl.BlockSpec((1, hp, SSM_STATE), per_seq)]
    out_specs = [pl.BlockSpec((tc, D_MODEL), row), pl.BlockSpec((1, hp, SSM_STATE), per_seq)]
    out_shape = [jax.ShapeDtypeStruct((bsz * seq, D_MODEL), F32),
                 jax.ShapeDtypeStruct((bsz, hp, SSM_STATE), F32)]
    scratch = [pltpu.VMEM((SSM_STATE, SSM_WIDTH), F32), pltpu.VMEM((tc, SSM_WIDTH), BF16)]
    if conv_in_kernel:
        operands += [conv0, wts["conv_w"], wts["conv_b"]]
        in_specs += [pl.BlockSpec((1, CONV_W - 1, CONV_CH), per_seq),
                     _resident((CONV_W, CONV_CH)), _resident((1, CONV_CH))]
        out_specs.append(pl.BlockSpec((1, CONV_W - 1, CONV_CH), per_seq))
        out_shape.append(jax.ShapeDtypeStruct((bsz, CONV_W - 1, CONV_CH), F32))
        scratch += [pltpu.VMEM((SUBLANES + tc, CONV_CH), F32), pltpu.VMEM((tc, CONV_CH), F32)]
    consts = [wts["a_row"], wts["e3"], wts["d_x"], wts["ssm_norm_w"], wts["w_ssm_o"], wts["w_out"],
              wts["post_norm_w"], *_ssd_masks()]
    operands += consts
    in_specs += [_resident(c.shape) for c in consts]
    return pl.pallas_call(
        kern,
        grid=(bsz, nblk),
        in_specs=in_specs,
        out_specs=out_specs,
        out_shape=out_shape,
        scratch_shapes=scratch,
        compiler_params=_compiler_params(("arbitrary", "arbitrary")),
        name="ssd",
    )(*operands)


def _rope_angles(pos):
    half = ROT_DIM // 2
    inv = ROPE_THETA ** (-(jnp.arange(half, dtype=F32) * 2.0 / ROT_DIM))
    ang = pos.astype(F32)[:, None] * inv[None, :]
    return jnp.cos(ang), jnp.sin(ang)


def _rope_tables(pos):
    cos, sin = _rope_angles(pos)
    half = ROT_DIM // 2
    n = pos.shape[0]
    rest = HEAD_DIM - ROT_DIM
    c = jnp.concatenate([cos, cos, jnp.ones((n, rest), F32)], axis=1)
    s1 = jnp.concatenate([-sin, jnp.zeros((n, half + rest), F32)], axis=1)
    s2 = jnp.concatenate([jnp.zeros((n, half), F32), sin, jnp.zeros((n, rest), F32)], axis=1)
    rep = LANES // HEAD_DIM
    return tuple(jnp.tile(t, (1, rep)) for t in (c, s1, s2))


def _layer_prompt(x, pos, wts, *, tm, tb, tc):
    bsz, seq, _ = x.shape
    x2d = x.reshape(bsz * seq, D_MODEL)
    cos, sin = _rope_angles(pos)
    qt, gt, vt, k, v, zs, xc, gates, dt, conv_tail = _inproj_t(
        x2d, wts["pre_norm_w"], wts["w_nat"], wts["w_t"], wts["dt_bias"], (cos.T, sin.T), _rope_tables(pos),
        wts["conv_w"], wts["conv_b"], seq, tm)
    sink_rows = jnp.repeat(wts["sinks"].reshape(N_KV_HEADS, Q_PER_KV), PAIR, axis=1)
    branch_a = _attention_t(qt, k, vt, gt, _window_bias(), sink_rows, wts["w_attn_o"], bsz=bsz, seq=seq, tb=tb)
    n_keep = min(WINDOW, seq)
    keep = lambda t: t.reshape(bsz, seq, N_KV_HEADS, HEAD_DIM)[:, seq - n_keep:]
    zero_ssm = jnp.zeros((bsz, SSM_HEADS * SSM_HEAD_DIM, SSM_STATE), F32)
    y, new_ssm = _ssd(xc, zs, dt, x2d, branch_a, gates, zero_ssm, None, wts,
                      bsz=bsz, seq=seq, tc=tc, l_real=seq, conv_in_kernel=False)
    return (y.reshape(bsz, seq, D_MODEL), keep(k), keep(v), conv_tail[:, SUBLANES - (CONV_W - 1):],
            new_ssm.reshape(bsz, SSM_HEADS, SSM_HEAD_DIM, SSM_STATE))


def _layer_sample(x, pos, cache_k, cache_v, conv0, ssm0, wts, *, tm):
    bsz, seq, _ = x.shape
    x2d = x.reshape(bsz * seq, D_MODEL)
    q, kv, g, zs, xbc, gates, dt = _inproj(x2d, wts["pre_norm_w"], wts["w_cat"], wts["dt_bias"], tm)
    sink_cols = jnp.broadcast_to(
        jnp.repeat(wts["sinks"].reshape(N_KV_HEADS, Q_PER_KV), seq, axis=1)[:, :, None],
        (N_KV_HEADS, Q_PER_KV * seq, LANES))
    branch_a, new_k, new_v = _attention(
        q, kv, g, cache_k, cache_v, _rope_tables(pos), sink_cols, wts["w_attn_o"],
        bsz=bsz, seq=seq, tb=seq, ch=seq, mask_first=False)
    seq_p = -(-seq // SSD_Q) * SSD_Q
    padrows = lambda t: jnp.pad(t.reshape(bsz, seq, -1), ((0, 0), (0, seq_p - seq), (0, 0))).reshape(
        bsz * seq_p, -1)
    y, new_ssm, new_conv = _ssd(
        padrows(xbc), padrows(zs), padrows(dt), padrows(x2d), padrows(branch_a), padrows(gates), ssm0, conv0, wts,
        bsz=bsz, seq=seq_p, tc=seq_p, l_real=seq, conv_in_kernel=True)
    n_keep = new_k.shape[1]
    return (y.reshape(bsz, seq_p, D_MODEL)[:, :seq],
            new_k.reshape(bsz, n_keep, N_KV_HEADS, HEAD_DIM), new_v.reshape(bsz, n_keep, N_KV_HEADS, HEAD_DIM),
            new_conv, new_ssm.reshape(bsz, SSM_HEADS, SSM_HEAD_DIM, SSM_STATE))


def _prep_weights(pre_norm_w, w_in, conv_w, conv_b, dt_bias, a_log, d_skip, sinks, ssm_norm_w, w_attn_o,
                  w_ssm_o, w_out, post_norm_w):
    g_off = ATT_WIDTH + 2 * KV_WIDTH
    z_off = g_off + ATT_WIDTH
    dt_off = z_off + SSM_WIDTH + CONV_CH
    wb = w_in.astype(BF16)
    dt_cols = jnp.concatenate([wb[:, dt_off:dt_off + SSM_HEADS],
                               jnp.zeros((D_MODEL, LANES - SSM_HEADS), BF16)], axis=1)
    w_cat = jnp.concatenate([wb[:, :dt_off], wb[:, dt_off + SSM_HEADS:], dt_cols], axis=1)
    w_nat = jnp.concatenate([wb[:, ATT_WIDTH:g_off], wb[:, z_off:dt_off], wb[:, dt_off + SSM_HEADS:], dt_cols],
                            axis=1)
    w_t = jnp.concatenate([wb[:, :ATT_WIDTH], wb[:, g_off:z_off], wb[:, ATT_WIDTH + KV_WIDTH:g_off]], axis=1).T
    pad_heads = lambda v: jnp.pad(v.astype(F32), (0, LANES - SSM_HEADS)).reshape(1, LANES)
    k_idx = jnp.arange(LANES)[:, None]
    c_idx = jnp.arange(SSM_WIDTH)[None, :]
    e3 = ((k_idx < 3 * PIECE) & (k_idx % PIECE == c_idx // SSM_HEAD_DIM)).astype(BF16)
    return dict(
        pre_norm_w=pre_norm_w.reshape(1, D_MODEL), w_cat=w_cat, w_nat=w_nat, w_t=w_t, dt_bias=pad_heads(dt_bias),
        conv_w=conv_w, conv_b=conv_b.reshape(1, CONV_CH), a_row=pad_heads(-jnp.exp(a_log.astype(F32))),
        e3=e3, d_x=jnp.repeat(d_skip.astype(F32), SSM_HEAD_DIM).reshape(1, SSM_WIDTH), sinks=sinks.astype(F32),
        ssm_norm_w=ssm_norm_w.reshape(1, SSM_WIDTH), w_attn_o=w_attn_o.astype(BF16),
        w_ssm_o=w_ssm_o.astype(BF16), w_out=w_out.astype(BF16), post_norm_w=post_norm_w.reshape(1, D_MODEL))


def kernel(x_prompt, x_sample, cache_k, cache_v, state_conv, state_ssm, pre_norm_w, w_in, conv_w, conv_b, dt_bias,
           a_log, d_skip, sinks, ssm_norm_w, w_attn_o, w_ssm_o, w_out, post_norm_w):
    depth = w_in.shape[0]
    lp = x_prompt.shape[1]
    bs, ls, _ = x_sample.shape
    pos_p = jnp.arange(lp, dtype=F32)
    pos_s = PAST_LEN + jnp.arange(ls, dtype=F32)
    hp = SSM_HEADS * SSM_HEAD_DIM
    yp, ys = x_prompt, x_sample
    outs = [[] for _ in range(8)]
    for layer in range(depth):
        wts = _prep_weights(pre_norm_w[layer], w_in[layer], conv_w[layer], conv_b[layer], dt_bias[layer],
                            a_log[layer], d_skip[layer], sinks[layer], ssm_norm_w[layer], w_attn_o[layer],
                            w_ssm_o[layer], w_out[layer], post_norm_w[layer])
        yp, kp, vp, cp, sp = _layer_prompt(yp, pos_p, wts, tm=256, tb=256, tc=256)
        ys, ks, vs, cs, ss = _layer_sample(ys, pos_s,
                                           cache_k[layer].reshape(bs, WINDOW, KV_WIDTH),
                                           cache_v[layer].reshape(bs, WINDOW, KV_WIDTH),
                                           state_conv[layer], state_ssm[layer].reshape(bs, hp, SSM_STATE), wts,
                                           tm=256)
        for lst, val in zip(outs, (kp, vp, cp, sp, ks, vs, cs, ss)):
            lst.append(val)
    return (yp, ys) + tuple(jnp.stack(lst) for lst in outs)
```

```python
import functools

import jax
import jax.numpy as jnp
import numpy as np
from jax import lax
from jax.experimental import pallas as pl
from jax.experimental.pallas import tpu as pltpu

F32 = jnp.float32
BF16 = jnp.bfloat16

D_MODEL = 1024
PAST_LEN = 2048
CHUNK = 64
N_Q_HEADS = 16
N_KV_HEADS = 4
HEAD_DIM = 64
Q_PER_KV = N_Q_HEADS // N_KV_HEADS
ATT_WIDTH = N_Q_HEADS * HEAD_DIM
KV_WIDTH = N_KV_HEADS * HEAD_DIM
WINDOW = 128
ROT_DIM = HEAD_DIM // 4
ROPE_THETA = 500000.0
SSM_WIDTH = 2 * D_MODEL
SSM_HEAD_DIM = 64
SSM_HEADS = SSM_WIDTH // SSM_HEAD_DIM
SSM_GROUPS = 4
SSM_STATE = 128
GROUP_WIDTH = SSM_WIDTH // SSM_GROUPS
BC_WIDTH = SSM_GROUPS * SSM_STATE
CONV_W = 4
CONV_CH = SSM_WIDTH + 2 * BC_WIDTH
EPS = 1e-6
LOG2_E = 1.4426950408889634

LANES = 128
SUBLANES = 8
VMEM_LIMIT = 56 * 1024 * 1024

OFF_KV = 0
OFF_Z = OFF_KV + 2 * KV_WIDTH
OFF_XBC = OFF_Z + SSM_WIDTH
OFF_GATES = OFF_XBC + CONV_CH
OFF_DT = OFF_GATES + 2 * D_MODEL
OFF_Q = OFF_DT + LANES
OFF_G = OFF_Q + ATT_WIDTH
IN_WIDTH_PADDED = OFF_G + ATT_WIDTH
MM_COLS = 256


def _sigmoid(x):
    return 0.5 * jnp.tanh(0.5 * x) + 0.5


def _silu(x):
    half = 0.5 * x
    return half * jnp.tanh(half) + half


def _softplus(x):
    return jnp.maximum(x, 0.0) + jnp.log1p(jnp.exp(-jnp.abs(x)))


def _compiler_params(semantics):
    return pltpu.CompilerParams(dimension_semantics=semantics, vmem_limit_bytes=VMEM_LIMIT)


def _resident(shape):
    zeros = (0,) * len(shape)
    return pl.BlockSpec(shape, lambda *_: zeros, pipeline_mode=pl.Buffered(1))


def _inproj_kernel(x_ref, nw_ref, w_ref, dtb_ref, q_ref, kv_ref, g_ref, z_ref, xbc_ref, gates_ref, dt_ref):
    x = x_ref[...]
    ms = jnp.mean(x * x, axis=-1, keepdims=True)
    h = (x * lax.rsqrt(ms + EPS) * nw_ref[...]).astype(BF16)

    def section(out_ref, off, width, act):
        step = min(MM_COLS, width)
        for n0 in range(0, width, step):
            r = jnp.dot(h, w_ref[:, off + n0:off + n0 + step], preferred_element_type=F32)
            out_ref[:, n0:n0 + step] = act(r)

    ident = lambda r: r
    section(q_ref, OFF_Q, ATT_WIDTH, ident)
    section(kv_ref, OFF_KV, 2 * KV_WIDTH, ident)
    section(g_ref, OFF_G, ATT_WIDTH, _silu)
    section(z_ref, OFF_Z, SSM_WIDTH, _silu)
    section(xbc_ref, OFF_XBC, CONV_CH, ident)
    section(gates_ref, OFF_GATES, 2 * D_MODEL, _sigmoid)
    lane = lax.broadcasted_iota(jnp.int32, (1, LANES), 1)
    section(dt_ref, OFF_DT, LANES,
            lambda r: jnp.where(lane < SSM_HEADS, _softplus(r + dtb_ref[...]), 0.0))


def _inproj(x2d, nw, w_cat, dtb, tm):
    m = x2d.shape[0]
    widths = (ATT_WIDTH, 2 * KV_WIDTH, ATT_WIDTH, SSM_WIDTH, CONV_CH, 2 * D_MODEL, LANES)
    return pl.pallas_call(
        _inproj_kernel,
        grid=(m // tm,),
        in_specs=[pl.BlockSpec((tm, D_MODEL), lambda i: (i, 0)),
                  _resident((1, D_MODEL)),
                  _resident((D_MODEL, IN_WIDTH_PADDED)),
                  _resident((1, LANES))],
        out_specs=[pl.BlockSpec((tm, w), lambda i: (i, 0)) for w in widths],
        out_shape=[jax.ShapeDtypeStruct((m, w), F32) for w in widths],
        compiler_params=_compiler_params(("arbitrary",)),
        name="inproj",
    )(x2d, nw, w_cat, dtb)


def _attn_kernel(q_ref, kv_ref, g_ref, ck_ref, cv_ref, cos_ref, s1_ref, s2_ref, sink_ref, wo_ref,
                 o_ref, nk_ref, nv_ref, qbuf, kbuf, vbuf, att, *, tb, ch, nblk, mask_first, n_keep):
    blk = pl.program_id(1)

    @pl.when(blk == 0)
    def _():
        kbuf[0:WINDOW, :] = ck_ref[0]
        vbuf[0:WINDOW, :] = cv_ref[0]

    cos = cos_ref[...]
    s1 = s1_ref[...]
    s2 = s2_ref[...]

    def rope(x):
        return x * cos + pltpu.roll(x, LANES - ROT_DIM // 2, 1) * s1 + pltpu.roll(x, ROT_DIM // 2, 1) * s2

    scale = HEAD_DIM ** -0.5
    for s in range(ATT_WIDTH // LANES):
        sl = slice(s * LANES, (s + 1) * LANES)
        qbuf[:, sl] = (rope(q_ref[:, sl]) * scale).astype(BF16)
    for s in range(KV_WIDTH // LANES):
        sl = slice(s * LANES, (s + 1) * LANES)
        kbuf[WINDOW:WINDOW + tb, sl] = rope(kv_ref[:, sl])
    vbuf[WINDOW:WINDOW + tb, :] = kv_ref[:, KV_WIDTH:2 * KV_WIDTH]

    nkeys = WINDOW + ch
    for c in range(tb // ch):
        r0 = c * ch
        if mask_first:
            col_chunk = lax.broadcasted_iota(jnp.int32, (1, nkeys), 1) // CHUNK
            valid = (blk * (tb // ch) + c - WINDOW // CHUNK + col_chunk) >= 0
        for h in range(N_KV_HEADS):
            hs = slice(h * HEAD_DIM, (h + 1) * HEAD_DIM)
            qs = jnp.concatenate(
                [qbuf[r0:r0 + ch, (h * Q_PER_KV + g) * HEAD_DIM:(h * Q_PER_KV + g + 1) * HEAD_DIM]
                 for g in range(Q_PER_KV)], axis=0)
            kb = kbuf[r0:r0 + nkeys, hs].astype(BF16)
            vb = vbuf[r0:r0 + nkeys, hs].astype(BF16)
            s = lax.dot_general(qs, kb, (((1,), (1,)), ((), ())), preferred_element_type=F32)
            if mask_first:
                s = jnp.where(valid, s, -jnp.inf)
            sk = sink_ref[h][:, 0:1]
            m = jnp.maximum(jnp.max(s, axis=-1, keepdims=True), sk)
            p = jnp.exp(s - m)
            denom = jnp.sum(p, axis=-1, keepdims=True) + jnp.exp(sk - m)
            o = jnp.dot(p.astype(BF16), vb, preferred_element_type=F32) / denom
            for g in range(Q_PER_KV):
                head = h * Q_PER_KV + g
                att[r0:r0 + ch, head * HEAD_DIM:(head + 1) * HEAD_DIM] = o[g * ch:(g + 1) * ch, :]

    a = (att[...] * g_ref[...]).astype(BF16)
    o_ref[...] = jnp.dot(a, wo_ref[...], preferred_element_type=F32)

    @pl.when(blk == nblk - 1)
    def _():
        nk_ref[0] = kbuf[WINDOW + tb - n_keep:WINDOW + tb, :]
        nv_ref[0] = vbuf[WINDOW + tb - n_keep:WINDOW + tb, :]

    if nblk > 1:
        kbuf[0:WINDOW, :] = kbuf[tb:tb + WINDOW, :]
        vbuf[0:WINDOW, :] = vbuf[tb:tb + WINDOW, :]


def _attention(q, kv, g, cache_k, cache_v, tables, sink_cols, wo, *, bsz, seq, tb, ch, mask_first):
    nblk = seq // tb
    n_keep = min(WINDOW, seq)
    cos, s1, s2 = tables
    row = lambda b, j: (b * nblk + j, 0)
    kern = functools.partial(_attn_kernel, tb=tb, ch=ch, nblk=nblk, mask_first=mask_first, n_keep=n_keep)
    return pl.pallas_call(
        kern,
        grid=(bsz, nblk),
        in_specs=[pl.BlockSpec((tb, ATT_WIDTH), row),
                  pl.BlockSpec((tb, 2 * KV_WIDTH), row),
                  pl.BlockSpec((tb, ATT_WIDTH), row),
                  pl.BlockSpec((1, WINDOW, KV_WIDTH), lambda b, j: (b, 0, 0)),
                  pl.BlockSpec((1, WINDOW, KV_WIDTH), lambda b, j: (b, 0, 0)),
                  pl.BlockSpec((tb, LANES), lambda b, j: (j, 0)),
                  pl.BlockSpec((tb, LANES), lambda b, j: (j, 0)),
                  pl.BlockSpec((tb, LANES), lambda b, j: (j, 0)),
                  _resident((N_KV_HEADS, Q_PER_KV * ch, LANES)),
                  _resident((ATT_WIDTH, D_MODEL))],
        out_specs=[pl.BlockSpec((tb, D_MODEL), row),
                   pl.BlockSpec((1, n_keep, KV_WIDTH), lambda b, j: (b, 0, 0)),
                   pl.BlockSpec((1, n_keep, KV_WIDTH), lambda b, j: (b, 0, 0))],
        out_shape=[jax.ShapeDtypeStruct((bsz * seq, D_MODEL), F32),
                   jax.ShapeDtypeStruct((bsz, n_keep, KV_WIDTH), F32),
                   jax.ShapeDtypeStruct((bsz, n_keep, KV_WIDTH), F32)],
        scratch_shapes=[pltpu.VMEM((tb, ATT_WIDTH), BF16),
                        pltpu.VMEM((WINDOW + tb, KV_WIDTH), F32),
                        pltpu.VMEM((WINDOW + tb, KV_WIDTH), F32),
                        pltpu.VMEM((tb, ATT_WIDTH), F32)],
        compiler_params=_compiler_params(("arbitrary", "arbitrary")),
        name="attention",
    )(q, kv, g, cache_k, cache_v, cos, s1, s2, sink_cols, wo)


PN_K = OFF_KV
PN_V = PN_K + KV_WIDTH
PN_Z = OFF_Z
PN_XBC = OFF_XBC
PN_GATES = OFF_GATES
PN_DT = OFF_DT
PT_Q = 0
PT_G = PT_Q + ATT_WIDTH
PT_V = PT_G + ATT_WIDTH
PT_WIDTH = PT_V + KV_WIDTH
HALF_ROT = ROT_DIM // 2
PAIR = 2 * CHUNK
KEY_WIN = WINDOW + PAIR


def _inproj_t_kernel(x_ref, nw_ref, w_ref, dtb_ref, cosq_ref, sinq_ref, cosk_ref, s1k_ref, s2k_ref,
                     cw_ref, cb_ref,
                     qt_ref, gt_ref, vt_ref, k_ref, v_ref, z_ref, xc_ref, gates_ref, dt_ref, ctail_ref,
                     cprev, wt_ref, *, tm, nper):
    step = pl.program_id(0)
    seq_start = step % nper == 0
    cur = step % 2

    @pl.when(step == 0)
    def _():
        for dst, src, width in ((PT_Q, OFF_Q, ATT_WIDTH), (PT_G, OFF_G, ATT_WIDTH), (PT_V, PN_V, KV_WIDTH)):
            for n0 in range(0, width, MM_COLS):
                wt_ref[dst + n0:dst + n0 + MM_COLS, :] = w_ref[:, src + n0:src + n0 + MM_COLS].T

    x = x_ref[...]
    ms = jnp.mean(x * x, axis=-1, keepdims=True)
    h = (x * lax.rsqrt(ms + EPS) * nw_ref[...]).astype(BF16)

    def mm_t(off, rows):
        return lax.dot_general(wt_ref[off:off + rows, :], h, (((1,), (1,)), ((), ())),
                               preferred_element_type=F32)

    cosq = cosq_ref[...]
    sinq = sinq_ref[...]
    scale = HEAD_DIM ** -0.5


    def q_tail(n0, r):
        r = r * scale
        for a in range(MM_COLS // HEAD_DIM):
            base = a * HEAD_DIM
            x1 = r[base:base + HALF_ROT]
            x2 = r[base + HALF_ROT:base + ROT_DIM]
            rot = jnp.concatenate([x1 * cosq - x2 * sinq, x2 * cosq + x1 * sinq], axis=0)
            qt_ref[n0 + base:n0 + base + ROT_DIM, :] = rot.astype(BF16)
            qt_ref[n0 + base + ROT_DIM:n0 + base + HEAD_DIM, :] = r[base + ROT_DIM:base + HEAD_DIM].astype(BF16)

    def g_tail(n0, r):
        gt_ref[n0:n0 + MM_COLS, :] = _silu(r).astype(BF16)

    def vt_tail(r):
        vt_ref[...] = r.astype(BF16)

    def mm_n(off, width):
        return jnp.dot(h, w_ref[:, off:off + width], preferred_element_type=F32)

    def nat_tail(out_ref, n0, width, act, r):
        out_ref[:, n0:n0 + width] = act(r).astype(out_ref.dtype)

    def k_tail(k_all):
        for s in range(KV_WIDTH // LANES):
            kk = k_all[:, s * LANES:(s + 1) * LANES]
            k_ref[:, s * LANES:(s + 1) * LANES] = (
                kk * cosk_ref[...] + pltpu.roll(kk, LANES - HALF_ROT, 1) * s1k_ref[...]
                + pltpu.roll(kk, HALF_ROT, 1) * s2k_ref[...])

    row8 = lax.broadcasted_iota(jnp.int32, (SUBLANES, MM_COLS), 0)

    def conv_tail(n0, r):
        cols = slice(n0, n0 + MM_COLS)
        prev =jnp.where(seq_start, 0.0, cprev[cur, :, cols])
        acc = cb_ref[:, cols] + r * cw_ref[CONV_W - 1:CONV_W, cols]
        for k in range(1, CONV_W):
            rolled = pltpu.roll(r, k, 0)
            top = jnp.where(row8 < k, pltpu.roll(prev, k, 0), rolled[0:SUBLANES])
            shifted = jnp.concatenate([top, rolled[SUBLANES:tm]], axis=0)
            acc = acc + shifted * cw_ref[CONV_W - 1 - k:CONV_W - k, cols]
        xc_ref[:, cols] = _silu(acc).astype(xc_ref.dtype)
        cprev[1 - cur, :, cols] = r[tm - SUBLANES:tm]
        ctail_ref[0, :, cols] = r[tm - SUBLANES:tm]

    lane = lax.broadcasted_iota(jnp.int32, (1, LANES), 1)
    ident = lambda r: r
    dt_act = lambda r: jnp.where(lane < SSM_HEADS, _softplus(r + dtb_ref[...]), 0.0)
    P = functools.partial
    light = ([(P(mm_t, PT_Q + n0, MM_COLS), P(q_tail, n0)) for n0 in range(0, ATT_WIDTH, MM_COLS)]
             + [(P(mm_t, PT_V, KV_WIDTH), vt_tail),
                (P(mm_n, PN_K, KV_WIDTH), k_tail),
                (P(mm_n, PN_V, KV_WIDTH), P(nat_tail, v_ref, 0, KV_WIDTH, ident)),
                (P(mm_n, PN_DT, LANES), P(nat_tail, dt_ref, 0, LANES, dt_act))])
    medium = ([(P(mm_n, PN_Z + n0, MM_COLS), P(nat_tail, z_ref, n0, MM_COLS, _silu))
               for n0 in range(0, SSM_WIDTH, MM_COLS)]
              + [(P(mm_t, PT_G + n0, MM_COLS), P(g_tail, n0)) for n0 in range(0, ATT_WIDTH, MM_COLS)]
              + [(P(mm_n, PN_GATES + n0, MM_COLS), P(nat_tail, gates_ref, n0, MM_COLS, _sigmoid))
                 for n0 in range(0, 2 * D_MODEL, MM_COLS)])
    heavy = [(P(mm_n, PN_XBC + n0, MM_COLS), P(conv_tail, n0)) for n0 in range(0, CONV_CH, MM_COLS)]
    order = [grp[idx] for idx in range(max(len(light), len(medium), len(heavy)))
             for grp in (heavy, light, medium) if idx < len(grp)]
    pending = order[0][0]()
    for idx, (_, tail) in enumerate(order):
        nxt = order[idx + 1][0]() if idx + 1 < len(order) else None
        tail(pending)
        pending = nxt


def _inproj_t(x2d, nw, w_cat, dtb, qtabs, ktabs, cw, cb, seq, tm):
    m = x2d.shape[0]
    nper = seq // tm
    row = lambda i: (i, 0)
    col = lambda i: (0, i)
    nat_widths = (KV_WIDTH, KV_WIDTH, SSM_WIDTH, CONV_CH, 2 * D_MODEL, LANES)
    nat_dtypes = (F32, F32, BF16, BF16, BF16, F32)
    return pl.pallas_call(
        functools.partial(_inproj_t_kernel, tm=tm, nper=nper),
        grid=(m // tm,),
        in_specs=[pl.BlockSpec((tm, D_MODEL), row),
                  _resident((1, D_MODEL)),
                  _resident((D_MODEL, IN_WIDTH_PADDED)),
                  _resident((1, LANES)),
                  pl.BlockSpec((HALF_ROT, tm), lambda i: (0, i % nper)),
                  pl.BlockSpec((HALF_ROT, tm), lambda i: (0, i % nper)),
                  pl.BlockSpec((tm, LANES), lambda i: (i % nper, 0)),
                  pl.BlockSpec((tm, LANES), lambda i: (i % nper, 0)),
                  pl.BlockSpec((tm, LANES), lambda i: (i % nper, 0)),
                  _resident((CONV_W, CONV_CH)),
                  _resident((1, CONV_CH))],
        out_specs=[pl.BlockSpec((ATT_WIDTH, tm), col),
                   pl.BlockSpec((ATT_WIDTH, tm), col),
                   pl.BlockSpec((KV_WIDTH, tm), col)]
                  + [pl.BlockSpec((tm, w), row) for w in nat_widths]
                  + [pl.BlockSpec((1, SUBLANES, CONV_CH), lambda i: (i // nper, 0, 0))],
        out_shape=[jax.ShapeDtypeStruct((ATT_WIDTH, m), BF16),
                   jax.ShapeDtypeStruct((ATT_WIDTH, m), BF16),
                   jax.ShapeDtypeStruct((KV_WIDTH, m), BF16)]
                  + [jax.ShapeDtypeStruct((m, w), d) for w, d in zip(nat_widths, nat_dtypes)]
                  + [jax.ShapeDtypeStruct((m // seq, SUBLANES, CONV_CH), F32)],
        scratch_shapes=[pltpu.VMEM((2, SUBLANES, CONV_CH), F32), pltpu.VMEM((PT_WIDTH, D_MODEL), BF16)],
        compiler_params=_compiler_params(("arbitrary",)),
        name="inproj_t",
    )(x2d, nw, w_cat, dtb, *qtabs, *ktabs, cw, cb)


def _attn_t_kernel(qt_ref, k_ref, vt_ref, gt_ref, bias_ref, sink_ref, wo_ref, o_ref, kbuf, vtbuf, att_t, *, tb):
    blk = pl.program_id(1)
    cur = blk % 2
    prv = 1 - cur

    @pl.when(blk == 0)
    def _():
        kbuf[1, tb - WINDOW:tb, :] = jnp.zeros((WINDOW, KV_WIDTH), BF16)
        vtbuf[1, :, tb - WINDOW:tb] = jnp.zeros((KV_WIDTH, WINDOW), BF16)

    kbuf[cur] = k_ref[...].astype(BF16)
    vtbuf[cur] = vt_ref[...]
    seq_start = jnp.where(blk == 0, -jnp.inf, 0.0)

    n_sub = Q_PER_KV
    width = n_sub * PAIR

    def scores(pr, h, part):
        c0 = pr * PAIR
        hs = slice(h * HEAD_DIM, (h + 1) * HEAD_DIM)
        heads = [h * Q_PER_KV + part * n_sub + g for g in range(n_sub)]
        if pr == 0:
            k_win = jnp.concatenate([kbuf[prv, tb - WINDOW:tb, hs], kbuf[cur, 0:PAIR, hs]], axis=0)
            vt_win = jnp.concatenate([vtbuf[prv, hs, tb - WINDOW:tb], vtbuf[cur, hs, 0:PAIR]], axis=1)
        else:
            k_win = kbuf[cur, c0 - WINDOW:c0 + PAIR, hs]
            vt_win = vtbuf[cur, hs, c0 - WINDOW:c0 + PAIR]
        qn = jnp.concatenate([qt_ref[hd * HEAD_DIM:(hd + 1) * HEAD_DIM, c0:c0 + PAIR] for hd in heads], axis=1)
        s = jnp.dot(k_win, qn, preferred_element_type=F32)
        return s, vt_win, heads, c0, h, part

    def finish(s, vt_win, heads, c0, h, part):
        bias = bias_ref[:, 0:width]
        if c0 == 0:
            bias = jnp.concatenate([bias[0:WINDOW] + seq_start, bias[WINDOW:KEY_WIN]], axis=0)
        s = s + bias
        sk = sink_ref[h:h + 1, part * width:(part + 1) * width]
        m = jnp.maximum(jnp.max(s, axis=0, keepdims=True), sk)
        p = jnp.exp(s - m)
        denom = jnp.sum(p, axis=0, keepdims=True) + jnp.exp(sk - m)
        o = jnp.dot(vt_win, p.astype(BF16), preferred_element_type=F32) / denom
        for g, hd in enumerate(heads):
            rows = slice(hd * HEAD_DIM, (hd + 1) * HEAD_DIM)
            att_t[rows, c0:c0 + PAIR] = (
                o[:, g * PAIR:(g + 1) * PAIR] * gt_ref[rows, c0:c0 + PAIR].astype(F32)).astype(BF16)

    units = [(pr, h, part) for pr in range(tb // PAIR) for h in range(N_KV_HEADS)
             for part in range(Q_PER_KV // n_sub)]
    pending = scores(*units[0])
    for idx in range(len(units)):
        nxt = scores(*units[idx + 1]) if idx + 1 < len(units) else None
        finish(*pending)
        pending = nxt

    o_ref[...] = lax.dot_general(att_t[...], wo_ref[...], (((0,), (0,)), ((), ())), preferred_element_type=F32)


def _attention_t(qt, k, vt, gt, bias, sink_rows, wo, *, bsz, seq, tb):
    nblk = seq // tb
    row = lambda b, j: (b * nblk + j, 0)
    col = lambda b, j: (0, b * nblk + j)
    return pl.pallas_call(
        functools.partial(_attn_t_kernel, tb=tb),
        grid=(bsz, nblk),
        in_specs=[pl.BlockSpec((ATT_WIDTH, tb), col),
                  pl.BlockSpec((tb, KV_WIDTH), row),
                  pl.BlockSpec((KV_WIDTH, tb), col),
                  pl.BlockSpec((ATT_WIDTH, tb), col),
                  _resident((KEY_WIN, Q_PER_KV * PAIR)),
                  _resident((N_KV_HEADS, Q_PER_KV * PAIR)),
                  _resident((ATT_WIDTH, D_MODEL))],
        out_specs=pl.BlockSpec((tb, D_MODEL), row),
        out_shape=jax.ShapeDtypeStruct((bsz * seq, D_MODEL), F32),
        scratch_shapes=[pltpu.VMEM((2, tb, KV_WIDTH), BF16),
                        pltpu.VMEM((2, KV_WIDTH, tb), BF16),
                        pltpu.VMEM((ATT_WIDTH, tb), BF16)],
        compiler_params=_compiler_params(("arbitrary", "arbitrary")),
        name="attention_t",
    )(qt, k, vt, gt, bias, sink_rows, wo)


def _window_bias():
    key_chunk = np.arange(KEY_WIN)[:, None] // CHUNK
    q_chunk = (np.arange(Q_PER_KV * PAIR)[None, :] % PAIR) // CHUNK
    ok = (key_chunk >= q_chunk) & (key_chunk <= q_chunk + WINDOW // CHUNK)
    return jnp.asarray(np.where(ok, 0.0, -np.inf), F32)


SSD_Q = 64
PIECE = 32
QUAD = 4 * SSM_HEAD_DIM


def _split3(x):
    hi = x.astype(BF16).astype(F32)
    r1 = x - hi
    mid = r1.astype(BF16).astype(F32)
    lo = (r1 - mid).astype(BF16).astype(F32)
    return (hi + pltpu.roll(mid, PIECE, 1) + pltpu.roll(lo, 2 * PIECE, 1)).astype(BF16)


def _ssd_kernel(*refs, tc, out_row, conv_in_kernel):
    if conv_in_kernel:
        (xbc_ref, z_ref, dt_ref, x_ref, ba_ref, gates_ref, h0_ref, cs0_ref, cw_ref, cb_ref,
         a_ref, e3_ref, dx_ref, nw_ref, wso_ref, wout_ref, pnw_ref, eye_ref, tril_ref, qmask_ref,
         y_ref, nssm_ref, nconv_ref, ht, ybuf, xpad, xc) = refs
    else:
        (xc, z_ref, dt_ref, x_ref, ba_ref, gates_ref, h0_ref,
         a_ref, e3_ref, dx_ref, nw_ref, wso_ref, wout_ref, pnw_ref, eye_ref, tril_ref, qmask_ref,
         y_ref, nssm_ref, ht, ybuf) = refs
    blk = pl.program_id(1)
    nblk = pl.num_programs(1)

    @pl.when(blk == 0)
    def _():
        ht[...] = h0_ref[0].T

    if conv_in_kernel:
        pad0 = SUBLANES - (CONV_W - 1)
        xpad[pad0:SUBLANES, :] = cs0_ref[0]
        xpad[SUBLANES:SUBLANES + tc, :] = xbc_ref[...]
        acc = cb_ref[...] + xpad[pad0:pad0 + tc, :] * cw_ref[0:1, :]
        for tap in range(1, CONV_W):
            acc = acc + xpad[pad0 + tap:pad0 + tap + tc, :] * cw_ref[tap:tap + 1, :]
        xc[...] = _silu(acc)
        nconv_ref[0] = xpad[SUBLANES + out_row - (CONV_W - 1):SUBLANES + out_row, :]

    lane = lax.broadcasted_iota(jnp.int32, (SSD_Q, LANES), 1)
    ri = lax.broadcasted_iota(jnp.int32, (SSD_Q, SSD_Q), 0)
    rj = lax.broadcasted_iota(jnp.int32, (SSD_Q, SSD_Q), 1)
    tri = (ri >= rj).astype(BF16)
    a2_row = a_ref[...]
    dx = dx_ref[...]
    nw = nw_ref[...]

    def expand(v):
        return jnp.dot(_split3(v), e3_ref[...], preferred_element_type=F32)

    def chunk(c, carry):
        r0 = pl.multiple_of(c * SSD_Q, SSD_Q)
        rows = pl.ds(r0, SSD_Q)
        xs = xc[rows, 0:SSM_WIDTH].astype(F32)
        bm = xc[rows, SSM_WIDTH:SSM_WIDTH + BC_WIDTH].astype(BF16)
        cm = xc[rows, SSM_WIDTH + BC_WIDTH:CONV_CH].astype(BF16)
        dt = dt_ref[rows, :]
        cum3 = jnp.dot(tri, _split3(dt * a2_row), preferred_element_type=F32)
        cum = jnp.where(lane < PIECE,
                        cum3 + pltpu.roll(cum3, LANES - PIECE, 1) + pltpu.roll(cum3, LANES - 2 * PIECE, 1), 0.0)
        dt_x = expand(dt)
        cum_x = expand(cum)
        ecum_x = jnp.exp2(cum_x)
        cum_last = cum_x[SSD_Q - 1:SSD_Q, :]
        xdt = xs * dt_x
        xdt_b = xdt.astype(BF16)
        xw = (xdt * jnp.exp2(cum_last - cum_x)).astype(BF16)
        cum_j = jnp.sum(cum_x * eye_ref[...], axis=0, keepdims=True)
        decay = jnp.exp2(cum_x - cum_j + tril_ref[...])

        y_parts = []
        for g in range(SSM_GROUPS):
            cg = cm[:, g * SSM_STATE:(g + 1) * SSM_STATE]
            bg = bm[:, g * SSM_STATE:(g + 1) * SSM_STATE]
            cb4 = lax.dot_general(cg, jnp.concatenate([bg] * 4, axis=0), (((1,), (1,)), ((), ())),
                                  preferred_element_type=F32)
            gs = slice(g * GROUP_WIDTH, (g + 1) * GROUP_WIDTH)
            y_state = jnp.dot(cg, ht[:, gs].astype(BF16), preferred_element_type=F32) * ecum_x[:, gs]
            for half in range(GROUP_WIDTH // QUAD):
                qs = slice(g * GROUP_WIDTH + half * QUAD, g * GROUP_WIDTH + (half + 1) * QUAD)
                m4 = (cb4 * decay[:, qs]).astype(BF16)
                x4 = xdt_b[:, qs]
                bd = jnp.concatenate([x4 * qmask_ref[a:a + 1, :] for a in range(4)], axis=0)
                y_parts.append(jnp.dot(m4, bd, preferred_element_type=F32)
                               + y_state[:, half * QUAD:(half + 1) * QUAD])
            ht[:, gs] = (ht[:, gs] * ecum_x[SSD_Q - 1:SSD_Q, gs]
                         + jnp.dot(bg.T, xw[:, gs], preferred_element_type=F32))
        y = jnp.concatenate(y_parts, axis=1) + xs * dx
        yz = y * z_ref[rows, :].astype(F32)
        for g in range(SSM_GROUPS):
            gs = slice(g * GROUP_WIDTH, (g + 1) * GROUP_WIDTH)
            yg = yz[:, gs]
            ms = jnp.mean(yg * yg, axis=-1, keepdims=True)
            ybuf[rows, gs] = (yg * lax.rsqrt(ms + EPS) * nw[:, gs]).astype(BF16)
        return carry

    lax.fori_loop(0, tc // SSD_Q, chunk, 0, unroll=2)
    branch_s = jnp.dot(ybuf[...], wso_ref[...], preferred_element_type=F32)
    merged = (gates_ref[:, 0:D_MODEL].astype(F32) * ba_ref[...]
              + gates_ref[:, D_MODEL:2 * D_MODEL].astype(F32) * branch_s)
    out = jnp.dot(merged.astype(BF16), wout_ref[...], preferred_element_type=F32)
    ms = jnp.mean(out * out, axis=-1, keepdims=True)
    y_ref[...] = x_ref[...] + out * lax.rsqrt(ms + EPS) * pnw_ref[...]

    @pl.when(blk == nblk - 1)
    def _():
        nssm_ref[0] = ht[...].T


def _ssd_masks():
    i = np.arange(SSD_Q)[:, None]
    j = np.arange(SSM_WIDTH)[None, :] % SSD_Q
    eye = jnp.asarray(i == j, F32)
    tril = jnp.asarray(np.where(i >= j, 0.0, -np.inf), F32)
    qmask = jnp.asarray(np.arange(QUAD)[None, :] // SSM_HEAD_DIM == np.arange(4)[:, None], BF16)
    return eye, tril, qmask


def _ssd(xc, zs, dt, x2d, branch_a, gates, ssm0, conv0, wts, *, bsz, seq, tc, l_real, conv_in_kernel):
    nblk = seq // tc
    assert not conv_in_kernel or nblk == 1
    row = lambda b, j: (b * nblk + j, 0)
    per_seq = lambda b, j: (b, 0, 0)
    hp = SSM_HEADS * SSM_HEAD_DIM
    kern = functools.partial(_ssd_kernel, tc=tc, out_row=l_real, conv_in_kernel=conv_in_kernel)
    operands = [xc, zs, dt, x2d, branch_a, gates, ssm0]
    in_specs = [pl.BlockSpec((tc, CONV_CH), row),
                pl.BlockSpec((tc, SSM_WIDTH), row),
                pl.BlockSpec((tc, LANES), row),
                pl.BlockSpec((tc, D_MODEL), row),
                pl.BlockSpec((tc, D_MODEL), row),
                pl.BlockSpec((tc, 2 * D_MODEL), row),
                pl.BlockSpec((1, hp, SSM_STATE), per_seq)]
    out_specs = [pl.BlockSpec((tc, D_MODEL), row), pl.BlockSpec((1, hp, SSM_STATE), per_seq)]
    out_shape = [jax.ShapeDtypeStruct((bsz * seq, D_MODEL), F32),
                 jax.ShapeDtypeStruct((bsz, hp, SSM_STATE), F32)]
    scratch = [pltpu.VMEM((SSM_STATE, SSM_WIDTH), F32), pltpu.VMEM((tc, SSM_WIDTH), BF16)]
    if conv_in_kernel:
        operands += [conv0, wts["conv_w"], wts["conv_b"]]
        in_specs += [pl.BlockSpec((1, CONV_W - 1, CONV_CH), per_seq),
                     _resident((CONV_W, CONV_CH)), _resident((1, CONV_CH))]
        out_specs.append(pl.BlockSpec((1, CONV_W - 1, CONV_CH), per_seq))
        out_shape.append(jax.ShapeDtypeStruct((bsz, CONV_W - 1, CONV_CH), F32))
        scratch += [pltpu.VMEM((SUBLANES + tc, CONV_CH), F32), pltpu.VMEM((tc, CONV_CH), F32)]
    consts = [wts["a_row"], wts["e3"], wts["d_x"], wts["ssm_norm_w"], wts["w_ssm_o"], wts["w_out"],
              wts["post_norm_w"], *_ssd_masks()]
    operands += consts
    in_specs += [_resident(c.shape) for c in consts]
    return pl.pallas_call(
        kern,
        grid=(bsz, nblk),
        in_specs=in_specs,
        out_specs=out_specs,
        out_shape=out_shape,
        scratch_shapes=scratch,
        compiler_params=_compiler_params(("arbitrary", "arbitrary")),
        name="ssd",
    )(*operands)


def _rope_angles(pos):
    half = ROT_DIM // 2
    inv = ROPE_THETA ** (-(jnp.arange(half, dtype=F32) * 2.0 / ROT_DIM))
    ang = pos.astype(F32)[:, None] * inv[None, :]
    return jnp.cos(ang), jnp.sin(ang)


def _rope_tables(pos):
    cos, sin = _rope_angles(pos)
    half = ROT_DIM // 2
    n = pos.shape[0]
    rest = HEAD_DIM - ROT_DIM
    c = jnp.concatenate([cos, cos, jnp.ones((n, rest), F32)], axis=1)
    s1 = jnp.concatenate([-sin, jnp.zeros((n, half + rest), F32)], axis=1)
    s2 = jnp.concatenate([jnp.zeros((n, half), F32), sin, jnp.zeros((n, rest), F32)], axis=1)
    rep = LANES // HEAD_DIM
    return tuple(jnp.tile(t, (1, rep)) for t in (c, s1, s2))


def _layer_prompt(x, pos, wts, *, tm, tb, tc):
    bsz, seq, _ = x.shape
    x2d = x.reshape(bsz * seq, D_MODEL)
    cos, sin = _rope_angles(pos)
    qt, gt, vt, k, v, zs, xc, gates, dt, conv_tail = _inproj_t(
        x2d, wts["pre_norm_w"], wts["w_cat"], wts["dt_bias"], (cos.T, sin.T), _rope_tables(pos),
        wts["conv_w"], wts["conv_b"], seq, tm)
    sink_rows = jnp.repeat(wts["sinks"].reshape(N_KV_HEADS, Q_PER_KV), PAIR, axis=1)
    branch_a = _attention_t(qt, k, vt, gt, _window_bias(), sink_rows, wts["w_attn_o"], bsz=bsz, seq=seq, tb=tb)
    n_keep = min(WINDOW, seq)
    keep = lambda t: t.reshape(bsz, seq, N_KV_HEADS, HEAD_DIM)[:, seq - n_keep:]
    zero_ssm = jnp.zeros((bsz, SSM_HEADS * SSM_HEAD_DIM, SSM_STATE), F32)
    y, new_ssm = _ssd(xc, zs, dt, x2d, branch_a, gates, zero_ssm, None, wts,
                      bsz=bsz, seq=seq, tc=tc, l_real=seq, conv_in_kernel=False)
    return (y.reshape(bsz, seq, D_MODEL), keep(k), keep(v), conv_tail[:, SUBLANES - (CONV_W - 1):],
            new_ssm.reshape(bsz, SSM_HEADS, SSM_HEAD_DIM, SSM_STATE))


def _layer_sample(x, pos, cache_k, cache_v, conv0, ssm0, wts, *, tm):
    bsz, seq, _ = x.shape
    x2d = x.reshape(bsz * seq, D_MODEL)
    q, kv, g, zs, xbc, gates, dt = _inproj(x2d, wts["pre_norm_w"], wts["w_cat"], wts["dt_bias"], tm)
    sink_cols = jnp.broadcast_to(
        jnp.repeat(wts["sinks"].reshape(N_KV_HEADS, Q_PER_KV), seq, axis=1)[:, :, None],
        (N_KV_HEADS, Q_PER_KV * seq, LANES))
    branch_a, new_k, new_v = _attention(
        q, kv, g, cache_k, cache_v, _rope_tables(pos), sink_cols, wts["w_attn_o"],
        bsz=bsz, seq=seq, tb=seq, ch=seq, mask_first=False)
    seq_p = -(-seq // SSD_Q) * SSD_Q
    padrows = lambda t: jnp.pad(t.reshape(bsz, seq, -1), ((0, 0), (0, seq_p - seq), (0, 0))).reshape(
        bsz * seq_p, -1)
    y, new_ssm, new_conv = _ssd(
        padrows(xbc), padrows(zs), padrows(dt), padrows(x2d), padrows(branch_a), padrows(gates), ssm0, conv0, wts,
        bsz=bsz, seq=seq_p, tc=seq_p, l_real=seq, conv_in_kernel=True)
    n_keep = new_k.shape[1]
    return (y.reshape(bsz, seq_p, D_MODEL)[:, :seq],
            new_k.reshape(bsz, n_keep, N_KV_HEADS, HEAD_DIM), new_v.reshape(bsz, n_keep, N_KV_HEADS, HEAD_DIM),
            new_conv, new_ssm.reshape(bsz, SSM_HEADS, SSM_HEAD_DIM, SSM_STATE))


def _prep_weights(pre_norm_w, w_in, conv_w, conv_b, dt_bias, a_log, d_skip, sinks, ssm_norm_w, w_attn_o,
                  w_ssm_o, w_out, post_norm_w):
    g_off = ATT_WIDTH + 2 * KV_WIDTH
    z_off = g_off + ATT_WIDTH
    dt_off = z_off + SSM_WIDTH + CONV_CH
    w_cat = jnp.concatenate(
        [w_in[:, ATT_WIDTH:g_off], w_in[:, z_off:dt_off], w_in[:, dt_off + SSM_HEADS:],
         w_in[:, dt_off:dt_off + SSM_HEADS], jnp.zeros((D_MODEL, LANES - SSM_HEADS), w_in.dtype),
         w_in[:, :ATT_WIDTH], w_in[:, g_off:z_off]], axis=1).astype(BF16)
    pad_heads = lambda v: jnp.pad(v.astype(F32), (0, LANES - SSM_HEADS)).reshape(1, LANES)
    k_idx = np.arange(LANES)[:, None]
    c_idx = np.arange(SSM_WIDTH)[None, :]
    e3 = jnp.asarray((k_idx < 3 * PIECE) & (k_idx % PIECE == c_idx // SSM_HEAD_DIM), BF16)
    return dict(
        pre_norm_w=pre_norm_w.reshape(1, D_MODEL), w_cat=w_cat, dt_bias=pad_heads(dt_bias),
        conv_w=conv_w, conv_b=conv_b.reshape(1, CONV_CH), a_row=pad_heads(-jnp.exp(a_log.astype(F32)) * LOG2_E),
        e3=e3, d_x=jnp.repeat(d_skip.astype(F32), SSM_HEAD_DIM).reshape(1, SSM_WIDTH), sinks=sinks.astype(F32),
        ssm_norm_w=ssm_norm_w.reshape(1, SSM_WIDTH), w_attn_o=w_attn_o.astype(BF16),
        w_ssm_o=w_ssm_o.astype(BF16), w_out=w_out.astype(BF16), post_norm_w=post_norm_w.reshape(1, D_MODEL))


def kernel(x_prompt, x_sample, cache_k, cache_v, state_conv, state_ssm, pre_norm_w, w_in, conv_w, conv_b, dt_bias,
           a_log, d_skip, sinks, ssm_norm_w, w_attn_o, w_ssm_o, w_out, post_norm_w):
    depth = w_in.shape[0]
    lp = x_prompt.shape[1]
    bs, ls, _ = x_sample.shape
    pos_p = jnp.arange(lp, dtype=F32)
    pos_s = PAST_LEN + jnp.arange(ls, dtype=F32)
    hp = SSM_HEADS * SSM_HEAD_DIM
    yp, ys = x_prompt, x_sample
    outs = [[] for _ in range(8)]
    for layer in range(depth):
        wts = _prep_weights(pre_norm_w[layer], w_in[layer], conv_w[layer], conv_b[layer], dt_bias[layer],
                            a_log[layer], d_skip[layer], sinks[layer], ssm_norm_w[layer], w_attn_o[layer],
                            w_ssm_o[layer], w_out[layer], post_norm_w[layer])
        yp, kp, vp, cp, sp = _layer_prompt(yp, pos_p, wts, tm=512, tb=512, tc=512)
        ys, ks, vs, cs, ss = _layer_sample(ys, pos_s,
                                           cache_k[layer].reshape(bs, WINDOW, KV_WIDTH),
                                           cache_v[layer].reshape(bs, WINDOW, KV_WIDTH),
                                           state_conv[layer], state_ssm[layer].reshape(bs, hp, SSM_STATE), wts,
                                           tm=256)
        for lst, val in zip(outs, (kp, vp, cp, sp, ks, vs, cs, ss)):
            lst.append(val)
    return (yp, ys) + tuple(jnp.stack(lst) for lst in outs)
```

```python
import functools

import jax
import jax.numpy as jnp
import numpy as np
from jax import lax
from jax.experimental import pallas as pl
from jax.experimental.pallas import tpu as pltpu

F32 = jnp.float32
BF16 = jnp.bfloat16

D_MODEL = 1024
PAST_LEN = 2048
CHUNK = 64
N_Q_HEADS = 16
N_KV_HEADS = 4
HEAD_DIM = 64
Q_PER_KV = N_Q_HEADS // N_KV_HEADS
ATT_WIDTH = N_Q_HEADS * HEAD_DIM
KV_WIDTH = N_KV_HEADS * HEAD_DIM
WINDOW = 128
ROT_DIM = HEAD_DIM // 4
ROPE_THETA = 500000.0
SSM_WIDTH = 2 * D_MODEL
SSM_HEAD_DIM = 64
SSM_HEADS = SSM_WIDTH // SSM_HEAD_DIM
SSM_GROUPS = 4
SSM_STATE = 128
GROUP_WIDTH = SSM_WIDTH // SSM_GROUPS
BC_WIDTH = SSM_GROUPS * SSM_STATE
CONV_W = 4
CONV_CH = SSM_WIDTH + 2 * BC_WIDTH
EPS = 1e-6
LOG2_E = 1.4426950408889634

LANES = 128
SUBLANES = 8
VMEM_LIMIT = 56 * 1024 * 1024

OFF_KV = 0
OFF_Z = OFF_KV + 2 * KV_WIDTH
OFF_XBC = OFF_Z + SSM_WIDTH
OFF_GATES = OFF_XBC + CONV_CH
OFF_DT = OFF_GATES + 2 * D_MODEL
OFF_Q = OFF_DT + LANES
OFF_G = OFF_Q + ATT_WIDTH
IN_WIDTH_PADDED = OFF_G + ATT_WIDTH
MM_COLS = 256


def _sigmoid(x):
    return 0.5 * jnp.tanh(0.5 * x) + 0.5


def _silu(x):
    half = 0.5 * x
    return half * jnp.tanh(half) + half


def _softplus(x):
    return jnp.maximum(x, 0.0) + jnp.log1p(jnp.exp(-jnp.abs(x)))


def _compiler_params(semantics):
    return pltpu.CompilerParams(dimension_semantics=semantics, vmem_limit_bytes=VMEM_LIMIT)


def _resident(shape):
    zeros = (0,) * len(shape)
    return pl.BlockSpec(shape, lambda *_: zeros, pipeline_mode=pl.Buffered(1))


def _inproj_kernel(x_ref, nw_ref, w_ref, dtb_ref, q_ref, kv_ref, g_ref, z_ref, xbc_ref, gates_ref, dt_ref):
    x = x_ref[...]
    ms = jnp.mean(x * x, axis=-1, keepdims=True)
    h = (x * lax.rsqrt(ms + EPS) * nw_ref[...]).astype(BF16)

    def section(out_ref, off, width, act):
        step = min(MM_COLS, width)
        for n0 in range(0, width, step):
            r = jnp.dot(h, w_ref[:, off + n0:off + n0 + step], preferred_element_type=F32)
            out_ref[:, n0:n0 + step] = act(r)

    ident = lambda r: r
    section(q_ref, OFF_Q, ATT_WIDTH, ident)
    section(kv_ref, OFF_KV, 2 * KV_WIDTH, ident)
    section(g_ref, OFF_G, ATT_WIDTH, _silu)
    section(z_ref, OFF_Z, SSM_WIDTH, _silu)
    section(xbc_ref, OFF_XBC, CONV_CH, ident)
    section(gates_ref, OFF_GATES, 2 * D_MODEL, _sigmoid)
    lane = lax.broadcasted_iota(jnp.int32, (1, LANES), 1)
    section(dt_ref, OFF_DT, LANES,
            lambda r: jnp.where(lane < SSM_HEADS, _softplus(r + dtb_ref[...]), 0.0))


def _inproj(x2d, nw, w_cat, dtb, tm):
    m = x2d.shape[0]
    widths = (ATT_WIDTH, 2 * KV_WIDTH, ATT_WIDTH, SSM_WIDTH, CONV_CH, 2 * D_MODEL, LANES)
    return pl.pallas_call(
        _inproj_kernel,
        grid=(m // tm,),
        in_specs=[pl.BlockSpec((tm, D_MODEL), lambda i: (i, 0)),
                  _resident((1, D_MODEL)),
                  _resident((D_MODEL, IN_WIDTH_PADDED)),
                  _resident((1, LANES))],
        out_specs=[pl.BlockSpec((tm, w), lambda i: (i, 0)) for w in widths],
        out_shape=[jax.ShapeDtypeStruct((m, w), F32) for w in widths],
        compiler_params=_compiler_params(("arbitrary",)),
        name="inproj",
    )(x2d, nw, w_cat, dtb)


def _attn_kernel(q_ref, kv_ref, g_ref, ck_ref, cv_ref, cos_ref, s1_ref, s2_ref, sink_ref, wo_ref,
                 o_ref, nk_ref, nv_ref, qbuf, kbuf, vbuf, att, *, tb, ch, nblk, mask_first, n_keep):
    blk = pl.program_id(1)

    @pl.when(blk == 0)
    def _():
        kbuf[0:WINDOW, :] = ck_ref[0]
        vbuf[0:WINDOW, :] = cv_ref[0]

    cos = cos_ref[...]
    s1 = s1_ref[...]
    s2 = s2_ref[...]

    def rope(x):
        return x * cos + pltpu.roll(x, LANES - ROT_DIM // 2, 1) * s1 + pltpu.roll(x, ROT_DIM // 2, 1) * s2

    scale = HEAD_DIM ** -0.5
    for s in range(ATT_WIDTH // LANES):
        sl = slice(s * LANES, (s + 1) * LANES)
        qbuf[:, sl] = (rope(q_ref[:, sl]) * scale).astype(BF16)
    for s in range(KV_WIDTH // LANES):
        sl = slice(s * LANES, (s + 1) * LANES)
        kbuf[WINDOW:WINDOW + tb, sl] = rope(kv_ref[:, sl])
    vbuf[WINDOW:WINDOW + tb, :] = kv_ref[:, KV_WIDTH:2 * KV_WIDTH]

    nkeys = WINDOW + ch
    for c in range(tb // ch):
        r0 = c * ch
        if mask_first:
            col_chunk = lax.broadcasted_iota(jnp.int32, (1, nkeys), 1) // CHUNK
            valid = (blk * (tb // ch) + c - WINDOW // CHUNK + col_chunk) >= 0
        for h in range(N_KV_HEADS):
            hs = slice(h * HEAD_DIM, (h + 1) * HEAD_DIM)
            qs = jnp.concatenate(
                [qbuf[r0:r0 + ch, (h * Q_PER_KV + g) * HEAD_DIM:(h * Q_PER_KV + g + 1) * HEAD_DIM]
                 for g in range(Q_PER_KV)], axis=0)
            kb = kbuf[r0:r0 + nkeys, hs].astype(BF16)
            vb = vbuf[r0:r0 + nkeys, hs].astype(BF16)
            s = lax.dot_general(qs, kb, (((1,), (1,)), ((), ())), preferred_element_type=F32)
            if mask_first:
                s = jnp.where(valid, s, -jnp.inf)
            sk = sink_ref[h][:, 0:1]
            m = jnp.maximum(jnp.max(s, axis=-1, keepdims=True), sk)
            p = jnp.exp(s - m)
            denom = jnp.sum(p, axis=-1, keepdims=True) + jnp.exp(sk - m)
            o = jnp.dot(p.astype(BF16), vb, preferred_element_type=F32) / denom
            for g in range(Q_PER_KV):
                head = h * Q_PER_KV + g
                att[r0:r0 + ch, head * HEAD_DIM:(head + 1) * HEAD_DIM] = o[g * ch:(g + 1) * ch, :]

    a = (att[...] * g_ref[...]).astype(BF16)
    o_ref[...] = jnp.dot(a, wo_ref[...], preferred_element_type=F32)

    @pl.when(blk == nblk - 1)
    def _():
        nk_ref[0] = kbuf[WINDOW + tb - n_keep:WINDOW + tb, :]
        nv_ref[0] = vbuf[WINDOW + tb - n_keep:WINDOW + tb, :]

    if nblk > 1:
        kbuf[0:WINDOW, :] = kbuf[tb:tb + WINDOW, :]
        vbuf[0:WINDOW, :] = vbuf[tb:tb + WINDOW, :]


def _attention(q, kv, g, cache_k, cache_v, tables, sink_cols, wo, *, bsz, seq, tb, ch, mask_first):
    nblk = seq // tb
    n_keep = min(WINDOW, seq)
    cos, s1, s2 = tables
    row = lambda b, j: (b * nblk + j, 0)
    kern = functools.partial(_attn_kernel, tb=tb, ch=ch, nblk=nblk, mask_first=mask_first, n_keep=n_keep)
    return pl.pallas_call(
        kern,
        grid=(bsz, nblk),
        in_specs=[pl.BlockSpec((tb, ATT_WIDTH), row),
                  pl.BlockSpec((tb, 2 * KV_WIDTH), row),
                  pl.BlockSpec((tb, ATT_WIDTH), row),
                  pl.BlockSpec((1, WINDOW, KV_WIDTH), lambda b, j: (b, 0, 0)),
                  pl.BlockSpec((1, WINDOW, KV_WIDTH), lambda b, j: (b, 0, 0)),
                  pl.BlockSpec((tb, LANES), lambda b, j: (j, 0)),
                  pl.BlockSpec((tb, LANES), lambda b, j: (j, 0)),
                  pl.BlockSpec((tb, LANES), lambda b, j: (j, 0)),
                  _resident((N_KV_HEADS, Q_PER_KV * ch, LANES)),
                  _resident((ATT_WIDTH, D_MODEL))],
        out_specs=[pl.BlockSpec((tb, D_MODEL), row),
                   pl.BlockSpec((1, n_keep, KV_WIDTH), lambda b, j: (b, 0, 0)),
                   pl.BlockSpec((1, n_keep, KV_WIDTH), lambda b, j: (b, 0, 0))],
        out_shape=[jax.ShapeDtypeStruct((bsz * seq, D_MODEL), F32),
                   jax.ShapeDtypeStruct((bsz, n_keep, KV_WIDTH), F32),
                   jax.ShapeDtypeStruct((bsz, n_keep, KV_WIDTH), F32)],
        scratch_shapes=[pltpu.VMEM((tb, ATT_WIDTH), BF16),
                        pltpu.VMEM((WINDOW + tb, KV_WIDTH), F32),
                        pltpu.VMEM((WINDOW + tb, KV_WIDTH), F32),
                        pltpu.VMEM((tb, ATT_WIDTH), F32)],
        compiler_params=_compiler_params(("arbitrary", "arbitrary")),
        name="attention",
    )(q, kv, g, cache_k, cache_v, cos, s1, s2, sink_cols, wo)


PN_K = OFF_KV
PN_V = PN_K + KV_WIDTH
PN_Z = OFF_Z
PN_XBC = OFF_XBC
PN_GATES = OFF_GATES
PN_DT = OFF_DT
PT_Q = 0
PT_G = PT_Q + ATT_WIDTH
PT_V = PT_G + ATT_WIDTH
PT_WIDTH = PT_V + KV_WIDTH
HALF_ROT = ROT_DIM // 2
PAIR = 2 * CHUNK
KEY_WIN = WINDOW + PAIR


def _inproj_t_kernel(x_ref, nw_ref, w_ref, dtb_ref, cosq_ref, sinq_ref, cosk_ref, s1k_ref, s2k_ref,
                     cw_ref, cb_ref,
                     qt_ref, gt_ref, vt_ref, k_ref, v_ref, z_ref, xc_ref, gates_ref, dt_ref, ctail_ref,
                     cprev, wt_ref, *, tm, nper):
    step = pl.program_id(0)
    seq_start = step % nper == 0
    cur = step % 2

    @pl.when(step == 0)
    def _():
        for dst, src, width in ((PT_Q, OFF_Q, ATT_WIDTH), (PT_G, OFF_G, ATT_WIDTH), (PT_V, PN_V, KV_WIDTH)):
            for n0 in range(0, width, MM_COLS):
                wt_ref[dst + n0:dst + n0 + MM_COLS, :] = w_ref[:, src + n0:src + n0 + MM_COLS].T

    x = x_ref[...]
    ms = jnp.mean(x * x, axis=-1, keepdims=True)
    h = (x * lax.rsqrt(ms + EPS) * nw_ref[...]).astype(BF16)

    def mm_t(off, rows):
        return lax.dot_general(wt_ref[off:off + rows, :], h, (((1,), (1,)), ((), ())),
                               preferred_element_type=F32)

    cosq = cosq_ref[...]
    sinq = sinq_ref[...]
    scale = HEAD_DIM ** -0.5 * LOG2_E


    def q_tail(n0, r):
        r = r * scale
        for a in range(MM_COLS // HEAD_DIM):
            base = a * HEAD_DIM
            x1 = r[base:base + HALF_ROT]
            x2 = r[base + HALF_ROT:base + ROT_DIM]
            rot = jnp.concatenate([x1 * cosq - x2 * sinq, x2 * cosq + x1 * sinq], axis=0)
            qt_ref[n0 + base:n0 + base + ROT_DIM, :] = rot.astype(BF16)
            qt_ref[n0 + base + ROT_DIM:n0 + base + HEAD_DIM, :] = r[base + ROT_DIM:base + HEAD_DIM].astype(BF16)

    def g_tail(n0, r):
        gt_ref[n0:n0 + MM_COLS, :] = _silu(r).astype(BF16)

    def vt_tail(r):
        vt_ref[...] = r.astype(BF16)

    def mm_n(off, width):
        return jnp.dot(h, w_ref[:, off:off + width], preferred_element_type=F32)

    def nat_tail(out_ref, n0, width, act, r):
        out_ref[:, n0:n0 + width] = act(r).astype(out_ref.dtype)

    def k_tail(k_all):
        for s in range(KV_WIDTH // LANES):
            kk = k_all[:, s * LANES:(s + 1) * LANES]
            k_ref[:, s * LANES:(s + 1) * LANES] = (
                kk * cosk_ref[...] + pltpu.roll(kk, LANES - HALF_ROT, 1) * s1k_ref[...]
                + pltpu.roll(kk, HALF_ROT, 1) * s2k_ref[...])

    row8 = lax.broadcasted_iota(jnp.int32, (SUBLANES, MM_COLS), 0)

    def conv_tail(n0, r):
        cols = slice(n0, n0 + MM_COLS)
        prev =jnp.where(seq_start, 0.0, cprev[cur, :, cols])
        acc = cb_ref[:, cols] + r * cw_ref[CONV_W - 1:CONV_W, cols]
        for k in range(1, CONV_W):
            rolled = pltpu.roll(r, k, 0)
            top = jnp.where(row8 < k, pltpu.roll(prev, k, 0), rolled[0:SUBLANES])
            shifted = jnp.concatenate([top, rolled[SUBLANES:tm]], axis=0)
            acc = acc + shifted * cw_ref[CONV_W - 1 - k:CONV_W - k, cols]
        xc_ref[:, cols] = _silu(acc).astype(xc_ref.dtype)
        cprev[1 - cur, :, cols] = r[tm - SUBLANES:tm]
        ctail_ref[0, :, cols] = r[tm - SUBLANES:tm]

    lane = lax.broadcasted_iota(jnp.int32, (1, LANES), 1)
    ident = lambda r: r
    dt_act = lambda r: jnp.where(lane < SSM_HEADS, _softplus(r + dtb_ref[...]), 0.0)
    P = functools.partial
    light = ([(P(mm_t, PT_Q + n0, MM_COLS), P(q_tail, n0)) for n0 in range(0, ATT_WIDTH, MM_COLS)]
             + [(P(mm_t, PT_V, KV_WIDTH), vt_tail),
                (P(mm_n, PN_K, KV_WIDTH), k_tail),
                (P(mm_n, PN_V, KV_WIDTH), P(nat_tail, v_ref, 0, KV_WIDTH, ident)),
                (P(mm_n, PN_DT, LANES), P(nat_tail, dt_ref, 0, LANES, dt_act))])
    medium = ([(P(mm_n, PN_Z + n0, MM_COLS), P(nat_tail, z_ref, n0, MM_COLS, _silu))
               for n0 in range(0, SSM_WIDTH, MM_COLS)]
              + [(P(mm_t, PT_G + n0, MM_COLS), P(g_tail, n0)) for n0 in range(0, ATT_WIDTH, MM_COLS)]
              + [(P(mm_n, PN_GATES + n0, MM_COLS), P(nat_tail, gates_ref, n0, MM_COLS, _sigmoid))
                 for n0 in range(0, 2 * D_MODEL, MM_COLS)])
    heavy = [(P(mm_n, PN_XBC + n0, MM_COLS), P(conv_tail, n0)) for n0 in range(0, CONV_CH, MM_COLS)]
    order = [grp[idx] for idx in range(max(len(light), len(medium), len(heavy)))
             for grp in (heavy, light, medium) if idx < len(grp)]
    pending = order[0][0]()
    for idx, (_, tail) in enumerate(order):
        nxt = order[idx + 1][0]() if idx + 1 < len(order) else None
        tail(pending)
        pending = nxt


def _inproj_t(x2d, nw, w_cat, dtb, qtabs, ktabs, cw, cb, seq, tm):
    m = x2d.shape[0]
    nper = seq // tm
    row = lambda i: (i, 0)
    col = lambda i: (0, i)
    nat_widths = (KV_WIDTH, KV_WIDTH, SSM_WIDTH, CONV_CH, 2 * D_MODEL, LANES)
    nat_dtypes = (F32, F32, BF16, BF16, BF16, F32)
    return pl.pallas_call(
        functools.partial(_inproj_t_kernel, tm=tm, nper=nper),
        grid=(m // tm,),
        in_specs=[pl.BlockSpec((tm, D_MODEL), row),
                  _resident((1, D_MODEL)),
                  _resident((D_MODEL, IN_WIDTH_PADDED)),
                  _resident((1, LANES)),
                  pl.BlockSpec((HALF_ROT, tm), lambda i: (0, i % nper)),
                  pl.BlockSpec((HALF_ROT, tm), lambda i: (0, i % nper)),
                  pl.BlockSpec((tm, LANES), lambda i: (i % nper, 0)),
                  pl.BlockSpec((tm, LANES), lambda i: (i % nper, 0)),
                  pl.BlockSpec((tm, LANES), lambda i: (i % nper, 0)),
                  _resident((CONV_W, CONV_CH)),
                  _resident((1, CONV_CH))],
        out_specs=[pl.BlockSpec((ATT_WIDTH, tm), col),
                   pl.BlockSpec((ATT_WIDTH, tm), col),
                   pl.BlockSpec((KV_WIDTH, tm), col)]
                  + [pl.BlockSpec((tm, w), row) for w in nat_widths]
                  + [pl.BlockSpec((1, SUBLANES, CONV_CH), lambda i: (i // nper, 0, 0))],
        out_shape=[jax.ShapeDtypeStruct((ATT_WIDTH, m), BF16),
                   jax.ShapeDtypeStruct((ATT_WIDTH, m), BF16),
                   jax.ShapeDtypeStruct((KV_WIDTH, m), BF16)]
                  + [jax.ShapeDtypeStruct((m, w), d) for w, d in zip(nat_widths, nat_dtypes)]
                  + [jax.ShapeDtypeStruct((m // seq, SUBLANES, CONV_CH), F32)],
        scratch_shapes=[pltpu.VMEM((2, SUBLANES, CONV_CH), F32), pltpu.VMEM((PT_WIDTH, D_MODEL), BF16)],
        compiler_params=_compiler_params(("arbitrary",)),
        name="inproj_t",
    )(x2d, nw, w_cat, dtb, *qtabs, *ktabs, cw, cb)


def _attn_t_kernel(qt_ref, k_ref, vt_ref, gt_ref, bias_ref, sink_ref, wo_ref, o_ref, kbuf, vtbuf, att_t, *, tb):
    blk = pl.program_id(1)
    cur = blk % 2
    prv = 1 - cur

    @pl.when(blk == 0)
    def _():
        kbuf[1, tb - WINDOW:tb, :] = jnp.zeros((WINDOW, KV_WIDTH), BF16)
        vtbuf[1, :, tb - WINDOW:tb] = jnp.zeros((KV_WIDTH, WINDOW), BF16)

    kbuf[cur] = k_ref[...].astype(BF16)
    vtbuf[cur] = vt_ref[...]
    seq_start = jnp.where(blk == 0, -jnp.inf, 0.0)

    batch_dims = (((2,), (1,)), ((0,), (0,)))
    for pr in range(tb // PAIR):
        c0 = pr * PAIR
        k_wins, vt_wins, qs = [], [], []
        for h in range(N_KV_HEADS):
            hs = slice(h * HEAD_DIM, (h + 1) * HEAD_DIM)
            if pr == 0:
                k_wins.append(jnp.concatenate([kbuf[prv, tb - WINDOW:tb, hs], kbuf[cur, 0:PAIR, hs]], axis=0))
                vt_wins.append(jnp.concatenate([vtbuf[prv, hs, tb - WINDOW:tb], vtbuf[cur, hs, 0:PAIR]], axis=1))
            else:
                k_wins.append(kbuf[cur, c0 - WINDOW:c0 + PAIR, hs])
                vt_wins.append(vtbuf[cur, hs, c0 - WINDOW:c0 + PAIR])
            qs.append(jnp.concatenate(
                [qt_ref[(h * Q_PER_KV + g) * HEAD_DIM:(h * Q_PER_KV + g + 1) * HEAD_DIM, c0:c0 + PAIR]
                 for g in range(Q_PER_KV)], axis=1))
        s = lax.dot_general(jnp.stack(k_wins), jnp.stack(qs), batch_dims,
                            preferred_element_type=F32)
        lo = bias_ref[0:CHUNK] + seq_start if pr == 0 else bias_ref[0:CHUNK]
        mid = [s[:, CHUNK:WINDOW] + seq_start, s[:, WINDOW:WINDOW + CHUNK]] if pr == 0 else [s[:, CHUNK:WINDOW + CHUNK]]
        s = jnp.concatenate([s[:, 0:CHUNK] + lo[None]] + mid
                            + [s[:, WINDOW + CHUNK:KEY_WIN] + bias_ref[WINDOW + CHUNK:KEY_WIN][None]], axis=1)
        sk = sink_ref[...][:, None, :]
        m = jnp.maximum(jnp.max(s, axis=1, keepdims=True), sk)
        p = jnp.exp2(s - m)
        denom = jnp.sum(p, axis=1, keepdims=True) + jnp.exp2(sk - m)
        o = lax.dot_general(jnp.stack(vt_wins), p.astype(BF16), batch_dims, preferred_element_type=F32) / denom
        for h in range(N_KV_HEADS):
            for g in range(Q_PER_KV):
                rows = slice((h * Q_PER_KV + g) * HEAD_DIM, (h * Q_PER_KV + g + 1) * HEAD_DIM)
                att_t[rows, c0:c0 + PAIR] = (
                    o[h, :, g * PAIR:(g + 1) * PAIR] * gt_ref[rows, c0:c0 + PAIR].astype(F32)).astype(BF16)

    o_ref[...] = lax.dot_general(att_t[...], wo_ref[...], (((0,), (0,)), ((), ())), preferred_element_type=F32)


def _attention_t(qt, k, vt, gt, bias, sink_rows, wo, *, bsz, seq, tb):
    nblk = seq // tb
    row = lambda b, j: (b * nblk + j, 0)
    col = lambda b, j: (0, b * nblk + j)
    return pl.pallas_call(
        functools.partial(_attn_t_kernel, tb=tb),
        grid=(bsz, nblk),
        in_specs=[pl.BlockSpec((ATT_WIDTH, tb), col),
                  pl.BlockSpec((tb, KV_WIDTH), row),
                  pl.BlockSpec((KV_WIDTH, tb), col),
                  pl.BlockSpec((ATT_WIDTH, tb), col),
                  _resident((KEY_WIN, Q_PER_KV * PAIR)),
                  _resident((N_KV_HEADS, Q_PER_KV * PAIR)),
                  _resident((ATT_WIDTH, D_MODEL))],
        out_specs=pl.BlockSpec((tb, D_MODEL), row),
        out_shape=jax.ShapeDtypeStruct((bsz * seq, D_MODEL), F32),
        scratch_shapes=[pltpu.VMEM((2, tb, KV_WIDTH), BF16),
                        pltpu.VMEM((2, KV_WIDTH, tb), BF16),
                        pltpu.VMEM((ATT_WIDTH, tb), BF16)],
        compiler_params=_compiler_params(("arbitrary", "arbitrary")),
        name="attention_t",
    )(qt, k, vt, gt, bias, sink_rows, wo)


def _window_bias():
    key_chunk = np.arange(KEY_WIN)[:, None] // CHUNK
    q_chunk = (np.arange(Q_PER_KV * PAIR)[None, :] % PAIR) // CHUNK
    ok = (key_chunk >= q_chunk) & (key_chunk <= q_chunk + WINDOW // CHUNK)
    return jnp.asarray(np.where(ok, 0.0, -np.inf), F32)


SSD_Q = 64
PIECE = 32
QUAD = 4 * SSM_HEAD_DIM


def _split3(x):
    hi = x.astype(BF16).astype(F32)
    r1 = x - hi
    mid = r1.astype(BF16).astype(F32)
    lo = (r1 - mid).astype(BF16).astype(F32)
    return (hi + pltpu.roll(mid, PIECE, 1) + pltpu.roll(lo, 2 * PIECE, 1)).astype(BF16)


def _ssd_kernel(*refs, tc, out_row, conv_in_kernel):
    if conv_in_kernel:
        (xbc_ref, z_ref, dt_ref, x_ref, ba_ref, gates_ref, h0_ref, cs0_ref, cw_ref, cb_ref,
         a_ref, e3_ref, dx_ref, nw_ref, wso_ref, wout_ref, pnw_ref, eye_ref, tril_ref, qmask_ref,
         y_ref, nssm_ref, nconv_ref, ht, ybuf, dtx_buf, cumx_buf, xpad, xc) = refs
    else:
        (xc, z_ref, dt_ref, x_ref, ba_ref, gates_ref, h0_ref,
         a_ref, e3_ref, dx_ref, nw_ref, wso_ref, wout_ref, pnw_ref, eye_ref, tril_ref, qmask_ref,
         y_ref, nssm_ref, ht, ybuf, dtx_buf, cumx_buf) = refs
    blk = pl.program_id(1)
    nblk = pl.num_programs(1)

    @pl.when(blk == 0)
    def _():
        ht[...] = h0_ref[0].T

    if conv_in_kernel:
        pad0 = SUBLANES - (CONV_W - 1)
        xpad[pad0:SUBLANES, :] = cs0_ref[0]
        xpad[SUBLANES:SUBLANES + tc, :] = xbc_ref[...]
        acc = cb_ref[...] + xpad[pad0:pad0 + tc, :] * cw_ref[0:1, :]
        for tap in range(1, CONV_W):
            acc = acc + xpad[pad0 + tap:pad0 + tap + tc, :] * cw_ref[tap:tap + 1, :]
        xc[...] = _silu(acc)
        nconv_ref[0] = xpad[SUBLANES + out_row - (CONV_W - 1):SUBLANES + out_row, :]

    n_chunks = tc // SSD_Q
    lane = lax.broadcasted_iota(jnp.int32, (tc, LANES), 1)
    ri = lax.broadcasted_iota(jnp.int32, (n_chunks, SSD_Q, SSD_Q), 1)
    rj = lax.broadcasted_iota(jnp.int32, (n_chunks, SSD_Q, SSD_Q), 2)
    tri = (ri >= rj).astype(BF16)
    dx = dx_ref[...]
    nw = nw_ref[...]

    def expand(v):
        return jnp.dot(_split3(v), e3_ref[...], preferred_element_type=F32)

    dt_all = dt_ref[...]
    da3 = _split3(dt_all * a_ref[...]).reshape(n_chunks, SSD_Q, LANES)
    cum3 = lax.dot_general(tri, da3, (((2,), (1,)), ((0,), (0,))),
                           preferred_element_type=F32).reshape(tc, LANES)
    cum_all = jnp.where(lane < PIECE,
                        cum3 + pltpu.roll(cum3, LANES - PIECE, 1) + pltpu.roll(cum3, LANES - 2 * PIECE, 1), 0.0)
    dtx_buf[...] = expand(dt_all)
    cumx_buf[...] = expand(cum_all)

    def chunk(c, carry):
        r0 = pl.multiple_of(c * SSD_Q, SSD_Q)
        rows = pl.ds(r0, SSD_Q)
        xs = xc[rows, 0:SSM_WIDTH].astype(F32)
        bm = xc[rows, SSM_WIDTH:SSM_WIDTH + BC_WIDTH].astype(BF16)
        cm = xc[rows, SSM_WIDTH + BC_WIDTH:CONV_CH].astype(BF16)
        dt_x = dtx_buf[rows, :]
        cum_x = cumx_buf[rows, :]
        ecum_x = jnp.exp2(cum_x)
        cum_last = cum_x[SSD_Q - 1:SSD_Q, :]
        xdt = xs * dt_x
        xdt_b = xdt.astype(BF16)
        xw = (xdt * jnp.exp2(cum_last - cum_x)).astype(BF16)
        cum_j = jnp.sum(cum_x * eye_ref[...], axis=0, keepdims=True)
        decay = jnp.exp2(cum_x - cum_j + tril_ref[...])

        y_parts = []
        for g in range(SSM_GROUPS):
            cg = cm[:, g * SSM_STATE:(g + 1) * SSM_STATE]
            bg = bm[:, g * SSM_STATE:(g + 1) * SSM_STATE]
            cb4 = lax.dot_general(cg, jnp.concatenate([bg] * 4, axis=0), (((1,), (1,)), ((), ())),
                                  preferred_element_type=F32)
            gs = slice(g * GROUP_WIDTH, (g + 1) * GROUP_WIDTH)
            y_state = jnp.dot(cg, ht[:, gs].astype(BF16), preferred_element_type=F32) * ecum_x[:, gs]
            for half in range(GROUP_WIDTH // QUAD):
                qs = slice(g * GROUP_WIDTH + half * QUAD, g * GROUP_WIDTH + (half + 1) * QUAD)
                m4 = (cb4 * decay[:, qs]).astype(BF16)
                x4 = xdt_b[:, qs]
                bd = jnp.concatenate([x4 * qmask_ref[a:a + 1, :] for a in range(4)], axis=0)
                y_parts.append(jnp.dot(m4, bd, preferred_element_type=F32)
                               + y_state[:, half * QUAD:(half + 1) * QUAD])
            ht[:, gs] = (ht[:, gs] * ecum_x[SSD_Q - 1:SSD_Q, gs]
                         + jnp.dot(bg.T, xw[:, gs], preferred_element_type=F32))
        y = jnp.concatenate(y_parts, axis=1) + xs * dx
        yz = y * z_ref[rows, :].astype(F32)
        for g in range(SSM_GROUPS):
            gs = slice(g * GROUP_WIDTH, (g + 1) * GROUP_WIDTH)
            yg = yz[:, gs]
            ms = jnp.mean(yg * yg, axis=-1, keepdims=True)
            ybuf[rows, gs] = (yg * lax.rsqrt(ms + EPS) * nw[:, gs]).astype(BF16)
        return carry

    lax.fori_loop(0, n_chunks, chunk, 0, unroll=min(4, n_chunks))
    branch_s = jnp.dot(ybuf[...], wso_ref[...], preferred_element_type=F32)
    merged = (gates_ref[:, 0:D_MODEL].astype(F32) * ba_ref[...]
              + gates_ref[:, D_MODEL:2 * D_MODEL].astype(F32) * branch_s)
    out = jnp.dot(merged.astype(BF16), wout_ref[...], preferred_element_type=F32)
    ms = jnp.mean(out * out, axis=-1, keepdims=True)
    y_ref[...] = x_ref[...] + out * lax.rsqrt(ms + EPS) * pnw_ref[...]

    @pl.when(blk == nblk - 1)
    def _():
        nssm_ref[0] = ht[...].T


def _ssd_masks():
    i = np.arange(SSD_Q)[:, None]
    j = np.arange(SSM_WIDTH)[None, :] % SSD_Q
    eye = jnp.asarray(i == j, F32)
    tril = jnp.asarray(np.where(i >= j, 0.0, -np.inf), F32)
    qmask = jnp.asarray(np.arange(QUAD)[None, :] // SSM_HEAD_DIM == np.arange(4)[:, None], BF16)
    return eye, tril, qmask


def _ssd(xc, zs, dt, x2d, branch_a, gates, ssm0, conv0, wts, *, bsz, seq, tc, l_real, conv_in_kernel):
    nblk = seq // tc
    assert not conv_in_kernel or nblk == 1
    row = lambda b, j: (b * nblk + j, 0)
    per_seq = lambda b, j: (b, 0, 0)
    hp = SSM_HEADS * SSM_HEAD_DIM
    kern = functools.partial(_ssd_kernel, tc=tc, out_row=l_real, conv_in_kernel=conv_in_kernel)
    operands = [xc, zs, dt, x2d, branch_a, gates, ssm0]
    in_specs = [pl.BlockSpec((tc, CONV_CH), row),
                pl.BlockSpec((tc, SSM_WIDTH), row),
                pl.BlockSpec((tc, LANES), row),
                pl.BlockSpec((tc, D_MODEL), row),
                pl.BlockSpec((tc, D_MODEL), row),
                pl.BlockSpec((tc, 2 * D_MODEL), row),
                pl.BlockSpec((1, hp, SSM_STATE), per_seq)]
    out_specs = [pl.BlockSpec((tc, D_MODEL), row), pl.BlockSpec((1, hp, SSM_STATE), per_seq)]
    out_shape = [jax.ShapeDtypeStruct((bsz * seq, D_MODEL), F32),
                 jax.ShapeDtypeStruct((bsz, hp, SSM_STATE), F32)]
    scratch = [pltpu.VMEM((SSM_STATE, SSM_WIDTH), F32), pltpu.VMEM((tc, SSM_WIDTH), BF16),
               pltpu.VMEM((tc, SSM_WIDTH), F32), pltpu.VMEM((tc, SSM_WIDTH), F32)]
    if conv_in_kernel:
        operands += [conv0, wts["conv_w"], wts["conv_b"]]
        in_specs += [pl.BlockSpec((1, CONV_W - 1, CONV_CH), per_seq),
                     _resident((CONV_W, CONV_CH)), _resident((1, CONV_CH))]
        out_specs.append(pl.BlockSpec((1, CONV_W - 1, CONV_CH), per_seq))
        out_shape.append(jax.ShapeDtypeStruct((bsz, CONV_W - 1, CONV_CH), F32))
        scratch += [pltpu.VMEM((SUBLANES + tc, CONV_CH), F32), pltpu.VMEM((tc, CONV_CH), F32)]
    consts = [wts["a_row"], wts["e3"], wts["d_x"], wts["ssm_norm_w"], wts["w_ssm_o"], wts["w_out"],
              wts["post_norm_w"], *_ssd_masks()]
    operands += consts
    in_specs += [_resident(c.shape) for c in consts]
    return pl.pallas_call(
        kern,
        grid=(bsz, nblk),
        in_specs=in_specs,
        out_specs=out_specs,
        out_shape=out_shape,
        scratch_shapes=scratch,
        compiler_params=_compiler_params(("arbitrary", "arbitrary")),
        name="ssd",
    )(*operands)


def _rope_angles(pos):
    half = ROT_DIM // 2
    inv = ROPE_THETA ** (-(jnp.arange(half, dtype=F32) * 2.0 / ROT_DIM))
    ang = pos.astype(F32)[:, None] * inv[None, :]
    return jnp.cos(ang), jnp.sin(ang)


def _rope_tables(pos):
    cos, sin = _rope_angles(pos)
    half = ROT_DIM // 2
    n = pos.shape[0]
    rest = HEAD_DIM - ROT_DIM
    c = jnp.concatenate([cos, cos, jnp.ones((n, rest), F32)], axis=1)
    s1 = jnp.concatenate([-sin, jnp.zeros((n, half + rest), F32)], axis=1)
    s2 = jnp.concatenate([jnp.zeros((n, half), F32), sin, jnp.zeros((n, rest), F32)], axis=1)
    rep = LANES // HEAD_DIM
    return tuple(jnp.tile(t, (1, rep)) for t in (c, s1, s2))


def _layer_prompt(x, pos, wts, *, tm, tb, tc):
    bsz, seq, _ = x.shape
    x2d = x.reshape(bsz * seq, D_MODEL)
    cos, sin = _rope_angles(pos)
    qt, gt, vt, k, v, zs, xc, gates, dt, conv_tail = _inproj_t(
        x2d, wts["pre_norm_w"], wts["w_cat"], wts["dt_bias"], (cos.T, sin.T), _rope_tables(pos),
        wts["conv_w"], wts["conv_b"], seq, tm)
    sink_rows = jnp.repeat(wts["sinks"].reshape(N_KV_HEADS, Q_PER_KV), PAIR, axis=1) * LOG2_E
    branch_a = _attention_t(qt, k, vt, gt, _window_bias(), sink_rows, wts["w_attn_o"], bsz=bsz, seq=seq, tb=tb)
    n_keep = min(WINDOW, seq)
    keep = lambda t: t.reshape(bsz, seq, KV_WIDTH)[:, seq - n_keep:].reshape(bsz, n_keep, N_KV_HEADS, HEAD_DIM)
    zero_ssm = jnp.zeros((bsz, SSM_HEADS * SSM_HEAD_DIM, SSM_STATE), F32)
    y, new_ssm = _ssd(xc, zs, dt, x2d, branch_a, gates, zero_ssm, None, wts,
                      bsz=bsz, seq=seq, tc=tc, l_real=seq, conv_in_kernel=False)
    return (y.reshape(bsz, seq, D_MODEL), keep(k), keep(v), conv_tail[:, SUBLANES - (CONV_W - 1):],
            new_ssm.reshape(bsz, SSM_HEADS, SSM_HEAD_DIM, SSM_STATE))


def _layer_sample(x, pos, cache_k, cache_v, conv0, ssm0, wts, *, tm):
    bsz, seq, _ = x.shape
    x2d = x.reshape(bsz * seq, D_MODEL)
    q, kv, g, zs, xbc, gates, dt = _inproj(x2d, wts["pre_norm_w"], wts["w_cat"], wts["dt_bias"], tm)
    sink_cols = jnp.broadcast_to(
        jnp.repeat(wts["sinks"].reshape(N_KV_HEADS, Q_PER_KV), seq, axis=1)[:, :, None],
        (N_KV_HEADS, Q_PER_KV * seq, LANES))
    branch_a, new_k, new_v = _attention(
        q, kv, g, cache_k, cache_v, _rope_tables(pos), sink_cols, wts["w_attn_o"],
        bsz=bsz, seq=seq, tb=seq, ch=seq, mask_first=False)
    seq_p = -(-seq // SSD_Q) * SSD_Q
    padrows = lambda t: jnp.pad(t.reshape(bsz, seq, -1), ((0, 0), (0, seq_p - seq), (0, 0))).reshape(
        bsz * seq_p, -1)
    y, new_ssm, new_conv = _ssd(
        padrows(xbc), padrows(zs), padrows(dt), padrows(x2d), padrows(branch_a), padrows(gates), ssm0, conv0, wts,
        bsz=bsz, seq=seq_p, tc=seq_p, l_real=seq, conv_in_kernel=True)
    n_keep = new_k.shape[1]
    return (y.reshape(bsz, seq_p, D_MODEL)[:, :seq],
            new_k.reshape(bsz, n_keep, N_KV_HEADS, HEAD_DIM), new_v.reshape(bsz, n_keep, N_KV_HEADS, HEAD_DIM),
            new_conv, new_ssm.reshape(bsz, SSM_HEADS, SSM_HEAD_DIM, SSM_STATE))


def _prep_weights(pre_norm_w, w_in, conv_w, conv_b, dt_bias, a_log, d_skip, sinks, ssm_norm_w, w_attn_o,
                  w_ssm_o, w_out, post_norm_w):
    g_off = ATT_WIDTH + 2 * KV_WIDTH
    z_off = g_off + ATT_WIDTH
    dt_off = z_off + SSM_WIDTH + CONV_CH
    w_cat = jnp.concatenate(
        [w_in[:, ATT_WIDTH:g_off], w_in[:, z_off:dt_off], w_in[:, dt_off + SSM_HEADS:],
         w_in[:, dt_off:dt_off + SSM_HEADS], jnp.zeros((D_MODEL, LANES - SSM_HEADS), w_in.dtype),
         w_in[:, :ATT_WIDTH], w_in[:, g_off:z_off]], axis=1).astype(BF16)
    pad_heads = lambda v: jnp.pad(v.astype(F32), (0, LANES - SSM_HEADS)).reshape(1, LANES)
    k_idx = np.arange(LANES)[:, None]
    c_idx = np.arange(SSM_WIDTH)[None, :]
    e3 = jnp.asarray((k_idx < 3 * PIECE) & (k_idx % PIECE == c_idx // SSM_HEAD_DIM), BF16)
    return dict(
        pre_norm_w=pre_norm_w.reshape(1, D_MODEL), w_cat=w_cat, dt_bias=pad_heads(dt_bias),
        conv_w=conv_w, conv_b=conv_b.reshape(1, CONV_CH), a_row=pad_heads(-jnp.exp(a_log.astype(F32)) * LOG2_E),
        e3=e3, d_x=jnp.repeat(d_skip.astype(F32), SSM_HEAD_DIM).reshape(1, SSM_WIDTH), sinks=sinks.astype(F32),
        ssm_norm_w=ssm_norm_w.reshape(1, SSM_WIDTH), w_attn_o=w_attn_o.astype(BF16),
        w_ssm_o=w_ssm_o.astype(BF16), w_out=w_out.astype(BF16), post_norm_w=post_norm_w.reshape(1, D_MODEL))


def kernel(x_prompt, x_sample, cache_k, cache_v, state_conv, state_ssm, pre_norm_w, w_in, conv_w, conv_b, dt_bias,
           a_log, d_skip, sinks, ssm_norm_w, w_attn_o, w_ssm_o, w_out, post_norm_w):
    depth = w_in.shape[0]
    lp = x_prompt.shape[1]
    bs, ls, _ = x_sample.shape
    pos_p = jnp.arange(lp, dtype=F32)
    pos_s = PAST_LEN + jnp.arange(ls, dtype=F32)
    hp = SSM_HEADS * SSM_HEAD_DIM
    yp, ys = x_prompt, x_sample
    outs = [[] for _ in range(8)]
    for layer in range(depth):
        wts = _prep_weights(pre_norm_w[layer], w_in[layer], conv_w[layer], conv_b[layer], dt_bias[layer],
                            a_log[layer], d_skip[layer], sinks[layer], ssm_norm_w[layer], w_attn_o[layer],
                            w_ssm_o[layer], w_out[layer], post_norm_w[layer])
        yp, kp, vp, cp, sp = _layer_prompt(yp, pos_p, wts, tm=512, tb=512, tc=512)
        ys, ks, vs, cs, ss = _layer_sample(ys, pos_s,
                                           cache_k[layer].reshape(bs, WINDOW, KV_WIDTH),
                                           cache_v[layer].reshape(bs, WINDOW, KV_WIDTH),
                                           state_conv[layer], state_ssm[layer].reshape(bs, hp, SSM_STATE), wts,
                                           tm=256)
        for lst, val in zip(outs, (kp, vp, cp, sp, ks, vs, cs, ss)):
            lst.append(val)
    return (yp, ys) + tuple(jnp.stack(lst) for lst in outs)
```

```python
import functools

import jax
import jax.numpy as jnp
import numpy as np
from jax import lax
from jax.experimental import pallas as pl
from jax.experimental.pallas import tpu as pltpu

F32 = jnp.float32
BF16 = jnp.bfloat16

D_MODEL = 1024
PAST_LEN = 2048
CHUNK = 64
N_Q_HEADS = 16
N_KV_HEADS = 4
HEAD_DIM = 64
Q_PER_KV = N_Q_HEADS // N_KV_HEADS
ATT_WIDTH = N_Q_HEADS * HEAD_DIM
KV_WIDTH = N_KV_HEADS * HEAD_DIM
WINDOW = 128
ROT_DIM = HEAD_DIM // 4
ROPE_THETA = 500000.0
SSM_WIDTH = 2 * D_MODEL
SSM_HEAD_DIM = 64
SSM_HEADS = SSM_WIDTH // SSM_HEAD_DIM
SSM_GROUPS = 4
SSM_STATE = 128
GROUP_WIDTH = SSM_WIDTH // SSM_GROUPS
BC_WIDTH = SSM_GROUPS * SSM_STATE
CONV_W = 4
CONV_CH = SSM_WIDTH + 2 * BC_WIDTH
EPS = 1e-6
LOG2_E = 1.4426950408889634

LANES = 128
SUBLANES = 8
VMEM_LIMIT = 56 * 1024 * 1024

OFF_KV = 0
OFF_Z = OFF_KV + 2 * KV_WIDTH
OFF_XBC = OFF_Z + SSM_WIDTH
OFF_GATES = OFF_XBC + CONV_CH
OFF_DT = OFF_GATES + 2 * D_MODEL
OFF_Q = OFF_DT + LANES
OFF_G = OFF_Q + ATT_WIDTH
IN_WIDTH_PADDED = OFF_G + ATT_WIDTH
MM_COLS = 256


def _sigmoid(x):
    return 0.5 * jnp.tanh(0.5 * x) + 0.5


def _silu(x):
    half = 0.5 * x
    return half * jnp.tanh(half) + half


def _softplus(x):
    return jnp.maximum(x, 0.0) + jnp.log1p(jnp.exp(-jnp.abs(x)))


def _compiler_params(semantics):
    return pltpu.CompilerParams(dimension_semantics=semantics, vmem_limit_bytes=VMEM_LIMIT)


def _resident(shape):
    zeros = (0,) * len(shape)
    return pl.BlockSpec(shape, lambda *_: zeros, pipeline_mode=pl.Buffered(1))


def _inproj_kernel(x_ref, nw_ref, w_ref, dtb_ref, q_ref, kv_ref, g_ref, z_ref, xbc_ref, gates_ref, dt_ref):
    x = x_ref[...]
    ms = jnp.mean(x * x, axis=-1, keepdims=True)
    h = (x * lax.rsqrt(ms + EPS) * nw_ref[...]).astype(BF16)

    def section(out_ref, off, width, act):
        step = min(MM_COLS, width)
        for n0 in range(0, width, step):
            r = jnp.dot(h, w_ref[:, off + n0:off + n0 + step], preferred_element_type=F32)
            out_ref[:, n0:n0 + step] = act(r)

    ident = lambda r: r
    section(q_ref, OFF_Q, ATT_WIDTH, ident)
    section(kv_ref, OFF_KV, 2 * KV_WIDTH, ident)
    section(g_ref, OFF_G, ATT_WIDTH, _silu)
    section(z_ref, OFF_Z, SSM_WIDTH, _silu)
    section(xbc_ref, OFF_XBC, CONV_CH, ident)
    section(gates_ref, OFF_GATES, 2 * D_MODEL, _sigmoid)
    lane = lax.broadcasted_iota(jnp.int32, (1, LANES), 1)
    section(dt_ref, OFF_DT, LANES,
            lambda r: jnp.where(lane < SSM_HEADS, _softplus(r + dtb_ref[...]), 0.0))


def _inproj(x2d, nw, w_cat, dtb, tm):
    m = x2d.shape[0]
    widths = (ATT_WIDTH, 2 * KV_WIDTH, ATT_WIDTH, SSM_WIDTH, CONV_CH, 2 * D_MODEL, LANES)
    return pl.pallas_call(
        _inproj_kernel,
        grid=(m // tm,),
        in_specs=[pl.BlockSpec((tm, D_MODEL), lambda i: (i, 0)),
                  _resident((1, D_MODEL)),
                  _resident((D_MODEL, IN_WIDTH_PADDED)),
                  _resident((1, LANES))],
        out_specs=[pl.BlockSpec((tm, w), lambda i: (i, 0)) for w in widths],
        out_shape=[jax.ShapeDtypeStruct((m, w), F32) for w in widths],
        compiler_params=_compiler_params(("arbitrary",)),
        name="inproj",
    )(x2d, nw, w_cat, dtb)


def _attn_kernel(q_ref, kv_ref, g_ref, ck_ref, cv_ref, cos_ref, s1_ref, s2_ref, sink_ref, wo_ref,
                 o_ref, nk_ref, nv_ref, qbuf, kbuf, vbuf, att, *, tb, ch, nblk, mask_first, n_keep):
    blk = pl.program_id(1)

    @pl.when(blk == 0)
    def _():
        kbuf[0:WINDOW, :] = ck_ref[0]
        vbuf[0:WINDOW, :] = cv_ref[0]

    cos = cos_ref[...]
    s1 = s1_ref[...]
    s2 = s2_ref[...]

    def rope(x):
        return x * cos + pltpu.roll(x, LANES - ROT_DIM // 2, 1) * s1 + pltpu.roll(x, ROT_DIM // 2, 1) * s2

    scale = HEAD_DIM ** -0.5
    for s in range(ATT_WIDTH // LANES):
        sl = slice(s * LANES, (s + 1) * LANES)
        qbuf[:, sl] = (rope(q_ref[:, sl]) * scale).astype(BF16)
    for s in range(KV_WIDTH // LANES):
        sl = slice(s * LANES, (s + 1) * LANES)
        kbuf[WINDOW:WINDOW + tb, sl] = rope(kv_ref[:, sl])
    vbuf[WINDOW:WINDOW + tb, :] = kv_ref[:, KV_WIDTH:2 * KV_WIDTH]

    nkeys = WINDOW + ch
    for c in range(tb // ch):
        r0 = c * ch
        if mask_first:
            col_chunk = lax.broadcasted_iota(jnp.int32, (1, nkeys), 1) // CHUNK
            valid = (blk * (tb // ch) + c - WINDOW // CHUNK + col_chunk) >= 0
        for h in range(N_KV_HEADS):
            hs = slice(h * HEAD_DIM, (h + 1) * HEAD_DIM)
            qs = jnp.concatenate(
                [qbuf[r0:r0 + ch, (h * Q_PER_KV + g) * HEAD_DIM:(h * Q_PER_KV + g + 1) * HEAD_DIM]
                 for g in range(Q_PER_KV)], axis=0)
            kb = kbuf[r0:r0 + nkeys, hs].astype(BF16)
            vb = vbuf[r0:r0 + nkeys, hs].astype(BF16)
            s = lax.dot_general(qs, kb, (((1,), (1,)), ((), ())), preferred_element_type=F32)
            if mask_first:
                s = jnp.where(valid, s, -jnp.inf)
            sk = sink_ref[h][:, 0:1]
            m = jnp.maximum(jnp.max(s, axis=-1, keepdims=True), sk)
            p = jnp.exp(s - m)
            denom = jnp.sum(p, axis=-1, keepdims=True) + jnp.exp(sk - m)
            o = jnp.dot(p.astype(BF16), vb, preferred_element_type=F32) / denom
            for g in range(Q_PER_KV):
                head = h * Q_PER_KV + g
                att[r0:r0 + ch, head * HEAD_DIM:(head + 1) * HEAD_DIM] = o[g * ch:(g + 1) * ch, :]

    a = (att[...] * g_ref[...]).astype(BF16)
    o_ref[...] = jnp.dot(a, wo_ref[...], preferred_element_type=F32)

    @pl.when(blk == nblk - 1)
    def _():
        nk_ref[0] = kbuf[WINDOW + tb - n_keep:WINDOW + tb, :]
        nv_ref[0] = vbuf[WINDOW + tb - n_keep:WINDOW + tb, :]

    if nblk > 1:
        kbuf[0:WINDOW, :] = kbuf[tb:tb + WINDOW, :]
        vbuf[0:WINDOW, :] = vbuf[tb:tb + WINDOW, :]


def _attention(q, kv, g, cache_k, cache_v, tables, sink_cols, wo, *, bsz, seq, tb, ch, mask_first):
    nblk = seq // tb
    n_keep = min(WINDOW, seq)
    cos, s1, s2 = tables
    row = lambda b, j: (b * nblk + j, 0)
    kern = functools.partial(_attn_kernel, tb=tb, ch=ch, nblk=nblk, mask_first=mask_first, n_keep=n_keep)
    return pl.pallas_call(
        kern,
        grid=(bsz, nblk),
        in_specs=[pl.BlockSpec((tb, ATT_WIDTH), row),
                  pl.BlockSpec((tb, 2 * KV_WIDTH), row),
                  pl.BlockSpec((tb, ATT_WIDTH), row),
                  pl.BlockSpec((1, WINDOW, KV_WIDTH), lambda b, j: (b, 0, 0)),
                  pl.BlockSpec((1, WINDOW, KV_WIDTH), lambda b, j: (b, 0, 0)),
                  pl.BlockSpec((tb, LANES), lambda b, j: (j, 0)),
                  pl.BlockSpec((tb, LANES), lambda b, j: (j, 0)),
                  pl.BlockSpec((tb, LANES), lambda b, j: (j, 0)),
                  _resident((N_KV_HEADS, Q_PER_KV * ch, LANES)),
                  _resident((ATT_WIDTH, D_MODEL))],
        out_specs=[pl.BlockSpec((tb, D_MODEL), row),
                   pl.BlockSpec((1, n_keep, KV_WIDTH), lambda b, j: (b, 0, 0)),
                   pl.BlockSpec((1, n_keep, KV_WIDTH), lambda b, j: (b, 0, 0))],
        out_shape=[jax.ShapeDtypeStruct((bsz * seq, D_MODEL), F32),
                   jax.ShapeDtypeStruct((bsz, n_keep, KV_WIDTH), F32),
                   jax.ShapeDtypeStruct((bsz, n_keep, KV_WIDTH), F32)],
        scratch_shapes=[pltpu.VMEM((tb, ATT_WIDTH), BF16),
                        pltpu.VMEM((WINDOW + tb, KV_WIDTH), F32),
                        pltpu.VMEM((WINDOW + tb, KV_WIDTH), F32),
                        pltpu.VMEM((tb, ATT_WIDTH), F32)],
        compiler_params=_compiler_params(("arbitrary", "arbitrary")),
        name="attention",
    )(q, kv, g, cache_k, cache_v, cos, s1, s2, sink_cols, wo)


PN_K = OFF_KV
PN_V = PN_K + KV_WIDTH
PN_Z = OFF_Z
PN_XBC = OFF_XBC
PN_GATES = OFF_GATES
PN_DT = OFF_DT
PT_Q = 0
PT_G = PT_Q + ATT_WIDTH
PT_V = PT_G + ATT_WIDTH
PT_WIDTH = PT_V + KV_WIDTH
HALF_ROT = ROT_DIM // 2
PAIR = 2 * CHUNK
KEY_WIN = WINDOW + PAIR


def _inproj_t_kernel(x_ref, nw_ref, w_ref, dtb_ref, cosq_ref, sinq_ref, cosk_ref, s1k_ref, s2k_ref,
                     cw_ref, cb_ref,
                     qt_ref, gt_ref, vt_ref, k_ref, v_ref, z_ref, xc_ref, gates_ref, dt_ref, ctail_ref,
                     cprev, wt_ref, *, tm, nper):
    step = pl.program_id(0)
    seq_start = step % nper == 0
    cur = step % 2

    @pl.when(step == 0)
    def _():
        for dst, src, width in ((PT_Q, OFF_Q, ATT_WIDTH), (PT_G, OFF_G, ATT_WIDTH), (PT_V, PN_V, KV_WIDTH)):
            for n0 in range(0, width, MM_COLS):
                wt_ref[dst + n0:dst + n0 + MM_COLS, :] = w_ref[:, src + n0:src + n0 + MM_COLS].T

    x = x_ref[...]
    ms = jnp.mean(x * x, axis=-1, keepdims=True)
    h = (x * lax.rsqrt(ms + EPS) * nw_ref[...]).astype(BF16)

    def mm_t(off, rows):
        return lax.dot_general(wt_ref[off:off + rows, :], h, (((1,), (1,)), ((), ())),
                               preferred_element_type=F32)

    cosq = cosq_ref[...]
    sinq = sinq_ref[...]
    scale = HEAD_DIM ** -0.5 * LOG2_E


    def q_tail(n0, r):
        r = r * scale
        for a in range(MM_COLS // HEAD_DIM):
            base = a * HEAD_DIM
            x1 = r[base:base + HALF_ROT]
            x2 = r[base + HALF_ROT:base + ROT_DIM]
            rot = jnp.concatenate([x1 * cosq - x2 * sinq, x2 * cosq + x1 * sinq], axis=0)
            qt_ref[n0 + base:n0 + base + ROT_DIM, :] = rot.astype(BF16)
            qt_ref[n0 + base + ROT_DIM:n0 + base + HEAD_DIM, :] = r[base + ROT_DIM:base + HEAD_DIM].astype(BF16)

    def g_tail(n0, r):
        gt_ref[n0:n0 + MM_COLS, :] = _silu(r).astype(BF16)

    def vt_tail(r):
        vt_ref[...] = r.astype(BF16)

    def mm_n(off, width):
        return jnp.dot(h, w_ref[:, off:off + width], preferred_element_type=F32)

    def nat_tail(out_ref, n0, width, act, r):
        out_ref[:, n0:n0 + width] = act(r).astype(out_ref.dtype)

    def k_tail(k_all):
        for s in range(KV_WIDTH // LANES):
            kk = k_all[:, s * LANES:(s + 1) * LANES]
            k_ref[:, s * LANES:(s + 1) * LANES] = (
                kk * cosk_ref[...] + pltpu.roll(kk, LANES - HALF_ROT, 1) * s1k_ref[...]
                + pltpu.roll(kk, HALF_ROT, 1) * s2k_ref[...])

    row8 = lax.broadcasted_iota(jnp.int32, (SUBLANES, MM_COLS), 0)

    def conv_tail(n0, r):
        cols = slice(n0, n0 + MM_COLS)
        prev =jnp.where(seq_start, 0.0, cprev[cur, :, cols])
        acc = cb_ref[:, cols] + r * cw_ref[CONV_W - 1:CONV_W, cols]
        for k in range(1, CONV_W):
            rolled = pltpu.roll(r, k, 0)
            top = jnp.where(row8 < k, pltpu.roll(prev, k, 0), rolled[0:SUBLANES])
            shifted = jnp.concatenate([top, rolled[SUBLANES:tm]], axis=0)
            acc = acc + shifted * cw_ref[CONV_W - 1 - k:CONV_W - k, cols]
        xc_ref[:, cols] = _silu(acc).astype(xc_ref.dtype)
        cprev[1 - cur, :, cols] = r[tm - SUBLANES:tm]
        ctail_ref[0, :, cols] = r[tm - SUBLANES:tm]

    lane = lax.broadcasted_iota(jnp.int32, (1, LANES), 1)
    ident = lambda r: r
    dt_act = lambda r: jnp.where(lane < SSM_HEADS, _softplus(r + dtb_ref[...]), 0.0)
    P = functools.partial
    light = ([(P(mm_t, PT_Q + n0, MM_COLS), P(q_tail, n0)) for n0 in range(0, ATT_WIDTH, MM_COLS)]
             + [(P(mm_t, PT_V, KV_WIDTH), vt_tail),
                (P(mm_n, PN_K, KV_WIDTH), k_tail),
                (P(mm_n, PN_V, KV_WIDTH), P(nat_tail, v_ref, 0, KV_WIDTH, ident)),
                (P(mm_n, PN_DT, LANES), P(nat_tail, dt_ref, 0, LANES, dt_act))])
    medium = ([(P(mm_n, PN_Z + n0, MM_COLS), P(nat_tail, z_ref, n0, MM_COLS, _silu))
               for n0 in range(0, SSM_WIDTH, MM_COLS)]
              + [(P(mm_t, PT_G + n0, MM_COLS), P(g_tail, n0)) for n0 in range(0, ATT_WIDTH, MM_COLS)]
              + [(P(mm_n, PN_GATES + n0, MM_COLS), P(nat_tail, gates_ref, n0, MM_COLS, _sigmoid))
                 for n0 in range(0, 2 * D_MODEL, MM_COLS)])
    heavy = [(P(mm_n, PN_XBC + n0, MM_COLS), P(conv_tail, n0)) for n0 in range(0, CONV_CH, MM_COLS)]
    order = [grp[idx] for idx in range(max(len(light), len(medium), len(heavy)))
             for grp in (heavy, light, medium) if idx < len(grp)]
    pending = order[0][0]()
    for idx, (_, tail) in enumerate(order):
        nxt = order[idx + 1][0]() if idx + 1 < len(order) else None
        tail(pending)
        pending = nxt


def _inproj_t(x2d, nw, w_cat, dtb, qtabs, ktabs, cw, cb, seq, tm):
    m = x2d.shape[0]
    nper = seq // tm
    row = lambda i: (i, 0)
    col = lambda i: (0, i)
    nat_widths = (KV_WIDTH, KV_WIDTH, SSM_WIDTH, CONV_CH, 2 * D_MODEL, LANES)
    nat_dtypes = (F32, F32, BF16, BF16, BF16, F32)
    return pl.pallas_call(
        functools.partial(_inproj_t_kernel, tm=tm, nper=nper),
        grid=(m // tm,),
        in_specs=[pl.BlockSpec((tm, D_MODEL), row),
                  _resident((1, D_MODEL)),
                  _resident((D_MODEL, IN_WIDTH_PADDED)),
                  _resident((1, LANES)),
                  pl.BlockSpec((HALF_ROT, tm), lambda i: (0, i % nper)),
                  pl.BlockSpec((HALF_ROT, tm), lambda i: (0, i % nper)),
                  pl.BlockSpec((tm, LANES), lambda i: (i % nper, 0)),
                  pl.BlockSpec((tm, LANES), lambda i: (i % nper, 0)),
                  pl.BlockSpec((tm, LANES), lambda i: (i % nper, 0)),
                  _resident((CONV_W, CONV_CH)),
                  _resident((1, CONV_CH))],
        out_specs=[pl.BlockSpec((ATT_WIDTH, tm), col),
                   pl.BlockSpec((ATT_WIDTH, tm), col),
                   pl.BlockSpec((KV_WIDTH, tm), col)]
                  + [pl.BlockSpec((tm, w), row) for w in nat_widths]
                  + [pl.BlockSpec((1, SUBLANES, CONV_CH), lambda i: (i // nper, 0, 0))],
        out_shape=[jax.ShapeDtypeStruct((ATT_WIDTH, m), BF16),
                   jax.ShapeDtypeStruct((ATT_WIDTH, m), BF16),
                   jax.ShapeDtypeStruct((KV_WIDTH, m), BF16)]
                  + [jax.ShapeDtypeStruct((m, w), d) for w, d in zip(nat_widths, nat_dtypes)]
                  + [jax.ShapeDtypeStruct((m // seq, SUBLANES, CONV_CH), F32)],
        scratch_shapes=[pltpu.VMEM((2, SUBLANES, CONV_CH), F32), pltpu.VMEM((PT_WIDTH, D_MODEL), BF16)],
        compiler_params=_compiler_params(("arbitrary",)),
        name="inproj_t",
    )(x2d, nw, w_cat, dtb, *qtabs, *ktabs, cw, cb)


def _attn_t_kernel(qt_ref, k_ref, vt_ref, gt_ref, bias_ref, sink_ref, wo_ref, o_ref, kbuf, vtbuf, att_t, *, tb):
    blk = pl.program_id(1)
    cur = blk % 2
    prv = 1 - cur

    @pl.when(blk == 0)
    def _():
        kbuf[1, tb - WINDOW:tb, :] = jnp.zeros((WINDOW, KV_WIDTH), BF16)
        vtbuf[1, :, tb - WINDOW:tb] = jnp.zeros((KV_WIDTH, WINDOW), BF16)

    kbuf[cur] = k_ref[...].astype(BF16)
    vtbuf[cur] = vt_ref[...]
    seq_start = jnp.where(blk == 0, -jnp.inf, 0.0)

    batch_dims = (((2,), (1,)), ((0,), (0,)))
    for pr in range(tb // PAIR):
        c0 = pr * PAIR
        k_wins, vt_wins, qs = [], [], []
        for h in range(N_KV_HEADS):
            hs = slice(h * HEAD_DIM, (h + 1) * HEAD_DIM)
            if pr == 0:
                k_wins.append(jnp.concatenate([kbuf[prv, tb - WINDOW:tb, hs], kbuf[cur, 0:PAIR, hs]], axis=0))
                vt_wins.append(jnp.concatenate([vtbuf[prv, hs, tb - WINDOW:tb], vtbuf[cur, hs, 0:PAIR]], axis=1))
            else:
                k_wins.append(kbuf[cur, c0 - WINDOW:c0 + PAIR, hs])
                vt_wins.append(vtbuf[cur, hs, c0 - WINDOW:c0 + PAIR])
            qs.append(jnp.concatenate(
                [qt_ref[(h * Q_PER_KV + g) * HEAD_DIM:(h * Q_PER_KV + g + 1) * HEAD_DIM, c0:c0 + PAIR]
                 for g in range(Q_PER_KV)], axis=1))
        s = lax.dot_general(jnp.stack(k_wins), jnp.stack(qs), batch_dims,
                            preferred_element_type=F32)
        lo = bias_ref[0:CHUNK] + seq_start if pr == 0 else bias_ref[0:CHUNK]
        mid = [s[:, CHUNK:WINDOW] + seq_start, s[:, WINDOW:WINDOW + CHUNK]] if pr == 0 else [s[:, CHUNK:WINDOW + CHUNK]]
        s = jnp.concatenate([s[:, 0:CHUNK] + lo[None]] + mid
                            + [s[:, WINDOW + CHUNK:KEY_WIN] + bias_ref[WINDOW + CHUNK:KEY_WIN][None]], axis=1)
        sk = sink_ref[...][:, None, :]
        m = jnp.maximum(jnp.max(s, axis=1, keepdims=True), sk)
        p = jnp.exp2(s - m)
        denom = jnp.sum(p, axis=1, keepdims=True) + jnp.exp2(sk - m)
        o = lax.dot_general(jnp.stack(vt_wins), p.astype(BF16), batch_dims, preferred_element_type=F32) / denom
        for h in range(N_KV_HEADS):
            for g in range(Q_PER_KV):
                rows = slice((h * Q_PER_KV + g) * HEAD_DIM, (h * Q_PER_KV + g + 1) * HEAD_DIM)
                att_t[rows, c0:c0 + PAIR] = (
                    o[h, :, g * PAIR:(g + 1) * PAIR] * gt_ref[rows, c0:c0 + PAIR].astype(F32)).astype(BF16)

    o_ref[...] = lax.dot_general(att_t[...], wo_ref[...], (((0,), (0,)), ((), ())), preferred_element_type=F32)


def _attention_t(qt, k, vt, gt, bias, sink_rows, wo, *, bsz, seq, tb):
    nblk = seq // tb
    row = lambda b, j: (b * nblk + j, 0)
    col = lambda b, j: (0, b * nblk + j)
    return pl.pallas_call(
        functools.partial(_attn_t_kernel, tb=tb),
        grid=(bsz, nblk),
        in_specs=[pl.BlockSpec((ATT_WIDTH, tb), col),
                  pl.BlockSpec((tb, KV_WIDTH), row),
                  pl.BlockSpec((KV_WIDTH, tb), col),
                  pl.BlockSpec((ATT_WIDTH, tb), col),
                  _resident((KEY_WIN, Q_PER_KV * PAIR)),
                  _resident((N_KV_HEADS, Q_PER_KV * PAIR)),
                  _resident((ATT_WIDTH, D_MODEL))],
        out_specs=pl.BlockSpec((tb, D_MODEL), row),
        out_shape=jax.ShapeDtypeStruct((bsz * seq, D_MODEL), F32),
        scratch_shapes=[pltpu.VMEM((2, tb, KV_WIDTH), BF16),
                        pltpu.VMEM((2, KV_WIDTH, tb), BF16),
                        pltpu.VMEM((ATT_WIDTH, tb), BF16)],
        compiler_params=_compiler_params(("arbitrary", "arbitrary")),
        name="attention_t",
    )(qt, k, vt, gt, bias, sink_rows, wo)


def _window_bias():
    key_chunk = np.arange(KEY_WIN)[:, None] // CHUNK
    q_chunk = (np.arange(Q_PER_KV * PAIR)[None, :] % PAIR) // CHUNK
    ok = (key_chunk >= q_chunk) & (key_chunk <= q_chunk + WINDOW // CHUNK)
    return jnp.asarray(np.where(ok, 0.0, -np.inf), F32)


SSD_Q = 64
PIECE = 32
QUAD = 4 * SSM_HEAD_DIM


def _split3(x):
    hi = x.astype(BF16).astype(F32)
    r1 = x - hi
    mid = r1.astype(BF16).astype(F32)
    lo = (r1 - mid).astype(BF16).astype(F32)
    return (hi + pltpu.roll(mid, PIECE, 1) + pltpu.roll(lo, 2 * PIECE, 1)).astype(BF16)


def _ssd_kernel(*refs, tc, out_row, conv_in_kernel):
    if conv_in_kernel:
        (xbc_ref, z_ref, dt_ref, x_ref, ba_ref, gates_ref, h0_ref, cs0_ref, cw_ref, cb_ref,
         a_ref, e3_ref, dx_ref, nw_ref, wso_ref, wout_ref, pnw_ref, eye_ref, tril_ref, qmask_ref,
         y_ref, nssm_ref, nconv_ref, ht, ybuf, dtx_buf, cumx_buf, cb_buf, xpad, xc) = refs
    else:
        (xc, z_ref, dt_ref, x_ref, ba_ref, gates_ref, h0_ref,
         a_ref, e3_ref, dx_ref, nw_ref, wso_ref, wout_ref, pnw_ref, eye_ref, tril_ref, qmask_ref,
         y_ref, nssm_ref, ht, ybuf, dtx_buf, cumx_buf, cb_buf) = refs
    blk = pl.program_id(1)
    nblk = pl.num_programs(1)

    @pl.when(blk == 0)
    def _():
        ht[...] = h0_ref[0].T

    if conv_in_kernel:
        pad0 = SUBLANES - (CONV_W - 1)
        xpad[pad0:SUBLANES, :] = cs0_ref[0]
        xpad[SUBLANES:SUBLANES + tc, :] = xbc_ref[...]
        acc = cb_ref[...] + xpad[pad0:pad0 + tc, :] * cw_ref[0:1, :]
        for tap in range(1, CONV_W):
            acc = acc + xpad[pad0 + tap:pad0 + tap + tc, :] * cw_ref[tap:tap + 1, :]
        xc[...] = _silu(acc)
        nconv_ref[0] = xpad[SUBLANES + out_row - (CONV_W - 1):SUBLANES + out_row, :]

    n_chunks = tc // SSD_Q
    lane = lax.broadcasted_iota(jnp.int32, (tc, LANES), 1)
    ri = lax.broadcasted_iota(jnp.int32, (n_chunks, SSD_Q, SSD_Q), 1)
    rj = lax.broadcasted_iota(jnp.int32, (n_chunks, SSD_Q, SSD_Q), 2)
    tri = (ri >= rj).astype(BF16)
    dx = dx_ref[...]
    nw = nw_ref[...]

    def expand(v):
        return jnp.dot(_split3(v), e3_ref[...], preferred_element_type=F32)

    dt_all = dt_ref[...]
    da3 = _split3(dt_all * a_ref[...]).reshape(n_chunks, SSD_Q, LANES)
    cum3 = lax.dot_general(tri, da3, (((2,), (1,)), ((0,), (0,))),
                           preferred_element_type=F32).reshape(tc, LANES)
    cum_all = jnp.where(lane < PIECE,
                        cum3 + pltpu.roll(cum3, LANES - PIECE, 1) + pltpu.roll(cum3, LANES - 2 * PIECE, 1), 0.0)
    dtx_buf[...] = expand(dt_all)
    cumx_buf[...] = expand(cum_all)
    c_parts, b_parts = [], []
    for c in range(n_chunks):
        rws = slice(c * SSD_Q, (c + 1) * SSD_Q)
        for g in range(SSM_GROUPS):
            bg = xc[rws, SSM_WIDTH + g * SSM_STATE:SSM_WIDTH + (g + 1) * SSM_STATE].astype(BF16)
            c_parts.append(xc[rws, SSM_WIDTH + BC_WIDTH + g * SSM_STATE:
                              SSM_WIDTH + BC_WIDTH + (g + 1) * SSM_STATE].astype(BF16))
            b_parts.append(jnp.concatenate([bg] * 4, axis=0))
    cb_buf[...] = lax.dot_general(jnp.stack(c_parts), jnp.stack(b_parts), (((2,), (2,)), ((0,), (0,))),
                                  preferred_element_type=F32)

    def chunk(c, carry):
        r0 = pl.multiple_of(c * SSD_Q, SSD_Q)
        rows = pl.ds(r0, SSD_Q)
        xs = xc[rows, 0:SSM_WIDTH].astype(F32)
        bm = xc[rows, SSM_WIDTH:SSM_WIDTH + BC_WIDTH].astype(BF16)
        cm = xc[rows, SSM_WIDTH + BC_WIDTH:CONV_CH].astype(BF16)
        dt_x = dtx_buf[rows, :]
        cum_x = cumx_buf[rows, :]
        ecum_x = jnp.exp2(cum_x)
        cum_last = cum_x[SSD_Q - 1:SSD_Q, :]
        xdt = xs * dt_x
        xdt_b = xdt.astype(BF16)
        xw = (xdt * jnp.exp2(cum_last - cum_x)).astype(BF16)
        cum_j = jnp.sum(cum_x * eye_ref[...], axis=0, keepdims=True)
        decay = jnp.exp2(cum_x - cum_j + tril_ref[...])

        y_parts = []
        for g in range(SSM_GROUPS):
            cg = cm[:, g * SSM_STATE:(g + 1) * SSM_STATE]
            bg = bm[:, g * SSM_STATE:(g + 1) * SSM_STATE]
            cb4 = cb_buf[c * SSM_GROUPS + g]
            gs = slice(g * GROUP_WIDTH, (g + 1) * GROUP_WIDTH)
            y_state = jnp.dot(cg, ht[:, gs].astype(BF16), preferred_element_type=F32) * ecum_x[:, gs]
            for half in range(GROUP_WIDTH // QUAD):
                qs = slice(g * GROUP_WIDTH + half * QUAD, g * GROUP_WIDTH + (half + 1) * QUAD)
                m4 = (cb4 * decay[:, qs]).astype(BF16)
                x4 = xdt_b[:, qs]
                bd = jnp.concatenate([x4 * qmask_ref[a:a + 1, :] for a in range(4)], axis=0)
                y_parts.append(jnp.dot(m4, bd, preferred_element_type=F32)
                               + y_state[:, half * QUAD:(half + 1) * QUAD])
            ht[:, gs] = (ht[:, gs] * ecum_x[SSD_Q - 1:SSD_Q, gs]
                         + jnp.dot(bg.T, xw[:, gs], preferred_element_type=F32))
        y = jnp.concatenate(y_parts, axis=1) + xs * dx
        yz = y * z_ref[rows, :].astype(F32)
        for g in range(SSM_GROUPS):
            gs = slice(g * GROUP_WIDTH, (g + 1) * GROUP_WIDTH)
            yg = yz[:, gs]
            ms = jnp.mean(yg * yg, axis=-1, keepdims=True)
            ybuf[rows, gs] = (yg * lax.rsqrt(ms + EPS) * nw[:, gs]).astype(BF16)
        return carry

    lax.fori_loop(0, n_chunks, chunk, 0, unroll=min(4, n_chunks))
    branch_s = jnp.dot(ybuf[...], wso_ref[...], preferred_element_type=F32)
    merged = (gates_ref[:, 0:D_MODEL].astype(F32) * ba_ref[...]
              + gates_ref[:, D_MODEL:2 * D_MODEL].astype(F32) * branch_s)
    out = jnp.dot(merged.astype(BF16), wout_ref[...], preferred_element_type=F32)
    ms = jnp.mean(out * out, axis=-1, keepdims=True)
    y_ref[...] = x_ref[...] + out * lax.rsqrt(ms + EPS) * pnw_ref[...]

    @pl.when(blk == nblk - 1)
    def _():
        nssm_ref[0] = ht[...].T


def _ssd_masks():
    i = np.arange(SSD_Q)[:, None]
    j = np.arange(SSM_WIDTH)[None, :] % SSD_Q
    eye = jnp.asarray(i == j, F32)
    tril = jnp.asarray(np.where(i >= j, 0.0, -np.inf), F32)
    qmask = jnp.asarray(np.arange(QUAD)[None, :] // SSM_HEAD_DIM == np.arange(4)[:, None], BF16)
    return eye, tril, qmask


def _ssd(xc, zs, dt, x2d, branch_a, gates, ssm0, conv0, wts, *, bsz, seq, tc, l_real, conv_in_kernel):
    nblk = seq // tc
    assert not conv_in_kernel or nblk == 1
    row = lambda b, j: (b * nblk + j, 0)
    per_seq = lambda b, j: (b, 0, 0)
    hp = SSM_HEADS * SSM_HEAD_DIM
    kern = functools.partial(_ssd_kernel, tc=tc, out_row=l_real, conv_in_kernel=conv_in_kernel)
    operands = [xc, zs, dt, x2d, branch_a, gates, ssm0]
    in_specs = [pl.BlockSpec((tc, CONV_CH), row),
                pl.BlockSpec((tc, SSM_WIDTH), row),
                pl.BlockSpec((tc, LANES), row),
                pl.BlockSpec((tc, D_MODEL), row),
                pl.BlockSpec((tc, D_MODEL), row),
                pl.BlockSpec((tc, 2 * D_MODEL), row),
                pl.BlockSpec((1, hp, SSM_STATE), per_seq)]
    out_specs = [pl.BlockSpec((tc, D_MODEL), row), pl.BlockSpec((1, hp, SSM_STATE), per_seq)]
    out_shape = [jax.ShapeDtypeStruct((bsz * seq, D_MODEL), F32),
                 jax.ShapeDtypeStruct((bsz, hp, SSM_STATE), F32)]
    scratch = [pltpu.VMEM((SSM_STATE, SSM_WIDTH), F32), pltpu.VMEM((tc, SSM_WIDTH), BF16),
               pltpu.VMEM((tc, SSM_WIDTH), F32), pltpu.VMEM((tc, SSM_WIDTH), F32),
               pltpu.VMEM((tc // SSD_Q * SSM_GROUPS, SSD_Q, QUAD), F32)]
    if conv_in_kernel:
        operands += [conv0, wts["conv_w"], wts["conv_b"]]
        in_specs += [pl.BlockSpec((1, CONV_W - 1, CONV_CH), per_seq),
                     _resident((CONV_W, CONV_CH)), _resident((1, CONV_CH))]
        out_specs.append(pl.BlockSpec((1, CONV_W - 1, CONV_CH), per_seq))
        out_shape.append(jax.ShapeDtypeStruct((bsz, CONV_W - 1, CONV_CH), F32))
        scratch += [pltpu.VMEM((SUBLANES + tc, CONV_CH), F32), pltpu.VMEM((tc, CONV_CH), F32)]
    consts = [wts["a_row"], wts["e3"], wts["d_x"], wts["ssm_norm_w"], wts["w_ssm_o"], wts["w_out"],
              wts["post_norm_w"], *_ssd_masks()]
    operands += consts
    in_specs += [_resident(c.shape) for c in consts]
    return pl.pallas_call(
        kern,
        grid=(bsz, nblk),
        in_specs=in_specs,
        out_specs=out_specs,
        out_shape=out_shape,
        scratch_shapes=scratch,
        compiler_params=_compiler_params(("arbitrary", "arbitrary")),
        name="ssd",
    )(*operands)


def _rope_angles(pos):
    half = ROT_DIM // 2
    inv = ROPE_THETA ** (-(jnp.arange(half, dtype=F32) * 2.0 / ROT_DIM))
    ang = pos.astype(F32)[:, None] * inv[None, :]
    return jnp.cos(ang), jnp.sin(ang)


def _rope_tables(pos):
    cos, sin = _rope_angles(pos)
    half = ROT_DIM // 2
    n = pos.shape[0]
    rest = HEAD_DIM - ROT_DIM
    c = jnp.concatenate([cos, cos, jnp.ones((n, rest), F32)], axis=1)
    s1 = jnp.concatenate([-sin, jnp.zeros((n, half + rest), F32)], axis=1)
    s2 = jnp.concatenate([jnp.zeros((n, half), F32), sin, jnp.zeros((n, rest), F32)], axis=1)
    rep = LANES // HEAD_DIM
    return tuple(jnp.tile(t, (1, rep)) for t in (c, s1, s2))


def _layer_prompt(x, pos, wts, *, tm, tb, tc):
    bsz, seq, _ = x.shape
    x2d = x.reshape(bsz * seq, D_MODEL)
    cos, sin = _rope_angles(pos)
    qt, gt, vt, k, v, zs, xc, gates, dt, conv_tail = _inproj_t(
        x2d, wts["pre_norm_w"], wts["w_cat"], wts["dt_bias"], (cos.T, sin.T), _rope_tables(pos),
        wts["conv_w"], wts["conv_b"], seq, tm)
    sink_rows = jnp.repeat(wts["sinks"].reshape(N_KV_HEADS, Q_PER_KV), PAIR, axis=1) * LOG2_E
    branch_a = _attention_t(qt, k, vt, gt, _window_bias(), sink_rows, wts["w_attn_o"], bsz=bsz, seq=seq, tb=tb)
    n_keep = min(WINDOW, seq)
    keep = lambda t: t.reshape(bsz, seq, KV_WIDTH)[:, seq - n_keep:].reshape(bsz, n_keep, N_KV_HEADS, HEAD_DIM)
    zero_ssm = jnp.zeros((bsz, SSM_HEADS * SSM_HEAD_DIM, SSM_STATE), F32)
    y, new_ssm = _ssd(xc, zs, dt, x2d, branch_a, gates, zero_ssm, None, wts,
                      bsz=bsz, seq=seq, tc=tc, l_real=seq, conv_in_kernel=False)
    return (y.reshape(bsz, seq, D_MODEL), keep(k), keep(v), conv_tail[:, SUBLANES - (CONV_W - 1):],
            new_ssm.reshape(bsz, SSM_HEADS, SSM_HEAD_DIM, SSM_STATE))


def _layer_sample(x, pos, cache_k, cache_v, conv0, ssm0, wts, *, tm):
    bsz, seq, _ = x.shape
    x2d = x.reshape(bsz * seq, D_MODEL)
    q, kv, g, zs, xbc, gates, dt = _inproj(x2d, wts["pre_norm_w"], wts["w_cat"], wts["dt_bias"], tm)
    sink_cols = jnp.broadcast_to(
        jnp.repeat(wts["sinks"].reshape(N_KV_HEADS, Q_PER_KV), seq, axis=1)[:, :, None],
        (N_KV_HEADS, Q_PER_KV * seq, LANES))
    branch_a, new_k, new_v = _attention(
        q, kv, g, cache_k, cache_v, _rope_tables(pos), sink_cols, wts["w_attn_o"],
        bsz=bsz, seq=seq, tb=seq, ch=seq, mask_first=False)
    seq_p = -(-seq // SSD_Q) * SSD_Q
    padrows = lambda t: jnp.pad(t.reshape(bsz, seq, -1), ((0, 0), (0, seq_p - seq), (0, 0))).reshape(
        bsz * seq_p, -1)
    y, new_ssm, new_conv = _ssd(
        padrows(xbc), padrows(zs), padrows(dt), padrows(x2d), padrows(branch_a), padrows(gates), ssm0, conv0, wts,
        bsz=bsz, seq=seq_p, tc=seq_p, l_real=seq, conv_in_kernel=True)
    n_keep = new_k.shape[1]
    return (y.reshape(bsz, seq_p, D_MODEL)[:, :seq],
            new_k.reshape(bsz, n_keep, N_KV_HEADS, HEAD_DIM), new_v.reshape(bsz, n_keep, N_KV_HEADS, HEAD_DIM),
            new_conv, new_ssm.reshape(bsz, SSM_HEADS, SSM_HEAD_DIM, SSM_STATE))


def _prep_weights(pre_norm_w, w_in, conv_w, conv_b, dt_bias, a_log, d_skip, sinks, ssm_norm_w, w_attn_o,
                  w_ssm_o, w_out, post_norm_w):
    g_off = ATT_WIDTH + 2 * KV_WIDTH
    z_off = g_off + ATT_WIDTH
    dt_off = z_off + SSM_WIDTH + CONV_CH
    w_cat = jnp.concatenate(
        [w_in[:, ATT_WIDTH:g_off], w_in[:, z_off:dt_off], w_in[:, dt_off + SSM_HEADS:],
         w_in[:, dt_off:dt_off + SSM_HEADS], jnp.zeros((D_MODEL, LANES - SSM_HEADS), w_in.dtype),
         w_in[:, :ATT_WIDTH], w_in[:, g_off:z_off]], axis=1).astype(BF16)
    pad_heads = lambda v: jnp.pad(v.astype(F32), (0, LANES - SSM_HEADS)).reshape(1, LANES)
    k_idx = np.arange(LANES)[:, None]
    c_idx = np.arange(SSM_WIDTH)[None, :]
    e3 = jnp.asarray((k_idx < 3 * PIECE) & (k_idx % PIECE == c_idx // SSM_HEAD_DIM), BF16)
    return dict(
        pre_norm_w=pre_norm_w.reshape(1, D_MODEL), w_cat=w_cat, dt_bias=pad_heads(dt_bias),
        conv_w=conv_w, conv_b=conv_b.reshape(1, CONV_CH), a_row=pad_heads(-jnp.exp(a_log.astype(F32)) * LOG2_E),
        e3=e3, d_x=jnp.repeat(d_skip.astype(F32), SSM_HEAD_DIM).reshape(1, SSM_WIDTH), sinks=sinks.astype(F32),
        ssm_norm_w=ssm_norm_w.reshape(1, SSM_WIDTH), w_attn_o=w_attn_o.astype(BF16),
        w_ssm_o=w_ssm_o.astype(BF16), w_out=w_out.astype(BF16), post_norm_w=post_norm_w.reshape(1, D_MODEL))


PROMPT_IN_ROWS = 512
PROMPT_ATT_ROWS = 1024
PROMPT_SSD_ROWS = 512
SAMPLE_IN_ROWS = 256


def _largest_tile(rows, cap):
    for unit in (PAIR, CHUNK, SUBLANES):
        best = max((t for t in range(unit, min(rows, cap) + 1, unit) if rows % t == 0), default=0)
        if best:
            return best
    raise ValueError(f"no row tile for {rows} rows")


def _prompt_tiles(seq):
    return dict(tm=_largest_tile(seq, PROMPT_IN_ROWS), tb=_largest_tile(seq, PROMPT_ATT_ROWS),
                tc=_largest_tile(seq, PROMPT_SSD_ROWS))


def kernel(x_prompt, x_sample, cache_k, cache_v, state_conv, state_ssm, pre_norm_w, w_in, conv_w, conv_b, dt_bias,
           a_log, d_skip, sinks, ssm_norm_w, w_attn_o, w_ssm_o, w_out, post_norm_w):
    depth = w_in.shape[0]
    lp = x_prompt.shape[1]
    bs, ls, _ = x_sample.shape
    pos_p = jnp.arange(lp, dtype=F32)
    pos_s = PAST_LEN + jnp.arange(ls, dtype=F32)
    hp = SSM_HEADS * SSM_HEAD_DIM
    yp, ys = x_prompt, x_sample
    outs = [[] for _ in range(8)]
    for layer in range(depth):
        wts = _prep_weights(pre_norm_w[layer], w_in[layer], conv_w[layer], conv_b[layer], dt_bias[layer],
                            a_log[layer], d_skip[layer], sinks[layer], ssm_norm_w[layer], w_attn_o[layer],
                            w_ssm_o[layer], w_out[layer], post_norm_w[layer])
        yp, kp, vp, cp, sp = _layer_prompt(yp, pos_p, wts, **_prompt_tiles(lp))
        ys, ks, vs, cs, ss = _layer_sample(ys, pos_s,
                                           cache_k[layer].reshape(bs, WINDOW, KV_WIDTH),
                                           cache_v[layer].reshape(bs, WINDOW, KV_WIDTH),
                                           state_conv[layer], state_ssm[layer].reshape(bs, hp, SSM_STATE), wts,
                                           tm=_largest_tile(bs * ls, SAMPLE_IN_ROWS))
        for lst, val in zip(outs, (kp, vp, cp, sp, ks, vs, cs, ss)):
            lst.append(val)
    return (yp, ys) + tuple(jnp.stack(lst) for lst in outs)
```

```python
import functools

import jax
import jax.numpy as jnp
import numpy as np
from jax import lax
from jax.experimental import pallas as pl
from jax.experimental.pallas import tpu as pltpu

F32 = jnp.float32
BF16 = jnp.bfloat16

D_MODEL = 1024
PAST_LEN = 2048
CHUNK = 64
N_Q_HEADS = 16
N_KV_HEADS = 4
HEAD_DIM = 64
Q_PER_KV = N_Q_HEADS // N_KV_HEADS
ATT_WIDTH = N_Q_HEADS * HEAD_DIM
KV_WIDTH = N_KV_HEADS * HEAD_DIM
WINDOW = 128
ROT_DIM = HEAD_DIM // 4
ROPE_THETA = 500000.0
SSM_WIDTH = 2 * D_MODEL
SSM_HEAD_DIM = 64
SSM_HEADS = SSM_WIDTH // SSM_HEAD_DIM
SSM_GROUPS = 4
SSM_STATE = 128
GROUP_WIDTH = SSM_WIDTH // SSM_GROUPS
BC_WIDTH = SSM_GROUPS * SSM_STATE
CONV_W = 4
CONV_CH = SSM_WIDTH + 2 * BC_WIDTH
EPS = 1e-6
LOG2_E = 1.4426950408889634

LANES = 128
SUBLANES = 8
VMEM_LIMIT = 56 * 1024 * 1024

N_MIXED = CONV_CH // LANES
N_MIXED_Z = SSM_WIDTH // LANES
OFF_MIXED = 0
OFF_GATE_S = OFF_MIXED + 2 * LANES * N_MIXED
OFF_KV = OFF_GATE_S + D_MODEL
OFF_DT = OFF_KV + 2 * KV_WIDTH
OFF_Q = OFF_DT + LANES
OFF_G = OFF_Q + ATT_WIDTH
IN_WIDTH_PADDED = OFF_G + ATT_WIDTH
MM_COLS = 256
assert N_MIXED - N_MIXED_Z == D_MODEL // LANES and MM_COLS == 2 * LANES


def _sigmoid(x):
    return 0.5 * jnp.tanh(0.5 * x) + 0.5


def _silu(x):
    half = 0.5 * x
    return half * jnp.tanh(half) + half


def _softplus(x):
    return jnp.maximum(x, 0.0) + jnp.log1p(jnp.exp(-jnp.abs(x)))


def _compiler_params(semantics):
    return pltpu.CompilerParams(dimension_semantics=semantics, vmem_limit_bytes=VMEM_LIMIT)


def _resident(shape):
    zeros = (0,) * len(shape)
    return pl.BlockSpec(shape, lambda *_: zeros, pipeline_mode=pl.Buffered(1))


def _inproj_kernel(x_ref, nw_ref, w_ref, dtb_ref, q_ref, kv_ref, g_ref, z_ref, xbc_ref, gates_ref, dt_ref):
    x = x_ref[...]
    ms = jnp.mean(x * x, axis=-1, keepdims=True)
    h = (x * lax.rsqrt(ms + EPS) * nw_ref[...]).astype(BF16)

    def section(out_ref, off, width, act):
        step = min(MM_COLS, width)
        for n0 in range(0, width, step):
            r = jnp.dot(h, w_ref[:, off + n0:off + n0 + step], preferred_element_type=F32)
            out_ref[:, n0:n0 + step] = act(r)

    ident = lambda r: r
    section(q_ref, OFF_Q, ATT_WIDTH, ident)
    section(kv_ref, OFF_KV, 2 * KV_WIDTH, ident)
    section(g_ref, OFF_G, ATT_WIDTH, _silu)
    for j in range(N_MIXED):
        r = jnp.dot(h, w_ref[:, OFF_MIXED + j * MM_COLS:OFF_MIXED + (j + 1) * MM_COLS], preferred_element_type=F32)
        xbc_ref[:, j * LANES:(j + 1) * LANES] = r[:, 0:LANES]
        if j < N_MIXED_Z:
            z_ref[:, j * LANES:(j + 1) * LANES] = _silu(r[:, LANES:MM_COLS])
        else:
            gates_ref[:, (j - N_MIXED_Z) * LANES:(j - N_MIXED_Z + 1) * LANES] = _sigmoid(r[:, LANES:MM_COLS])
    for n0 in range(0, D_MODEL, MM_COLS):
        r = jnp.dot(h, w_ref[:, OFF_GATE_S + n0:OFF_GATE_S + n0 + MM_COLS], preferred_element_type=F32)
        gates_ref[:, D_MODEL + n0:D_MODEL + n0 + MM_COLS] = _sigmoid(r)
    lane = lax.broadcasted_iota(jnp.int32, (1, LANES), 1)
    section(dt_ref, OFF_DT, LANES,
            lambda r: jnp.where(lane < SSM_HEADS, _softplus(r + dtb_ref[...]), 0.0))


def _inproj(x2d, nw, w_cat, dtb, tm):
    m = x2d.shape[0]
    widths = (ATT_WIDTH, 2 * KV_WIDTH, ATT_WIDTH, SSM_WIDTH, CONV_CH, 2 * D_MODEL, LANES)
    return pl.pallas_call(
        _inproj_kernel,
        grid=(m // tm,),
        in_specs=[pl.BlockSpec((tm, D_MODEL), lambda i: (i, 0)),
                  _resident((1, D_MODEL)),
                  _resident((D_MODEL, IN_WIDTH_PADDED)),
                  _resident((1, LANES))],
        out_specs=[pl.BlockSpec((tm, w), lambda i: (i, 0)) for w in widths],
        out_shape=[jax.ShapeDtypeStruct((m, w), F32) for w in widths],
        compiler_params=_compiler_params(("arbitrary",)),
        name="inproj",
    )(x2d, nw, w_cat, dtb)


def _attn_kernel(q_ref, kv_ref, g_ref, ck_ref, cv_ref, cos_ref, s1_ref, s2_ref, sink_ref, wo_ref,
                 o_ref, nk_ref, nv_ref, qbuf, kbuf, vbuf, att, *, tb, ch, nblk, mask_first, n_keep):
    blk = pl.program_id(1)

    @pl.when(blk == 0)
    def _():
        kbuf[0:WINDOW, :] = ck_ref[0]
        vbuf[0:WINDOW, :] = cv_ref[0]

    cos = cos_ref[...]
    s1 = s1_ref[...]
    s2 = s2_ref[...]

    def rope(x):
        return x * cos + pltpu.roll(x, LANES - ROT_DIM // 2, 1) * s1 + pltpu.roll(x, ROT_DIM // 2, 1) * s2

    scale = HEAD_DIM ** -0.5
    for s in range(ATT_WIDTH // LANES):
        sl = slice(s * LANES, (s + 1) * LANES)
        qbuf[:, sl] = (rope(q_ref[:, sl]) * scale).astype(BF16)
    for s in range(KV_WIDTH // LANES):
        sl = slice(s * LANES, (s + 1) * LANES)
        kbuf[WINDOW:WINDOW + tb, sl] = rope(kv_ref[:, sl])
    vbuf[WINDOW:WINDOW + tb, :] = kv_ref[:, KV_WIDTH:2 * KV_WIDTH]

    nkeys = WINDOW + ch
    for c in range(tb // ch):
        r0 = c * ch
        if mask_first:
            col_chunk = lax.broadcasted_iota(jnp.int32, (1, nkeys), 1) // CHUNK
            valid = (blk * (tb // ch) + c - WINDOW // CHUNK + col_chunk) >= 0
        for h in range(N_KV_HEADS):
            hs = slice(h * HEAD_DIM, (h + 1) * HEAD_DIM)
            qs = jnp.concatenate(
                [qbuf[r0:r0 + ch, (h * Q_PER_KV + g) * HEAD_DIM:(h * Q_PER_KV + g + 1) * HEAD_DIM]
                 for g in range(Q_PER_KV)], axis=0)
            kb = kbuf[r0:r0 + nkeys, hs].astype(BF16)
            vb = vbuf[r0:r0 + nkeys, hs].astype(BF16)
            s = lax.dot_general(qs, kb, (((1,), (1,)), ((), ())), preferred_element_type=F32)
            if mask_first:
                s = jnp.where(valid, s, -jnp.inf)
            sk = sink_ref[h][:, 0:1]
            m = jnp.maximum(jnp.max(s, axis=-1, keepdims=True), sk)
            p = jnp.exp(s - m)
            denom = jnp.sum(p, axis=-1, keepdims=True) + jnp.exp(sk - m)
            o = jnp.dot(p.astype(BF16), vb, preferred_element_type=F32) / denom
            for g in range(Q_PER_KV):
                head = h * Q_PER_KV + g
                att[r0:r0 + ch, head * HEAD_DIM:(head + 1) * HEAD_DIM] = o[g * ch:(g + 1) * ch, :]

    a = (att[...] * g_ref[...]).astype(BF16)
    o_ref[...] = jnp.dot(a, wo_ref[...], preferred_element_type=F32)

    @pl.when(blk == nblk - 1)
    def _():
        nk_ref[0] = kbuf[WINDOW + tb - n_keep:WINDOW + tb, :]
        nv_ref[0] = vbuf[WINDOW + tb - n_keep:WINDOW + tb, :]

    if nblk > 1:
        kbuf[0:WINDOW, :] = kbuf[tb:tb + WINDOW, :]
        vbuf[0:WINDOW, :] = vbuf[tb:tb + WINDOW, :]


def _attention(q, kv, g, cache_k, cache_v, tables, sink_cols, wo, *, bsz, seq, tb, ch, mask_first):
    nblk = seq // tb
    n_keep = min(WINDOW, seq)
    cos, s1, s2 = tables
    row = lambda b, j: (b * nblk + j, 0)
    kern = functools.partial(_attn_kernel, tb=tb, ch=ch, nblk=nblk, mask_first=mask_first, n_keep=n_keep)
    return pl.pallas_call(
        kern,
        grid=(bsz, nblk),
        in_specs=[pl.BlockSpec((tb, ATT_WIDTH), row),
                  pl.BlockSpec((tb, 2 * KV_WIDTH), row),
                  pl.BlockSpec((tb, ATT_WIDTH), row),
                  pl.BlockSpec((1, WINDOW, KV_WIDTH), lambda b, j: (b, 0, 0)),
                  pl.BlockSpec((1, WINDOW, KV_WIDTH), lambda b, j: (b, 0, 0)),
                  pl.BlockSpec((tb, LANES), lambda b, j: (j, 0)),
                  pl.BlockSpec((tb, LANES), lambda b, j: (j, 0)),
                  pl.BlockSpec((tb, LANES), lambda b, j: (j, 0)),
                  _resident((N_KV_HEADS, Q_PER_KV * ch, LANES)),
                  _resident((ATT_WIDTH, D_MODEL))],
        out_specs=[pl.BlockSpec((tb, D_MODEL), row),
                   pl.BlockSpec((1, n_keep, KV_WIDTH), lambda b, j: (b, 0, 0)),
                   pl.BlockSpec((1, n_keep, KV_WIDTH), lambda b, j: (b, 0, 0))],
        out_shape=[jax.ShapeDtypeStruct((bsz * seq, D_MODEL), F32),
                   jax.ShapeDtypeStruct((bsz, n_keep, KV_WIDTH), F32),
                   jax.ShapeDtypeStruct((bsz, n_keep, KV_WIDTH), F32)],
        scratch_shapes=[pltpu.VMEM((tb, ATT_WIDTH), BF16),
                        pltpu.VMEM((WINDOW + tb, KV_WIDTH), F32),
                        pltpu.VMEM((WINDOW + tb, KV_WIDTH), F32),
                        pltpu.VMEM((tb, ATT_WIDTH), F32)],
        compiler_params=_compiler_params(("arbitrary", "arbitrary")),
        name="attention",
    )(q, kv, g, cache_k, cache_v, cos, s1, s2, sink_cols, wo)


PN_K = OFF_KV
PN_V = PN_K + KV_WIDTH
PT_Q = 0
PT_G = PT_Q + ATT_WIDTH
PT_V = PT_G + ATT_WIDTH
PT_WIDTH = PT_V + KV_WIDTH
HALF_ROT = ROT_DIM // 2
PAIR = 2 * CHUNK
KEY_WIN = WINDOW + PAIR


def _inproj_t_kernel(x_ref, nw_ref, w_ref, dtb_ref, cosq_ref, sinq_ref, cosk_ref, s1k_ref, s2k_ref,
                     cw_ref, cb_ref,
                     qt_ref, gt_ref, vt_ref, k_ref, v_ref, z_ref, xc_ref, gates_ref, dt_ref, ctail_ref,
                     cprev, wt_ref, *, tm, nper):
    step = pl.program_id(0)
    seq_start = step % nper == 0
    cur = step % 2

    @pl.when(step == 0)
    def _():
        for dst, src, width in ((PT_Q, OFF_Q, ATT_WIDTH), (PT_G, OFF_G, ATT_WIDTH), (PT_V, PN_V, KV_WIDTH)):
            for n0 in range(0, width, MM_COLS):
                wt_ref[dst + n0:dst + n0 + MM_COLS, :] = w_ref[:, src + n0:src + n0 + MM_COLS].T

    x = x_ref[...]
    ms = jnp.mean(x * x, axis=-1, keepdims=True)
    h = (x * lax.rsqrt(ms + EPS) * nw_ref[...]).astype(BF16)

    def mm_t(off, rows):
        return lax.dot_general(wt_ref[off:off + rows, :], h, (((1,), (1,)), ((), ())),
                               preferred_element_type=F32)

    cosq = cosq_ref[...]
    sinq = sinq_ref[...]
    scale = HEAD_DIM ** -0.5 * LOG2_E


    def q_tail(n0, r):
        r = r * scale
        for a in range(MM_COLS // HEAD_DIM):
            base = a * HEAD_DIM
            x1 = r[base:base + HALF_ROT]
            x2 = r[base + HALF_ROT:base + ROT_DIM]
            rot = jnp.concatenate([x1 * cosq - x2 * sinq, x2 * cosq + x1 * sinq], axis=0)
            qt_ref[n0 + base:n0 + base + ROT_DIM, :] = rot.astype(BF16)
            qt_ref[n0 + base + ROT_DIM:n0 + base + HEAD_DIM, :] = r[base + ROT_DIM:base + HEAD_DIM].astype(BF16)

    def g_tail(n0, r):
        gt_ref[n0:n0 + MM_COLS, :] = _silu(r).astype(BF16)

    def vt_tail(r):
        vt_ref[...] = r.astype(BF16)

    def mm_n(off, width):
        return jnp.dot(h, w_ref[:, off:off + width], preferred_element_type=F32)

    def nat_tail(out_ref, n0, width, act, r):
        out_ref[:, n0:n0 + width] = act(r).astype(out_ref.dtype)

    def k_tail(k_all):
        for s in range(KV_WIDTH // LANES):
            kk = k_all[:, s * LANES:(s + 1) * LANES]
            k_ref[:, s * LANES:(s + 1) * LANES] = (
                kk * cosk_ref[...] + pltpu.roll(kk, LANES - HALF_ROT, 1) * s1k_ref[...]
                + pltpu.roll(kk, HALF_ROT, 1) * s2k_ref[...])

    row8 = lax.broadcasted_iota(jnp.int32, (SUBLANES, LANES), 0)

    def mixed_tail(j, r2):
        cols = slice(j * LANES, (j + 1) * LANES)
        r = r2[:, 0:LANES]
        prev = jnp.where(seq_start, 0.0, cprev[cur, :, cols])
        acc = cb_ref[:, cols] + r * cw_ref[CONV_W - 1:CONV_W, cols]
        for k in range(1, CONV_W):
            rolled = pltpu.roll(r, k, 0)
            top = jnp.where(row8 < k, pltpu.roll(prev, k, 0), rolled[0:SUBLANES])
            shifted = jnp.concatenate([top, rolled[SUBLANES:tm]], axis=0)
            acc = acc + shifted * cw_ref[CONV_W - 1 - k:CONV_W - k, cols]
        xc_ref[:, cols] = _silu(acc).astype(xc_ref.dtype)
        cprev[1 - cur, :, cols] = r[tm - SUBLANES:tm]
        ctail_ref[0, :, cols] = r[tm - SUBLANES:tm]
        other = r2[:, LANES:MM_COLS]
        if j < N_MIXED_Z:
            z_ref[:, cols] = _silu(other).astype(z_ref.dtype)
        else:
            gates_ref[:, (j - N_MIXED_Z) * LANES:(j - N_MIXED_Z + 1) * LANES] = _sigmoid(other).astype(
                gates_ref.dtype)

    def gate_s_tail(n0, r):
        gates_ref[:, D_MODEL + n0:D_MODEL + n0 + MM_COLS] = _sigmoid(r).astype(gates_ref.dtype)

    lane = lax.broadcasted_iota(jnp.int32, (1, LANES), 1)
    ident = lambda r: r
    dt_act = lambda r: jnp.where(lane < SSM_HEADS, _softplus(r + dtb_ref[...]), 0.0)
    P = functools.partial
    mixed = [(P(mm_n, OFF_MIXED + j * MM_COLS, MM_COLS), P(mixed_tail, j)) for j in range(N_MIXED)]
    rest = ([(P(mm_t, PT_Q + n0, MM_COLS), P(q_tail, n0)) for n0 in range(0, ATT_WIDTH, MM_COLS)]
            + [(P(mm_t, PT_G + n0, MM_COLS), P(g_tail, n0)) for n0 in range(0, ATT_WIDTH, MM_COLS)]
            + [(P(mm_n, OFF_GATE_S + n0, MM_COLS), P(gate_s_tail, n0)) for n0 in range(0, D_MODEL, MM_COLS)]
            + [(P(mm_t, PT_V, KV_WIDTH), vt_tail),
               (P(mm_n, PN_K, KV_WIDTH), k_tail),
               (P(mm_n, PN_V, KV_WIDTH), P(nat_tail, v_ref, 0, KV_WIDTH, ident)),
               (P(mm_n, OFF_DT, LANES), P(nat_tail, dt_ref, 0, LANES, dt_act))])
    order = []
    for idx in range(max(len(mixed), len(rest))):
        order += mixed[idx:idx + 1] + rest[idx:idx + 1]
    pending = order[0][0]()
    for idx, (_, tail) in enumerate(order):
        nxt = order[idx + 1][0]() if idx + 1 < len(order) else None
        tail(pending)
        pending = nxt


def _inproj_t(x2d, nw, w_cat, dtb, qtabs, ktabs, cw, cb, seq, tm):
    m = x2d.shape[0]
    nper = seq // tm
    row = lambda i: (i, 0)
    col = lambda i: (0, i)
    nat_widths = (KV_WIDTH, KV_WIDTH, SSM_WIDTH, CONV_CH, 2 * D_MODEL, LANES)
    nat_dtypes = (F32, F32, BF16, BF16, BF16, F32)
    return pl.pallas_call(
        functools.partial(_inproj_t_kernel, tm=tm, nper=nper),
        grid=(m // tm,),
        in_specs=[pl.BlockSpec((tm, D_MODEL), row),
                  _resident((1, D_MODEL)),
                  _resident((D_MODEL, IN_WIDTH_PADDED)),
                  _resident((1, LANES)),
                  pl.BlockSpec((HALF_ROT, tm), lambda i: (0, i % nper)),
                  pl.BlockSpec((HALF_ROT, tm), lambda i: (0, i % nper)),
                  pl.BlockSpec((tm, LANES), lambda i: (i % nper, 0)),
                  pl.BlockSpec((tm, LANES), lambda i: (i % nper, 0)),
                  pl.BlockSpec((tm, LANES), lambda i: (i % nper, 0)),
                  _resident((CONV_W, CONV_CH)),
                  _resident((1, CONV_CH))],
        out_specs=[pl.BlockSpec((ATT_WIDTH, tm), col),
                   pl.BlockSpec((ATT_WIDTH, tm), col),
                   pl.BlockSpec((KV_WIDTH, tm), col)]
                  + [pl.BlockSpec((tm, w), row) for w in nat_widths]
                  + [pl.BlockSpec((1, SUBLANES, CONV_CH), lambda i: (i // nper, 0, 0))],
        out_shape=[jax.ShapeDtypeStruct((ATT_WIDTH, m), BF16),
                   jax.ShapeDtypeStruct((ATT_WIDTH, m), BF16),
                   jax.ShapeDtypeStruct((KV_WIDTH, m), BF16)]
                  + [jax.ShapeDtypeStruct((m, w), d) for w, d in zip(nat_widths, nat_dtypes)]
                  + [jax.ShapeDtypeStruct((m // seq, SUBLANES, CONV_CH), F32)],
        scratch_shapes=[pltpu.VMEM((2, SUBLANES, CONV_CH), F32), pltpu.VMEM((PT_WIDTH, D_MODEL), BF16)],
        compiler_params=_compiler_params(("arbitrary",)),
        name="inproj_t",
    )(x2d, nw, w_cat, dtb, *qtabs, *ktabs, cw, cb)


def _attn_t_kernel(qt_ref, k_ref, vt_ref, gt_ref, bias_ref, sink_ref, wo_ref, o_ref, kbuf, vtbuf, att_t, *, tb):
    blk = pl.program_id(1)
    cur = blk % 2
    prv = 1 - cur

    @pl.when(blk == 0)
    def _():
        kbuf[1, tb - WINDOW:tb, :] = jnp.zeros((WINDOW, KV_WIDTH), BF16)
        vtbuf[1, :, tb - WINDOW:tb] = jnp.zeros((KV_WIDTH, WINDOW), BF16)

    kbuf[cur] = k_ref[...].astype(BF16)
    vtbuf[cur] = vt_ref[...]
    seq_start = jnp.where(blk == 0, -jnp.inf, 0.0)

    batch_dims = (((2,), (1,)), ((0,), (0,)))
    for pr in range(tb // PAIR):
        c0 = pr * PAIR
        k_wins, vt_wins, qs = [], [], []
        for h in range(N_KV_HEADS):
            hs = slice(h * HEAD_DIM, (h + 1) * HEAD_DIM)
            if pr == 0:
                k_wins.append(jnp.concatenate([kbuf[prv, tb - WINDOW:tb, hs], kbuf[cur, 0:PAIR, hs]], axis=0))
                vt_wins.append(jnp.concatenate([vtbuf[prv, hs, tb - WINDOW:tb], vtbuf[cur, hs, 0:PAIR]], axis=1))
            else:
                k_wins.append(kbuf[cur, c0 - WINDOW:c0 + PAIR, hs])
                vt_wins.append(vtbuf[cur, hs, c0 - WINDOW:c0 + PAIR])
            qs.append(jnp.concatenate(
                [qt_ref[(h * Q_PER_KV + g) * HEAD_DIM:(h * Q_PER_KV + g + 1) * HEAD_DIM, c0:c0 + PAIR]
                 for g in range(Q_PER_KV)], axis=1))
        s = lax.dot_general(jnp.stack(k_wins), jnp.stack(qs), batch_dims,
                            preferred_element_type=F32)
        lo = bias_ref[0:CHUNK] + seq_start if pr == 0 else bias_ref[0:CHUNK]
        mid = [s[:, CHUNK:WINDOW] + seq_start, s[:, WINDOW:WINDOW + CHUNK]] if pr == 0 else [s[:, CHUNK:WINDOW + CHUNK]]
        s = jnp.concatenate([s[:, 0:CHUNK] + lo[None]] + mid
                            + [s[:, WINDOW + CHUNK:KEY_WIN] + bias_ref[WINDOW + CHUNK:KEY_WIN][None]], axis=1)
        sk = sink_ref[...][:, None, :]
        m = jnp.maximum(jnp.max(s, axis=1, keepdims=True), sk)
        p = jnp.exp2(s - m)
        denom = jnp.sum(p, axis=1, keepdims=True) + jnp.exp2(sk - m)
        o = lax.dot_general(jnp.stack(vt_wins), p.astype(BF16), batch_dims, preferred_element_type=F32) / denom
        for h in range(N_KV_HEADS):
            for g in range(Q_PER_KV):
                rows = slice((h * Q_PER_KV + g) * HEAD_DIM, (h * Q_PER_KV + g + 1) * HEAD_DIM)
                att_t[rows, c0:c0 + PAIR] = (
                    o[h, :, g * PAIR:(g + 1) * PAIR] * gt_ref[rows, c0:c0 + PAIR].astype(F32)).astype(BF16)

    o_ref[...] = lax.dot_general(att_t[...], wo_ref[...], (((0,), (0,)), ((), ())), preferred_element_type=F32)


def _attention_t(qt, k, vt, gt, bias, sink_rows, wo, *, bsz, seq, tb):
    nblk = seq // tb
    row = lambda b, j: (b * nblk + j, 0)
    col = lambda b, j: (0, b * nblk + j)
    return pl.pallas_call(
        functools.partial(_attn_t_kernel, tb=tb),
        grid=(bsz, nblk),
        in_specs=[pl.BlockSpec((ATT_WIDTH, tb), col),
                  pl.BlockSpec((tb, KV_WIDTH), row),
                  pl.BlockSpec((KV_WIDTH, tb), col),
                  pl.BlockSpec((ATT_WIDTH, tb), col),
                  _resident((KEY_WIN, Q_PER_KV * PAIR)),
                  _resident((N_KV_HEADS, Q_PER_KV * PAIR)),
                  _resident((ATT_WIDTH, D_MODEL))],
        out_specs=pl.BlockSpec((tb, D_MODEL), row),
        out_shape=jax.ShapeDtypeStruct((bsz * seq, D_MODEL), F32),
        scratch_shapes=[pltpu.VMEM((2, tb, KV_WIDTH), BF16),
                        pltpu.VMEM((2, KV_WIDTH, tb), BF16),
                        pltpu.VMEM((ATT_WIDTH, tb), BF16)],
        compiler_params=_compiler_params(("arbitrary", "arbitrary")),
        name="attention_t",
    )(qt, k, vt, gt, bias, sink_rows, wo)


def _window_bias():
    key_chunk = np.arange(KEY_WIN)[:, None] // CHUNK
    q_chunk = (np.arange(Q_PER_KV * PAIR)[None, :] % PAIR) // CHUNK
    ok = (key_chunk >= q_chunk) & (key_chunk <= q_chunk + WINDOW // CHUNK)
    return jnp.asarray(np.where(ok, 0.0, -np.inf), F32)


SSD_Q = 64
PIECE = 32
QUAD = 4 * SSM_HEAD_DIM


def _split3(x):
    hi = x.astype(BF16).astype(F32)
    r1 = x - hi
    mid = r1.astype(BF16).astype(F32)
    lo = (r1 - mid).astype(BF16).astype(F32)
    return (hi + pltpu.roll(mid, PIECE, 1) + pltpu.roll(lo, 2 * PIECE, 1)).astype(BF16)


def _ssd_kernel(*refs, tc, out_row, conv_in_kernel):
    if conv_in_kernel:
        (xbc_ref, z_ref, dt_ref, x_ref, ba_ref, gates_ref, h0_ref, cs0_ref, cw_ref, cb_ref,
         a_ref, e3_ref, dx_ref, wso_ref, wout_ref, pnw_ref, eye_ref, tril_ref, qmask_ref,
         y_ref, nssm_ref, nconv_ref, ht, ybuf, dtx_buf, cumx_buf, cb_buf, xpad, xc) = refs
    else:
        (xc, z_ref, dt_ref, x_ref, ba_ref, gates_ref, h0_ref,
         a_ref, e3_ref, dx_ref, wso_ref, wout_ref, pnw_ref, eye_ref, tril_ref, qmask_ref,
         y_ref, nssm_ref, ht, ybuf, dtx_buf, cumx_buf, cb_buf) = refs
    blk = pl.program_id(1)
    nblk = pl.num_programs(1)

    @pl.when(blk == 0)
    def _():
        ht[...] = h0_ref[0].T

    if conv_in_kernel:
        pad0 = SUBLANES - (CONV_W - 1)
        xpad[pad0:SUBLANES, :] = cs0_ref[0]
        xpad[SUBLANES:SUBLANES + tc, :] = xbc_ref[...]
        acc = cb_ref[...] + xpad[pad0:pad0 + tc, :] * cw_ref[0:1, :]
        for tap in range(1, CONV_W):
            acc = acc + xpad[pad0 + tap:pad0 + tap + tc, :] * cw_ref[tap:tap + 1, :]
        xc[...] = _silu(acc)
        nconv_ref[0] = xpad[SUBLANES + out_row - (CONV_W - 1):SUBLANES + out_row, :]

    n_chunks = tc // SSD_Q
    lane = lax.broadcasted_iota(jnp.int32, (tc, LANES), 1)
    ri = lax.broadcasted_iota(jnp.int32, (n_chunks, SSD_Q, SSD_Q), 1)
    rj = lax.broadcasted_iota(jnp.int32, (n_chunks, SSD_Q, SSD_Q), 2)
    tri = (ri >= rj).astype(BF16)
    dx = dx_ref[...]

    def expand(v):
        return jnp.dot(_split3(v), e3_ref[...], preferred_element_type=F32)

    dt_all = dt_ref[...]
    da3 = _split3(dt_all * a_ref[...]).reshape(n_chunks, SSD_Q, LANES)
    cum3 = lax.dot_general(tri, da3, (((2,), (1,)), ((0,), (0,))),
                           preferred_element_type=F32).reshape(tc, LANES)
    cum_all = jnp.where(lane < PIECE,
                        cum3 + pltpu.roll(cum3, LANES - PIECE, 1) + pltpu.roll(cum3, LANES - 2 * PIECE, 1), 0.0)
    dtx_buf[...] = expand(dt_all)
    cumx_buf[...] = expand(cum_all)
    c_parts, b_parts = [], []
    for c in range(n_chunks):
        rws = slice(c * SSD_Q, (c + 1) * SSD_Q)
        for g in range(SSM_GROUPS):
            bg = xc[rws, SSM_WIDTH + g * SSM_STATE:SSM_WIDTH + (g + 1) * SSM_STATE].astype(BF16)
            c_parts.append(xc[rws, SSM_WIDTH + BC_WIDTH + g * SSM_STATE:
                              SSM_WIDTH + BC_WIDTH + (g + 1) * SSM_STATE].astype(BF16))
            b_parts.append(jnp.concatenate([bg] * 4, axis=0))
    cb_buf[...] = lax.dot_general(jnp.stack(c_parts), jnp.stack(b_parts), (((2,), (2,)), ((0,), (0,))),
                                  preferred_element_type=F32)

    def chunk(c, carry):
        r0 = pl.multiple_of(c * SSD_Q, SSD_Q)
        rows = pl.ds(r0, SSD_Q)
        xs = xc[rows, 0:SSM_WIDTH].astype(F32)
        bm = xc[rows, SSM_WIDTH:SSM_WIDTH + BC_WIDTH].astype(BF16)
        cm = xc[rows, SSM_WIDTH + BC_WIDTH:CONV_CH].astype(BF16)
        dt_x = dtx_buf[rows, :]
        cum_x = cumx_buf[rows, :]
        ecum_x = jnp.exp2(cum_x)
        cum_last = cum_x[SSD_Q - 1:SSD_Q, :]
        xdt = xs * dt_x
        xdt_b = xdt.astype(BF16)
        xw = (xdt * jnp.exp2(cum_last - cum_x)).astype(BF16)
        cum_j = jnp.sum(cum_x * eye_ref[...], axis=0, keepdims=True)
        decay = jnp.exp2(cum_x - cum_j + tril_ref[...])

        y_parts = []
        for g in range(SSM_GROUPS):
            cg = cm[:, g * SSM_STATE:(g + 1) * SSM_STATE]
            bg = bm[:, g * SSM_STATE:(g + 1) * SSM_STATE]
            cb4 = cb_buf[c * SSM_GROUPS + g]
            gs = slice(g * GROUP_WIDTH, (g + 1) * GROUP_WIDTH)
            y_state = jnp.dot(cg, ht[:, gs].astype(BF16), preferred_element_type=F32) * ecum_x[:, gs]
            for half in range(GROUP_WIDTH // QUAD):
                qs = slice(g * GROUP_WIDTH + half * QUAD, g * GROUP_WIDTH + (half + 1) * QUAD)
                m4 = (cb4 * decay[:, qs]).astype(BF16)
                x4 = xdt_b[:, qs]
                bd = jnp.concatenate([x4 * qmask_ref[a:a + 1, :] for a in range(4)], axis=0)
                y_parts.append(jnp.dot(m4, bd, preferred_element_type=F32)
                               + y_state[:, half * QUAD:(half + 1) * QUAD])
            ht[:, gs] = (ht[:, gs] * ecum_x[SSD_Q - 1:SSD_Q, gs]
                         + jnp.dot(bg.T, xw[:, gs], preferred_element_type=F32))
        y = jnp.concatenate(y_parts, axis=1) + xs * dx
        yz = y * z_ref[rows, :].astype(F32)
        for g in range(SSM_GROUPS):
            gs = slice(g * GROUP_WIDTH, (g + 1) * GROUP_WIDTH)
            yg = yz[:, gs]
            ms = jnp.mean(yg * yg, axis=-1, keepdims=True)
            ybuf[rows, gs] = (yg * lax.rsqrt(ms + EPS)).astype(BF16)
        return carry

    lax.fori_loop(0, n_chunks, chunk, 0, unroll=min(4, n_chunks))
    branch_s = jnp.dot(ybuf[...], wso_ref[...], preferred_element_type=F32)
    merged = (gates_ref[:, 0:D_MODEL].astype(F32) * ba_ref[...]
              + gates_ref[:, D_MODEL:2 * D_MODEL].astype(F32) * branch_s)
    out = jnp.dot(merged.astype(BF16), wout_ref[...], preferred_element_type=F32)
    ms = jnp.mean(out * out, axis=-1, keepdims=True)
    y_ref[...] = x_ref[...] + out * lax.rsqrt(ms + EPS) * pnw_ref[...]

    @pl.when(blk == nblk - 1)
    def _():
        nssm_ref[0] = ht[...].T


def _ssd_masks():
    i = np.arange(SSD_Q)[:, None]
    j = np.arange(SSM_WIDTH)[None, :] % SSD_Q
    eye = jnp.asarray(i == j, F32)
    tril = jnp.asarray(np.where(i >= j, 0.0, -np.inf), F32)
    qmask = jnp.asarray(np.arange(QUAD)[None, :] // SSM_HEAD_DIM == np.arange(4)[:, None], BF16)
    return eye, tril, qmask


def _ssd(xc, zs, dt, x2d, branch_a, gates, ssm0, conv0, wts, *, bsz, seq, tc, l_real, conv_in_kernel):
    nblk = seq // tc
    assert not conv_in_kernel or nblk == 1
    row = lambda b, j: (b * nblk + j, 0)
    per_seq = lambda b, j: (b, 0, 0)
    hp = SSM_HEADS * SSM_HEAD_DIM
    kern = functools.partial(_ssd_kernel, tc=tc, out_row=l_real, conv_in_kernel=conv_in_kernel)
    operands = [xc, zs, dt, x2d, branch_a, gates, ssm0]
    in_specs = [pl.BlockSpec((tc, CONV_CH), row),
                pl.BlockSpec((tc, SSM_WIDTH), row),
                pl.BlockSpec((tc, LANES), row),
                pl.BlockSpec((tc, D_MODEL), row),
                pl.BlockSpec((tc, D_MODEL), row),
                pl.BlockSpec((tc, 2 * D_MODEL), row),
                pl.BlockSpec((1, hp, SSM_STATE), per_seq)]
    out_specs = [pl.BlockSpec((tc, D_MODEL), row), pl.BlockSpec((1, hp, SSM_STATE), per_seq)]
    out_shape = [jax.ShapeDtypeStruct((bsz * seq, D_MODEL), F32),
                 jax.ShapeDtypeStruct((bsz, hp, SSM_STATE), F32)]
    scratch = [pltpu.VMEM((SSM_STATE, SSM_WIDTH), F32), pltpu.VMEM((tc, SSM_WIDTH), BF16),
               pltpu.VMEM((tc, SSM_WIDTH), F32), pltpu.VMEM((tc, SSM_WIDTH), F32),
               pltpu.VMEM((tc // SSD_Q * SSM_GROUPS, SSD_Q, QUAD), F32)]
    if conv_in_kernel:
        operands += [conv0, wts["conv_w"], wts["conv_b"]]
        in_specs += [pl.BlockSpec((1, CONV_W - 1, CONV_CH), per_seq),
                     _resident((CONV_W, CONV_CH)), _resident((1, CONV_CH))]
        out_specs.append(pl.BlockSpec((1, CONV_W - 1, CONV_CH), per_seq))
        out_shape.append(jax.ShapeDtypeStruct((bsz, CONV_W - 1, CONV_CH), F32))
        scratch += [pltpu.VMEM((SUBLANES + tc, CONV_CH), F32), pltpu.VMEM((tc, CONV_CH), F32)]
    consts = [wts["a_row"], wts["e3"], wts["d_x"], wts["w_ssm_o"], wts["w_out"],
              wts["post_norm_w"], *_ssd_masks()]
    operands += consts
    in_specs += [_resident(c.shape) for c in consts]
    return pl.pallas_call(
        kern,
        grid=(bsz, nblk),
        in_specs=in_specs,
        out_specs=out_specs,
        out_shape=out_shape,
        scratch_shapes=scratch,
        compiler_params=_compiler_params(("arbitrary", "arbitrary")),
        name="ssd",
    )(*operands)


def _rope_angles(pos):
    half = ROT_DIM // 2
    inv = ROPE_THETA ** (-(jnp.arange(half, dtype=F32) * 2.0 / ROT_DIM))
    ang = pos.astype(F32)[:, None] * inv[None, :]
    return jnp.cos(ang), jnp.sin(ang)


def _rope_tables(pos):
    cos, sin = _rope_angles(pos)
    half = ROT_DIM // 2
    n = pos.shape[0]
    rest = HEAD_DIM - ROT_DIM
    c = jnp.concatenate([cos, cos, jnp.ones((n, rest), F32)], axis=1)
    s1 = jnp.concatenate([-sin, jnp.zeros((n, half + rest), F32)], axis=1)
    s2 = jnp.concatenate([jnp.zeros((n, half), F32), sin, jnp.zeros((n, rest), F32)], axis=1)
    rep = LANES // HEAD_DIM
    return tuple(jnp.tile(t, (1, rep)) for t in (c, s1, s2))


def _layer_prompt(x, pos, wts, *, tm, tb, tc):
    bsz, seq, _ = x.shape
    x2d = x.reshape(bsz * seq, D_MODEL)
    cos, sin = _rope_angles(pos)
    qt, gt, vt, k, v, zs, xc, gates, dt, conv_tail = _inproj_t(
        x2d, wts["pre_norm_w"], wts["w_cat"], wts["dt_bias"], (cos.T, sin.T), _rope_tables(pos),
        wts["conv_w"], wts["conv_b"], seq, tm)
    sink_rows = jnp.repeat(wts["sinks"].reshape(N_KV_HEADS, Q_PER_KV), PAIR, axis=1) * LOG2_E
    branch_a = _attention_t(qt, k, vt, gt, _window_bias(), sink_rows, wts["w_attn_o"], bsz=bsz, seq=seq, tb=tb)
    n_keep = min(WINDOW, seq)
    keep = lambda t: t.reshape(bsz, seq, KV_WIDTH)[:, seq - n_keep:].reshape(bsz, n_keep, N_KV_HEADS, HEAD_DIM)
    zero_ssm = jnp.zeros((bsz, SSM_HEADS * SSM_HEAD_DIM, SSM_STATE), F32)
    y, new_ssm = _ssd(xc, zs, dt, x2d, branch_a, gates, zero_ssm, None, wts,
                      bsz=bsz, seq=seq, tc=tc, l_real=seq, conv_in_kernel=False)
    return (y.reshape(bsz, seq, D_MODEL), keep(k), keep(v), conv_tail[:, SUBLANES - (CONV_W - 1):],
            new_ssm.reshape(bsz, SSM_HEADS, SSM_HEAD_DIM, SSM_STATE))


def _layer_sample(x, pos, cache_k, cache_v, conv0, ssm0, wts, *, tm):
    bsz, seq, _ = x.shape
    x2d = x.reshape(bsz * seq, D_MODEL)
    q, kv, g, zs, xbc, gates, dt = _inproj(x2d, wts["pre_norm_w"], wts["w_cat"], wts["dt_bias"], tm)
    sink_cols = jnp.broadcast_to(
        jnp.repeat(wts["sinks"].reshape(N_KV_HEADS, Q_PER_KV), seq, axis=1)[:, :, None],
        (N_KV_HEADS, Q_PER_KV * seq, LANES))
    branch_a, new_k, new_v = _attention(
        q, kv, g, cache_k, cache_v, _rope_tables(pos), sink_cols, wts["w_attn_o"],
        bsz=bsz, seq=seq, tb=seq, ch=seq, mask_first=False)
    seq_p = -(-seq // SSD_Q) * SSD_Q
    padrows = lambda t: jnp.pad(t.reshape(bsz, seq, -1), ((0, 0), (0, seq_p - seq), (0, 0))).reshape(
        bsz * seq_p, -1)
    y, new_ssm, new_conv = _ssd(
        padrows(xbc), padrows(zs), padrows(dt), padrows(x2d), padrows(branch_a), padrows(gates), ssm0, conv0, wts,
        bsz=bsz, seq=seq_p, tc=seq_p, l_real=seq, conv_in_kernel=True)
    n_keep = new_k.shape[1]
    return (y.reshape(bsz, seq_p, D_MODEL)[:, :seq],
            new_k.reshape(bsz, n_keep, N_KV_HEADS, HEAD_DIM), new_v.reshape(bsz, n_keep, N_KV_HEADS, HEAD_DIM),
            new_conv, new_ssm.reshape(bsz, SSM_HEADS, SSM_HEAD_DIM, SSM_STATE))


def _prep_weights(pre_norm_w, w_in, conv_w, conv_b, dt_bias, a_log, d_skip, sinks, ssm_norm_w, w_attn_o,
                  w_ssm_o, w_out, post_norm_w):
    g_off = ATT_WIDTH + 2 * KV_WIDTH
    z_off = g_off + ATT_WIDTH
    dt_off = z_off + SSM_WIDTH + CONV_CH
    xbc_off = z_off + SSM_WIDTH
    ga_off = dt_off + SSM_HEADS
    gs_off = ga_off + D_MODEL
    tiles = lambda w: w.reshape(D_MODEL, -1, 1, LANES)
    companions = jnp.concatenate([w_in[:, z_off:xbc_off], w_in[:, ga_off:gs_off]], axis=1)
    mixed = jnp.concatenate([tiles(w_in[:, xbc_off:dt_off]), tiles(companions)], axis=2).reshape(D_MODEL, -1)
    w_cat = jnp.concatenate(
        [mixed, w_in[:, gs_off:], w_in[:, ATT_WIDTH:g_off],
         w_in[:, dt_off:dt_off + SSM_HEADS], jnp.zeros((D_MODEL, LANES - SSM_HEADS), w_in.dtype),
         w_in[:, :ATT_WIDTH], w_in[:, g_off:z_off]], axis=1).astype(BF16)
    pad_heads = lambda v: jnp.pad(v.astype(F32), (0, LANES - SSM_HEADS)).reshape(1, LANES)
    k_idx = np.arange(LANES)[:, None]
    c_idx = np.arange(SSM_WIDTH)[None, :]
    e3 = jnp.asarray((k_idx < 3 * PIECE) & (k_idx % PIECE == c_idx // SSM_HEAD_DIM), BF16)
    return dict(
        pre_norm_w=pre_norm_w.reshape(1, D_MODEL), w_cat=w_cat, dt_bias=pad_heads(dt_bias),
        conv_w=conv_w, conv_b=conv_b.reshape(1, CONV_CH), a_row=pad_heads(-jnp.exp(a_log.astype(F32)) * LOG2_E),
        e3=e3, d_x=jnp.repeat(d_skip.astype(F32), SSM_HEAD_DIM).reshape(1, SSM_WIDTH), sinks=sinks.astype(F32),
        w_attn_o=w_attn_o.astype(BF16),
        w_ssm_o=(ssm_norm_w.astype(F32)[:, None] * w_ssm_o).astype(BF16),
        w_out=w_out.astype(BF16), post_norm_w=post_norm_w.reshape(1, D_MODEL))


PROMPT_IN_ROWS = 512
PROMPT_ATT_ROWS = 1024
PROMPT_SSD_ROWS = 512
SAMPLE_IN_ROWS = 256


def _largest_tile(rows, cap):
    for unit in (PAIR, CHUNK, SUBLANES):
        best = max((t for t in range(unit, min(rows, cap) + 1, unit) if rows % t == 0), default=0)
        if best:
            return best
    raise ValueError(f"no row tile for {rows} rows")


def _prompt_tiles(seq):
    return dict(tm=_largest_tile(seq, PROMPT_IN_ROWS), tb=_largest_tile(seq, PROMPT_ATT_ROWS),
                tc=_largest_tile(seq, PROMPT_SSD_ROWS))


def kernel(x_prompt, x_sample, cache_k, cache_v, state_conv, state_ssm, pre_norm_w, w_in, conv_w, conv_b, dt_bias,
           a_log, d_skip, sinks, ssm_norm_w, w_attn_o, w_ssm_o, w_out, post_norm_w):
    depth = w_in.shape[0]
    lp = x_prompt.shape[1]
    bs, ls, _ = x_sample.shape
    pos_p = jnp.arange(lp, dtype=F32)
    pos_s = PAST_LEN + jnp.arange(ls, dtype=F32)
    hp = SSM_HEADS * SSM_HEAD_DIM
    yp, ys = x_prompt, x_sample
    outs = [[] for _ in range(8)]
    for layer in range(depth):
        wts = _prep_weights(pre_norm_w[layer], w_in[layer], conv_w[layer], conv_b[layer], dt_bias[layer],
                            a_log[layer], d_skip[layer], sinks[layer], ssm_norm_w[layer], w_attn_o[layer],
                            w_ssm_o[layer], w_out[layer], post_norm_w[layer])
        yp, kp, vp, cp, sp = _layer_prompt(yp, pos_p, wts, **_prompt_tiles(lp))
        ys, ks, vs, cs, ss = _layer_sample(ys, pos_s,
                                           cache_k[layer].reshape(bs, WINDOW, KV_WIDTH),
                                           cache_v[layer].reshape(bs, WINDOW, KV_WIDTH),
                                           state_conv[layer], state_ssm[layer].reshape(bs, hp, SSM_STATE), wts,
                                           tm=_largest_tile(bs * ls, SAMPLE_IN_ROWS))
        for lst, val in zip(outs, (kp, vp, cp, sp, ks, vs, cs, ss)):
            lst.append(val)
    return (yp, ys) + tuple(jnp.stack(lst) for lst in outs)
```

```python
import functools

import jax
import jax.numpy as jnp
import numpy as np
from jax import lax
from jax.experimental import pallas as pl
from jax.experimental.pallas import tpu as pltpu

F32 = jnp.float32
BF16 = jnp.bfloat16

D_MODEL = 1024
PAST_LEN = 2048
CHUNK = 64
N_Q_HEADS = 16
N_KV_HEADS = 4
HEAD_DIM = 64
Q_PER_KV = N_Q_HEADS // N_KV_HEADS
ATT_WIDTH = N_Q_HEADS * HEAD_DIM
KV_WIDTH = N_KV_HEADS * HEAD_DIM
WINDOW = 128
ROT_DIM = HEAD_DIM // 4
ROPE_THETA = 500000.0
SSM_WIDTH = 2 * D_MODEL
SSM_HEAD_DIM = 64
SSM_HEADS = SSM_WIDTH // SSM_HEAD_DIM
SSM_GROUPS = 4
SSM_STATE = 128
GROUP_WIDTH = SSM_WIDTH // SSM_GROUPS
BC_WIDTH = SSM_GROUPS * SSM_STATE
CONV_W = 4
CONV_CH = SSM_WIDTH + 2 * BC_WIDTH
EPS = 1e-6
LOG2_E = 1.4426950408889634

LANES = 128
SUBLANES = 8
VMEM_LIMIT = 56 * 1024 * 1024

N_MIXED = CONV_CH // LANES
N_MIXED_Z = SSM_WIDTH // LANES
OFF_MIXED = 0
OFF_GATE_S = OFF_MIXED + 2 * LANES * N_MIXED
OFF_KV = OFF_GATE_S + D_MODEL
OFF_DT = OFF_KV + 2 * KV_WIDTH
OFF_Q = OFF_DT + LANES
OFF_G = OFF_Q + ATT_WIDTH
IN_WIDTH_PADDED = OFF_G + ATT_WIDTH
MM_COLS = 256
assert N_MIXED - N_MIXED_Z == D_MODEL // LANES and MM_COLS == 2 * LANES


def _sigmoid(x):
    return 0.5 * jnp.tanh(0.5 * x) + 0.5


def _silu(x):
    half = 0.5 * x
    return half * jnp.tanh(half) + half


def _softplus(x):
    return jnp.maximum(x, 0.0) + jnp.log1p(jnp.exp(-jnp.abs(x)))


def _compiler_params(semantics):
    return pltpu.CompilerParams(dimension_semantics=semantics, vmem_limit_bytes=VMEM_LIMIT)


def _resident(shape):
    zeros = (0,) * len(shape)
    return pl.BlockSpec(shape, lambda *_: zeros, pipeline_mode=pl.Buffered(1))


def _inproj_kernel(x_ref, nw_ref, w_ref, dtb_ref, q_ref, kv_ref, g_ref, z_ref, xbc_ref, gates_ref, dt_ref):
    x = x_ref[...]
    ms = jnp.mean(x * x, axis=-1, keepdims=True)
    h = (x * lax.rsqrt(ms + EPS) * nw_ref[...]).astype(BF16)

    def section(out_ref, off, width, act):
        step = min(MM_COLS, width)
        for n0 in range(0, width, step):
            r = jnp.dot(h, w_ref[:, off + n0:off + n0 + step], preferred_element_type=F32)
            out_ref[:, n0:n0 + step] = act(r)

    ident = lambda r: r
    section(q_ref, OFF_Q, ATT_WIDTH, ident)
    section(kv_ref, OFF_KV, 2 * KV_WIDTH, ident)
    section(g_ref, OFF_G, ATT_WIDTH, _silu)
    for j in range(N_MIXED):
        r = jnp.dot(h, w_ref[:, OFF_MIXED + j * MM_COLS:OFF_MIXED + (j + 1) * MM_COLS], preferred_element_type=F32)
        xbc_ref[:, j * LANES:(j + 1) * LANES] = r[:, 0:LANES]
        if j < N_MIXED_Z:
            z_ref[:, j * LANES:(j + 1) * LANES] = _silu(r[:, LANES:MM_COLS])
        else:
            gates_ref[:, (j - N_MIXED_Z) * LANES:(j - N_MIXED_Z + 1) * LANES] = _sigmoid(r[:, LANES:MM_COLS])
    for n0 in range(0, D_MODEL, MM_COLS):
        r = jnp.dot(h, w_ref[:, OFF_GATE_S + n0:OFF_GATE_S + n0 + MM_COLS], preferred_element_type=F32)
        gates_ref[:, D_MODEL + n0:D_MODEL + n0 + MM_COLS] = _sigmoid(r)
    lane = lax.broadcasted_iota(jnp.int32, (1, LANES), 1)
    section(dt_ref, OFF_DT, LANES,
            lambda r: jnp.where(lane < SSM_HEADS, _softplus(r + dtb_ref[...]), 0.0))


def _inproj(x2d, nw, w_cat, dtb, tm):
    m = x2d.shape[0]
    widths = (ATT_WIDTH, 2 * KV_WIDTH, ATT_WIDTH, SSM_WIDTH, CONV_CH, 2 * D_MODEL, LANES)
    return pl.pallas_call(
        _inproj_kernel,
        grid=(m // tm,),
        in_specs=[pl.BlockSpec((tm, D_MODEL), lambda i: (i, 0)),
                  _resident((1, D_MODEL)),
                  _resident((D_MODEL, IN_WIDTH_PADDED)),
                  _resident((1, LANES))],
        out_specs=[pl.BlockSpec((tm, w), lambda i: (i, 0)) for w in widths],
        out_shape=[jax.ShapeDtypeStruct((m, w), F32) for w in widths],
        compiler_params=_compiler_params(("arbitrary",)),
        name="inproj",
    )(x2d, nw, w_cat, dtb)


def _attn_kernel(q_ref, kv_ref, g_ref, ck_ref, cv_ref, cos_ref, s1_ref, s2_ref, sink_ref, wo_ref,
                 o_ref, nk_ref, nv_ref, qbuf, kbuf, vbuf, att, *, tb, ch, nblk, mask_first, n_keep):
    blk = pl.program_id(1)

    @pl.when(blk == 0)
    def _():
        kbuf[0:WINDOW, :] = ck_ref[0]
        vbuf[0:WINDOW, :] = cv_ref[0]

    cos = cos_ref[...]
    s1 = s1_ref[...]
    s2 = s2_ref[...]

    def rope(x):
        return x * cos + pltpu.roll(x, LANES - ROT_DIM // 2, 1) * s1 + pltpu.roll(x, ROT_DIM // 2, 1) * s2

    scale = HEAD_DIM ** -0.5
    for s in range(ATT_WIDTH // LANES):
        sl = slice(s * LANES, (s + 1) * LANES)
        qbuf[:, sl] = (rope(q_ref[:, sl]) * scale).astype(BF16)
    for s in range(KV_WIDTH // LANES):
        sl = slice(s * LANES, (s + 1) * LANES)
        kbuf[WINDOW:WINDOW + tb, sl] = rope(kv_ref[:, sl])
    vbuf[WINDOW:WINDOW + tb, :] = kv_ref[:, KV_WIDTH:2 * KV_WIDTH]

    nkeys = WINDOW + ch
    for c in range(tb // ch):
        r0 = c * ch
        if mask_first:
            col_chunk = lax.broadcasted_iota(jnp.int32, (1, nkeys), 1) // CHUNK
            valid = (blk * (tb // ch) + c - WINDOW // CHUNK + col_chunk) >= 0
        for h in range(N_KV_HEADS):
            hs = slice(h * HEAD_DIM, (h + 1) * HEAD_DIM)
            qs = jnp.concatenate(
                [qbuf[r0:r0 + ch, (h * Q_PER_KV + g) * HEAD_DIM:(h * Q_PER_KV + g + 1) * HEAD_DIM]
                 for g in range(Q_PER_KV)], axis=0)
            kb = kbuf[r0:r0 + nkeys, hs].astype(BF16)
            vb = vbuf[r0:r0 + nkeys, hs].astype(BF16)
            s = lax.dot_general(qs, kb, (((1,), (1,)), ((), ())), preferred_element_type=F32)
            if mask_first:
                s = jnp.where(valid, s, -jnp.inf)
            sk = sink_ref[h][:, 0:1]
            m = jnp.maximum(jnp.max(s, axis=-1, keepdims=True), sk)
            p = jnp.exp(s - m)
            denom = jnp.sum(p, axis=-1, keepdims=True) + jnp.exp(sk - m)
            o = jnp.dot(p.astype(BF16), vb, preferred_element_type=F32) / denom
            for g in range(Q_PER_KV):
                head = h * Q_PER_KV + g
                att[r0:r0 + ch, head * HEAD_DIM:(head + 1) * HEAD_DIM] = o[g * ch:(g + 1) * ch, :]

    a = (att[...] * g_ref[...]).astype(BF16)
    o_ref[...] = jnp.dot(a, wo_ref[...], preferred_element_type=F32)

    @pl.when(blk == nblk - 1)
    def _():
        nk_ref[0] = kbuf[WINDOW + tb - n_keep:WINDOW + tb, :]
        nv_ref[0] = vbuf[WINDOW + tb - n_keep:WINDOW + tb, :]

    if nblk > 1:
        kbuf[0:WINDOW, :] = kbuf[tb:tb + WINDOW, :]
        vbuf[0:WINDOW, :] = vbuf[tb:tb + WINDOW, :]


def _attention(q, kv, g, cache_k, cache_v, tables, sink_cols, wo, *, bsz, seq, tb, ch, mask_first):
    nblk = seq // tb
    n_keep = min(WINDOW, seq)
    cos, s1, s2 = tables
    row = lambda b, j: (b * nblk + j, 0)
    kern = functools.partial(_attn_kernel, tb=tb, ch=ch, nblk=nblk, mask_first=mask_first, n_keep=n_keep)
    return pl.pallas_call(
        kern,
        grid=(bsz, nblk),
        in_specs=[pl.BlockSpec((tb, ATT_WIDTH), row),
                  pl.BlockSpec((tb, 2 * KV_WIDTH), row),
                  pl.BlockSpec((tb, ATT_WIDTH), row),
                  pl.BlockSpec((1, WINDOW, KV_WIDTH), lambda b, j: (b, 0, 0)),
                  pl.BlockSpec((1, WINDOW, KV_WIDTH), lambda b, j: (b, 0, 0)),
                  pl.BlockSpec((tb, LANES), lambda b, j: (j, 0)),
                  pl.BlockSpec((tb, LANES), lambda b, j: (j, 0)),
                  pl.BlockSpec((tb, LANES), lambda b, j: (j, 0)),
                  _resident((N_KV_HEADS, Q_PER_KV * ch, LANES)),
                  _resident((ATT_WIDTH, D_MODEL))],
        out_specs=[pl.BlockSpec((tb, D_MODEL), row),
                   pl.BlockSpec((1, n_keep, KV_WIDTH), lambda b, j: (b, 0, 0)),
                   pl.BlockSpec((1, n_keep, KV_WIDTH), lambda b, j: (b, 0, 0))],
        out_shape=[jax.ShapeDtypeStruct((bsz * seq, D_MODEL), F32),
                   jax.ShapeDtypeStruct((bsz, n_keep, KV_WIDTH), F32),
                   jax.ShapeDtypeStruct((bsz, n_keep, KV_WIDTH), F32)],
        scratch_shapes=[pltpu.VMEM((tb, ATT_WIDTH), BF16),
                        pltpu.VMEM((WINDOW + tb, KV_WIDTH), F32),
                        pltpu.VMEM((WINDOW + tb, KV_WIDTH), F32),
                        pltpu.VMEM((tb, ATT_WIDTH), F32)],
        compiler_params=_compiler_params(("arbitrary", "arbitrary")),
        name="attention",
    )(q, kv, g, cache_k, cache_v, cos, s1, s2, sink_cols, wo)


PN_K = OFF_KV
PN_V = PN_K + KV_WIDTH
PT_Q = 0
PT_G = PT_Q + ATT_WIDTH
PT_V = PT_G + ATT_WIDTH
PT_WIDTH = PT_V + KV_WIDTH
HALF_ROT = ROT_DIM // 2
PAIR = 2 * CHUNK
KEY_WIN = WINDOW + PAIR


def _inproj_t_kernel(x_ref, nw_ref, w_ref, dtb_ref, cosq_ref, sinq_ref, cosk_ref, s1k_ref, s2k_ref,
                     cw_ref, cb_ref,
                     qt_ref, gt_ref, vt_ref, k_ref, v_ref, z_ref, xc_ref, gates_ref, dt_ref, ctail_ref,
                     cprev, wt_ref, *, tm, nper):
    step = pl.program_id(0)
    seq_start = step % nper == 0
    cur = step % 2

    @pl.when(step == 0)
    def _():
        for dst, src, width in ((PT_Q, OFF_Q, ATT_WIDTH), (PT_G, OFF_G, ATT_WIDTH), (PT_V, PN_V, KV_WIDTH)):
            for n0 in range(0, width, MM_COLS):
                wt_ref[dst + n0:dst + n0 + MM_COLS, :] = w_ref[:, src + n0:src + n0 + MM_COLS].T

    x = x_ref[...]
    ms = jnp.mean(x * x, axis=-1, keepdims=True)
    h = (x * lax.rsqrt(ms + EPS) * nw_ref[...]).astype(BF16)

    def mm_t(off, rows):
        return lax.dot_general(wt_ref[off:off + rows, :], h, (((1,), (1,)), ((), ())),
                               preferred_element_type=F32)

    cosq = cosq_ref[...]
    sinq = sinq_ref[...]
    scale = HEAD_DIM ** -0.5 * LOG2_E


    def q_tail(n0, r):
        r = r * scale
        for a in range(MM_COLS // HEAD_DIM):
            base = a * HEAD_DIM
            x1 = r[base:base + HALF_ROT]
            x2 = r[base + HALF_ROT:base + ROT_DIM]
            rot = jnp.concatenate([x1 * cosq - x2 * sinq, x2 * cosq + x1 * sinq], axis=0)
            qt_ref[n0 + base:n0 + base + ROT_DIM, :] = rot.astype(BF16)
            qt_ref[n0 + base + ROT_DIM:n0 + base + HEAD_DIM, :] = r[base + ROT_DIM:base + HEAD_DIM].astype(BF16)

    def g_tail(n0, r):
        gt_ref[n0:n0 + MM_COLS, :] = _silu(r).astype(BF16)

    def vt_tail(r):
        vt_ref[...] = r.astype(BF16)

    def mm_n(off, width):
        return jnp.dot(h, w_ref[:, off:off + width], preferred_element_type=F32)

    def nat_tail(out_ref, n0, width, act, r):
        out_ref[:, n0:n0 + width] = act(r).astype(out_ref.dtype)

    def k_tail(k_all):
        for s in range(KV_WIDTH // LANES):
            kk = k_all[:, s * LANES:(s + 1) * LANES]
            k_ref[:, s * LANES:(s + 1) * LANES] = (
                kk * cosk_ref[...] + pltpu.roll(kk, LANES - HALF_ROT, 1) * s1k_ref[...]
                + pltpu.roll(kk, HALF_ROT, 1) * s2k_ref[...])

    row8 = lax.broadcasted_iota(jnp.int32, (SUBLANES, LANES), 0)

    def mixed_tail(j, r2):
        cols = slice(j * LANES, (j + 1) * LANES)
        r = r2[:, 0:LANES]
        prev = jnp.where(seq_start, 0.0, cprev[cur, :, cols])
        acc = cb_ref[:, cols] + r * cw_ref[CONV_W - 1:CONV_W, cols]
        for k in range(1, CONV_W):
            rolled = pltpu.roll(r, k, 0)
            top = jnp.where(row8 < k, pltpu.roll(prev, k, 0), rolled[0:SUBLANES])
            shifted = jnp.concatenate([top, rolled[SUBLANES:tm]], axis=0)
            acc = acc + shifted * cw_ref[CONV_W - 1 - k:CONV_W - k, cols]
        xc_ref[:, cols] = _silu(acc).astype(xc_ref.dtype)
        cprev[1 - cur, :, cols] = r[tm - SUBLANES:tm]
        ctail_ref[0, :, cols] = r[tm - SUBLANES:tm]
        other = r2[:, LANES:MM_COLS]
        if j < N_MIXED_Z:
            z_ref[:, cols] = _silu(other).astype(z_ref.dtype)
        else:
            gates_ref[:, (j - N_MIXED_Z) * LANES:(j - N_MIXED_Z + 1) * LANES] = _sigmoid(other).astype(
                gates_ref.dtype)

    def gate_s_tail(n0, r):
        gates_ref[:, D_MODEL + n0:D_MODEL + n0 + MM_COLS] = _sigmoid(r).astype(gates_ref.dtype)

    lane = lax.broadcasted_iota(jnp.int32, (1, LANES), 1)
    ident = lambda r: r
    dt_act = lambda r: jnp.where(lane < SSM_HEADS, _softplus(r + dtb_ref[...]), 0.0)
    P = functools.partial
    mixed = [(P(mm_n, OFF_MIXED + j * MM_COLS, MM_COLS), P(mixed_tail, j)) for j in range(N_MIXED)]
    rest = ([(P(mm_t, PT_Q + n0, MM_COLS), P(q_tail, n0)) for n0 in range(0, ATT_WIDTH, MM_COLS)]
            + [(P(mm_t, PT_G + n0, MM_COLS), P(g_tail, n0)) for n0 in range(0, ATT_WIDTH, MM_COLS)]
            + [(P(mm_n, OFF_GATE_S + n0, MM_COLS), P(gate_s_tail, n0)) for n0 in range(0, D_MODEL, MM_COLS)]
            + [(P(mm_t, PT_V, KV_WIDTH), vt_tail),
               (P(mm_n, PN_K, KV_WIDTH), k_tail),
               (P(mm_n, PN_V, KV_WIDTH), P(nat_tail, v_ref, 0, KV_WIDTH, ident)),
               (P(mm_n, OFF_DT, LANES), P(nat_tail, dt_ref, 0, LANES, dt_act))])
    order = []
    for idx in range(max(len(mixed), len(rest))):
        order += mixed[idx:idx + 1] + rest[idx:idx + 1]
    pending = order[0][0]()
    for idx, (_, tail) in enumerate(order):
        nxt = order[idx + 1][0]() if idx + 1 < len(order) else None
        tail(pending)
        pending = nxt


def _inproj_t(x2d, nw, w_cat, dtb, qtabs, ktabs, cw, cb, seq, tm):
    m = x2d.shape[0]
    nper = seq // tm
    row = lambda i: (i, 0)
    col = lambda i: (0, i)
    nat_widths = (KV_WIDTH, KV_WIDTH, SSM_WIDTH, CONV_CH, 2 * D_MODEL, LANES)
    nat_dtypes = (F32, F32, BF16, BF16, BF16, F32)
    return pl.pallas_call(
        functools.partial(_inproj_t_kernel, tm=tm, nper=nper),
        grid=(m // tm,),
        in_specs=[pl.BlockSpec((tm, D_MODEL), row),
                  _resident((1, D_MODEL)),
                  _resident((D_MODEL, IN_WIDTH_PADDED)),
                  _resident((1, LANES)),
                  pl.BlockSpec((HALF_ROT, tm), lambda i: (0, i % nper)),
                  pl.BlockSpec((HALF_ROT, tm), lambda i: (0, i % nper)),
                  pl.BlockSpec((tm, LANES), lambda i: (i % nper, 0)),
                  pl.BlockSpec((tm, LANES), lambda i: (i % nper, 0)),
                  pl.BlockSpec((tm, LANES), lambda i: (i % nper, 0)),
                  _resident((CONV_W, CONV_CH)),
                  _resident((1, CONV_CH))],
        out_specs=[pl.BlockSpec((ATT_WIDTH, tm), col),
                   pl.BlockSpec((ATT_WIDTH, tm), col),
                   pl.BlockSpec((KV_WIDTH, tm), col)]
                  + [pl.BlockSpec((tm, w), row) for w in nat_widths]
                  + [pl.BlockSpec((1, SUBLANES, CONV_CH), lambda i: (i // nper, 0, 0))],
        out_shape=[jax.ShapeDtypeStruct((ATT_WIDTH, m), BF16),
                   jax.ShapeDtypeStruct((ATT_WIDTH, m), BF16),
                   jax.ShapeDtypeStruct((KV_WIDTH, m), BF16)]
                  + [jax.ShapeDtypeStruct((m, w), d) for w, d in zip(nat_widths, nat_dtypes)]
                  + [jax.ShapeDtypeStruct((m // seq, SUBLANES, CONV_CH), F32)],
        scratch_shapes=[pltpu.VMEM((2, SUBLANES, CONV_CH), F32), pltpu.VMEM((PT_WIDTH, D_MODEL), BF16)],
        compiler_params=_compiler_params(("arbitrary",)),
        name="inproj_t",
    )(x2d, nw, w_cat, dtb, *qtabs, *ktabs, cw, cb)


def _attn_t_kernel(qt_ref, k_ref, vt_ref, gt_ref, bias_ref, sink_ref, wo_ref, o_ref, kbuf, vtbuf, att_t, *, tb):
    blk = pl.program_id(1)
    cur = blk % 2
    prv = 1 - cur

    @pl.when(blk == 0)
    def _():
        kbuf[1, tb - WINDOW:tb, :] = jnp.zeros((WINDOW, KV_WIDTH), BF16)
        vtbuf[1, :, tb - WINDOW:tb] = jnp.zeros((KV_WIDTH, WINDOW), BF16)

    kbuf[cur] = k_ref[...].astype(BF16)
    vtbuf[cur] = vt_ref[...]
    seq_start = jnp.where(blk == 0, -jnp.inf, 0.0)

    batch_dims = (((2,), (1,)), ((0,), (0,)))
    for pr in range(tb // PAIR):
        c0 = pr * PAIR
        k_wins, vt_wins, qs = [], [], []
        for h in range(N_KV_HEADS):
            hs = slice(h * HEAD_DIM, (h + 1) * HEAD_DIM)
            if pr == 0:
                k_wins.append(jnp.concatenate([kbuf[prv, tb - WINDOW:tb, hs], kbuf[cur, 0:PAIR, hs]], axis=0))
                vt_wins.append(jnp.concatenate([vtbuf[prv, hs, tb - WINDOW:tb], vtbuf[cur, hs, 0:PAIR]], axis=1))
            else:
                k_wins.append(kbuf[cur, c0 - WINDOW:c0 + PAIR, hs])
                vt_wins.append(vtbuf[cur, hs, c0 - WINDOW:c0 + PAIR])
            qs.append(jnp.concatenate(
                [qt_ref[(h * Q_PER_KV + g) * HEAD_DIM:(h * Q_PER_KV + g + 1) * HEAD_DIM, c0:c0 + PAIR]
                 for g in range(Q_PER_KV)], axis=1))
        s = lax.dot_general(jnp.stack(k_wins), jnp.stack(qs), batch_dims,
                            preferred_element_type=F32)
        lo = bias_ref[0:CHUNK] + seq_start if pr == 0 else bias_ref[0:CHUNK]
        mid = [s[:, CHUNK:WINDOW] + seq_start, s[:, WINDOW:WINDOW + CHUNK]] if pr == 0 else [s[:, CHUNK:WINDOW + CHUNK]]
        s = jnp.concatenate([s[:, 0:CHUNK] + lo[None]] + mid
                            + [s[:, WINDOW + CHUNK:KEY_WIN] + bias_ref[WINDOW + CHUNK:KEY_WIN][None]], axis=1)
        sk = sink_ref[...][:, None, :]
        m = jnp.maximum(jnp.max(s, axis=1, keepdims=True), sk)
        p = jnp.exp2(s - m)
        denom = jnp.sum(p, axis=1, keepdims=True) + jnp.exp2(sk - m)
        o = lax.dot_general(jnp.stack(vt_wins), p.astype(BF16), batch_dims, preferred_element_type=F32) / denom
        for h in range(N_KV_HEADS):
            for g in range(Q_PER_KV):
                rows = slice((h * Q_PER_KV + g) * HEAD_DIM, (h * Q_PER_KV + g + 1) * HEAD_DIM)
                att_t[rows, c0:c0 + PAIR] = (
                    o[h, :, g * PAIR:(g + 1) * PAIR] * gt_ref[rows, c0:c0 + PAIR].astype(F32)).astype(BF16)

    o_ref[...] = lax.dot_general(att_t[...], wo_ref[...], (((0,), (0,)), ((), ())), preferred_element_type=F32)


def _attention_t(qt, k, vt, gt, bias, sink_rows, wo, *, bsz, seq, tb):
    nblk = seq // tb
    row = lambda b, j: (b * nblk + j, 0)
    col = lambda b, j: (0, b * nblk + j)
    return pl.pallas_call(
        functools.partial(_attn_t_kernel, tb=tb),
        grid=(bsz, nblk),
        in_specs=[pl.BlockSpec((ATT_WIDTH, tb), col),
                  pl.BlockSpec((tb, KV_WIDTH), row),
                  pl.BlockSpec((KV_WIDTH, tb), col),
                  pl.BlockSpec((ATT_WIDTH, tb), col),
                  _resident((KEY_WIN, Q_PER_KV * PAIR)),
                  _resident((N_KV_HEADS, Q_PER_KV * PAIR)),
                  _resident((ATT_WIDTH, D_MODEL))],
        out_specs=pl.BlockSpec((tb, D_MODEL), row),
        out_shape=jax.ShapeDtypeStruct((bsz * seq, D_MODEL), F32),
        scratch_shapes=[pltpu.VMEM((2, tb, KV_WIDTH), BF16),
                        pltpu.VMEM((2, KV_WIDTH, tb), BF16),
                        pltpu.VMEM((ATT_WIDTH, tb), BF16)],
        compiler_params=_compiler_params(("arbitrary", "arbitrary")),
        name="attention_t",
    )(qt, k, vt, gt, bias, sink_rows, wo)


def _window_bias():
    key_chunk = np.arange(KEY_WIN)[:, None] // CHUNK
    q_chunk = (np.arange(Q_PER_KV * PAIR)[None, :] % PAIR) // CHUNK
    ok = (key_chunk >= q_chunk) & (key_chunk <= q_chunk + WINDOW // CHUNK)
    return jnp.asarray(np.where(ok, 0.0, -np.inf), F32)


SSD_Q = 64
PIECE = 32
QUAD = 4 * SSM_HEAD_DIM


def _split3(x):
    hi = x.astype(BF16).astype(F32)
    r1 = x - hi
    mid = r1.astype(BF16).astype(F32)
    lo = (r1 - mid).astype(BF16).astype(F32)
    return (hi + pltpu.roll(mid, PIECE, 1) + pltpu.roll(lo, 2 * PIECE, 1)).astype(BF16)


def _ssd_kernel(*refs, tc, out_row, conv_in_kernel):
    if conv_in_kernel:
        (xbc_ref, z_ref, dt_ref, x_ref, ba_ref, gates_ref, h0_ref, cs0_ref, cw_ref, cb_ref,
         a_ref, e3_ref, dx_ref, wso_ref, wout_ref, pnw_ref, eye_ref, tril_ref, qmask_ref,
         y_ref, nssm_ref, nconv_ref, ht, ybuf, dtx_buf, cumx_buf, cb_buf, xpad, xc) = refs
    else:
        (xc, z_ref, dt_ref, x_ref, ba_ref, gates_ref, h0_ref,
         a_ref, e3_ref, dx_ref, wso_ref, wout_ref, pnw_ref, eye_ref, tril_ref, qmask_ref,
         y_ref, nssm_ref, ht, ybuf, dtx_buf, cumx_buf, cb_buf) = refs
    blk = pl.program_id(1)
    nblk = pl.num_programs(1)

    @pl.when(blk == 0)
    def _():
        ht[...] = h0_ref[0].T

    if conv_in_kernel:
        pad0 = SUBLANES - (CONV_W - 1)
        xpad[pad0:SUBLANES, :] = cs0_ref[0]
        xpad[SUBLANES:SUBLANES + tc, :] = xbc_ref[...]
        acc = cb_ref[...] + xpad[pad0:pad0 + tc, :] * cw_ref[0:1, :]
        for tap in range(1, CONV_W):
            acc = acc + xpad[pad0 + tap:pad0 + tap + tc, :] * cw_ref[tap:tap + 1, :]
        xc[...] = _silu(acc)
        nconv_ref[0] = xpad[SUBLANES + out_row - (CONV_W - 1):SUBLANES + out_row, :]

    n_chunks = tc // SSD_Q
    lane = lax.broadcasted_iota(jnp.int32, (tc, LANES), 1)
    ri = lax.broadcasted_iota(jnp.int32, (n_chunks, SSD_Q, SSD_Q), 1)
    rj = lax.broadcasted_iota(jnp.int32, (n_chunks, SSD_Q, SSD_Q), 2)
    tri = (ri >= rj).astype(BF16)
    dx = dx_ref[...]

    def expand(v):
        return jnp.dot(_split3(v), e3_ref[...], preferred_element_type=F32)

    dt_all = dt_ref[...]
    da3 = _split3(dt_all * a_ref[...]).reshape(n_chunks, SSD_Q, LANES)
    cum3 = lax.dot_general(tri, da3, (((2,), (1,)), ((0,), (0,))),
                           preferred_element_type=F32).reshape(tc, LANES)
    cum_all = jnp.where(lane < PIECE,
                        cum3 + pltpu.roll(cum3, LANES - PIECE, 1) + pltpu.roll(cum3, LANES - 2 * PIECE, 1), 0.0)
    dtx_buf[...] = expand(dt_all)
    cumx_buf[...] = expand(cum_all)
    c_parts, b_parts = [], []
    for c in range(n_chunks):
        rws = slice(c * SSD_Q, (c + 1) * SSD_Q)
        for g in range(SSM_GROUPS):
            bg = xc[rws, SSM_WIDTH + g * SSM_STATE:SSM_WIDTH + (g + 1) * SSM_STATE].astype(BF16)
            c_parts.append(xc[rws, SSM_WIDTH + BC_WIDTH + g * SSM_STATE:
                              SSM_WIDTH + BC_WIDTH + (g + 1) * SSM_STATE].astype(BF16))
            b_parts.append(jnp.concatenate([bg] * 4, axis=0))
    cb_buf[...] = lax.dot_general(jnp.stack(c_parts), jnp.stack(b_parts), (((2,), (2,)), ((0,), (0,))),
                                  preferred_element_type=F32)

    def chunk(c, carry):
        r0 = pl.multiple_of(c * SSD_Q, SSD_Q)
        rows = pl.ds(r0, SSD_Q)
        xs = xc[rows, 0:SSM_WIDTH].astype(F32)
        bm = xc[rows, SSM_WIDTH:SSM_WIDTH + BC_WIDTH].astype(BF16)
        cm = xc[rows, SSM_WIDTH + BC_WIDTH:CONV_CH].astype(BF16)
        dt_x = dtx_buf[rows, :]
        cum_x = cumx_buf[rows, :]
        ecum_x = jnp.exp2(cum_x)
        cum_last = cum_x[SSD_Q - 1:SSD_Q, :]
        xdt = xs * dt_x
        xdt_b = xdt.astype(BF16)
        xw = (xdt * jnp.exp2(cum_last - cum_x)).astype(BF16)
        cum_j = jnp.sum(cum_x * eye_ref[...], axis=0, keepdims=True)
        decay = jnp.exp2(cum_x - cum_j + tril_ref[...])

        y_parts = []
        for g in range(SSM_GROUPS):
            cg = cm[:, g * SSM_STATE:(g + 1) * SSM_STATE]
            bg = bm[:, g * SSM_STATE:(g + 1) * SSM_STATE]
            cb4 = cb_buf[c * SSM_GROUPS + g]
            gs = slice(g * GROUP_WIDTH, (g + 1) * GROUP_WIDTH)
            y_state = jnp.dot(cg, ht[:, gs].astype(BF16), preferred_element_type=F32) * ecum_x[:, gs]
            for half in range(GROUP_WIDTH // QUAD):
                qs = slice(g * GROUP_WIDTH + half * QUAD, g * GROUP_WIDTH + (half + 1) * QUAD)
                m4 = (cb4 * decay[:, qs]).astype(BF16)
                x4 = xdt_b[:, qs]
                bd = jnp.concatenate([x4 * qmask_ref[a:a + 1, :] for a in range(4)], axis=0)
                y_parts.append(jnp.dot(m4, bd, preferred_element_type=F32)
                               + y_state[:, half * QUAD:(half + 1) * QUAD])
            ht[:, gs] = (ht[:, gs] * ecum_x[SSD_Q - 1:SSD_Q, gs]
                         + jnp.dot(bg.T, xw[:, gs], preferred_element_type=F32))
        y = jnp.concatenate(y_parts, axis=1) + xs * dx
        yz = y * z_ref[rows, :].astype(F32)
        for g in range(SSM_GROUPS):
            gs = slice(g * GROUP_WIDTH, (g + 1) * GROUP_WIDTH)
            yg = yz[:, gs]
            ms = jnp.mean(yg * yg, axis=-1, keepdims=True)
            ybuf[rows, gs] = (yg * lax.rsqrt(ms + EPS)).astype(BF16)
        return carry

    lax.fori_loop(0, n_chunks, chunk, 0, unroll=min(4, n_chunks))
    branch_s = jnp.dot(ybuf[...], wso_ref[...], preferred_element_type=F32)
    merged = (gates_ref[:, 0:D_MODEL].astype(F32) * ba_ref[...]
              + gates_ref[:, D_MODEL:2 * D_MODEL].astype(F32) * branch_s)
    out = jnp.dot(merged.astype(BF16), wout_ref[...], preferred_element_type=F32)
    ms = jnp.mean(out * out, axis=-1, keepdims=True)
    y_ref[...] = x_ref[...] + out * lax.rsqrt(ms + EPS) * pnw_ref[...]

    @pl.when(blk == nblk - 1)
    def _():
        nssm_ref[0] = ht[...].T


def _ssd_masks():
    i = np.arange(SSD_Q)[:, None]
    j = np.arange(SSM_WIDTH)[None, :] % SSD_Q
    eye = jnp.asarray(i == j, F32)
    tril = jnp.asarray(np.where(i >= j, 0.0, -np.inf), F32)
    qmask = jnp.asarray(np.arange(QUAD)[None, :] // SSM_HEAD_DIM == np.arange(4)[:, None], BF16)
    return eye, tril, qmask


def _ssd(xc, zs, dt, x2d, branch_a, gates, ssm0, conv0, wts, *, bsz, seq, tc, l_real, conv_in_kernel):
    nblk = seq // tc
    assert not conv_in_kernel or nblk == 1
    row = lambda b, j: (b * nblk + j, 0)
    per_seq = lambda b, j: (b, 0, 0)
    hp = SSM_HEADS * SSM_HEAD_DIM
    kern = functools.partial(_ssd_kernel, tc=tc, out_row=l_real, conv_in_kernel=conv_in_kernel)
    operands = [xc, zs, dt, x2d, branch_a, gates, ssm0]
    in_specs = [pl.BlockSpec((tc, CONV_CH), row),
                pl.BlockSpec((tc, SSM_WIDTH), row),
                pl.BlockSpec((tc, LANES), row),
                pl.BlockSpec((tc, D_MODEL), row),
                pl.BlockSpec((tc, D_MODEL), row),
                pl.BlockSpec((tc, 2 * D_MODEL), row),
                pl.BlockSpec((1, hp, SSM_STATE), per_seq)]
    out_specs = [pl.BlockSpec((tc, D_MODEL), row), pl.BlockSpec((1, hp, SSM_STATE), per_seq)]
    out_shape = [jax.ShapeDtypeStruct((bsz * seq, D_MODEL), F32),
                 jax.ShapeDtypeStruct((bsz, hp, SSM_STATE), F32)]
    scratch = [pltpu.VMEM((SSM_STATE, SSM_WIDTH), F32), pltpu.VMEM((tc, SSM_WIDTH), BF16),
               pltpu.VMEM((tc, SSM_WIDTH), F32), pltpu.VMEM((tc, SSM_WIDTH), F32),
               pltpu.VMEM((tc // SSD_Q * SSM_GROUPS, SSD_Q, QUAD), F32)]
    if conv_in_kernel:
        operands += [conv0, wts["conv_w"], wts["conv_b"]]
        in_specs += [pl.BlockSpec((1, CONV_W - 1, CONV_CH), per_seq),
                     _resident((CONV_W, CONV_CH)), _resident((1, CONV_CH))]
        out_specs.append(pl.BlockSpec((1, CONV_W - 1, CONV_CH), per_seq))
        out_shape.append(jax.ShapeDtypeStruct((bsz, CONV_W - 1, CONV_CH), F32))
        scratch += [pltpu.VMEM((SUBLANES + tc, CONV_CH), F32), pltpu.VMEM((tc, CONV_CH), F32)]
    consts = [wts["a_row"], wts["e3"], wts["d_x"], wts["w_ssm_o"], wts["w_out"],
              wts["post_norm_w"], *_ssd_masks()]
    operands += consts
    in_specs += [_resident(c.shape) for c in consts]
    return pl.pallas_call(
        kern,
        grid=(bsz, nblk),
        in_specs=in_specs,
        out_specs=out_specs,
        out_shape=out_shape,
        scratch_shapes=scratch,
        compiler_params=_compiler_params(("arbitrary", "arbitrary")),
        name="ssd",
    )(*operands)


def _rope_angles(pos):
    half = ROT_DIM // 2
    inv = ROPE_THETA ** (-(jnp.arange(half, dtype=F32) * 2.0 / ROT_DIM))
    ang = pos.astype(F32)[:, None] * inv[None, :]
    return jnp.cos(ang), jnp.sin(ang)


def _rope_tables(pos):
    cos, sin = _rope_angles(pos)
    half = ROT_DIM // 2
    n = pos.shape[0]
    rest = HEAD_DIM - ROT_DIM
    c = jnp.concatenate([cos, cos, jnp.ones((n, rest), F32)], axis=1)
    s1 = jnp.concatenate([-sin, jnp.zeros((n, half + rest), F32)], axis=1)
    s2 = jnp.concatenate([jnp.zeros((n, half), F32), sin, jnp.zeros((n, rest), F32)], axis=1)
    rep = LANES // HEAD_DIM
    return tuple(jnp.tile(t, (1, rep)) for t in (c, s1, s2))


def _layer_prompt(x, pos, wts, *, tm, tb, tc):
    bsz, seq, _ = x.shape
    x2d = x.reshape(bsz * seq, D_MODEL)
    cos, sin = _rope_angles(pos)
    qt, gt, vt, k, v, zs, xc, gates, dt, conv_tail = _inproj_t(
        x2d, wts["pre_norm_w"], wts["w_cat"], wts["dt_bias"], (cos.T, sin.T), _rope_tables(pos),
        wts["conv_w"], wts["conv_b"], seq, tm)
    sink_rows = jnp.repeat(wts["sinks"].reshape(N_KV_HEADS, Q_PER_KV), PAIR, axis=1) * LOG2_E
    branch_a = _attention_t(qt, k, vt, gt, _window_bias(), sink_rows, wts["w_attn_o"], bsz=bsz, seq=seq, tb=tb)
    n_keep = min(WINDOW, seq)
    keep = lambda t: t.reshape(bsz, seq, KV_WIDTH)[:, seq - n_keep:].reshape(bsz, n_keep, N_KV_HEADS, HEAD_DIM)
    zero_ssm = jnp.zeros((bsz, SSM_HEADS * SSM_HEAD_DIM, SSM_STATE), F32)
    y, new_ssm = _ssd(xc, zs, dt, x2d, branch_a, gates, zero_ssm, None, wts,
                      bsz=bsz, seq=seq, tc=tc, l_real=seq, conv_in_kernel=False)
    return (y.reshape(bsz, seq, D_MODEL), keep(k), keep(v), conv_tail[:, SUBLANES - (CONV_W - 1):],
            new_ssm.reshape(bsz, SSM_HEADS, SSM_HEAD_DIM, SSM_STATE))


def _layer_sample(x, pos, cache_k, cache_v, conv0, ssm0, wts, *, tm):
    bsz, seq, _ = x.shape
    x2d = x.reshape(bsz * seq, D_MODEL)
    q, kv, g, zs, xbc, gates, dt = _inproj(x2d, wts["pre_norm_w"], wts["w_cat"], wts["dt_bias"], tm)
    sink_cols = jnp.broadcast_to(
        jnp.repeat(wts["sinks"].reshape(N_KV_HEADS, Q_PER_KV), seq, axis=1)[:, :, None],
        (N_KV_HEADS, Q_PER_KV * seq, LANES))
    branch_a, new_k, new_v = _attention(
        q, kv, g, cache_k, cache_v, _rope_tables(pos), sink_cols, wts["w_attn_o"],
        bsz=bsz, seq=seq, tb=seq, ch=seq, mask_first=False)
    seq_p = -(-seq // SSD_Q) * SSD_Q
    padrows = lambda t: jnp.pad(t.reshape(bsz, seq, -1), ((0, 0), (0, seq_p - seq), (0, 0))).reshape(
        bsz * seq_p, -1)
    y, new_ssm, new_conv = _ssd(
        padrows(xbc), padrows(zs), padrows(dt), padrows(x2d), padrows(branch_a), padrows(gates), ssm0, conv0, wts,
        bsz=bsz, seq=seq_p, tc=seq_p, l_real=seq, conv_in_kernel=True)
    n_keep = new_k.shape[1]
    return (y.reshape(bsz, seq_p, D_MODEL)[:, :seq],
            new_k.reshape(bsz, n_keep, N_KV_HEADS, HEAD_DIM), new_v.reshape(bsz, n_keep, N_KV_HEADS, HEAD_DIM),
            new_conv, new_ssm.reshape(bsz, SSM_HEADS, SSM_HEAD_DIM, SSM_STATE))


def _prep_weights(pre_norm_w, w_in, conv_w, conv_b, dt_bias, a_log, d_skip, sinks, ssm_norm_w, w_attn_o,
                  w_ssm_o, w_out, post_norm_w):
    g_off = ATT_WIDTH + 2 * KV_WIDTH
    z_off = g_off + ATT_WIDTH
    dt_off = z_off + SSM_WIDTH + CONV_CH
    xbc_off = z_off + SSM_WIDTH
    ga_off = dt_off + SSM_HEADS
    gs_off = ga_off + D_MODEL
    tile = lambda off, j: w_in[:, off + j * LANES:off + (j + 1) * LANES]
    mixed = []
    for j in range(N_MIXED):
        mixed += [tile(xbc_off, j), tile(z_off, j) if j < N_MIXED_Z else tile(ga_off, j - N_MIXED_Z)]
    w_cat = jnp.concatenate(
        mixed + [w_in[:, gs_off:], w_in[:, ATT_WIDTH:g_off],
         w_in[:, dt_off:dt_off + SSM_HEADS], jnp.zeros((D_MODEL, LANES - SSM_HEADS), w_in.dtype),
         w_in[:, :ATT_WIDTH], w_in[:, g_off:z_off]], axis=1).astype(BF16)
    pad_heads = lambda v: jnp.pad(v.astype(F32), (0, LANES - SSM_HEADS)).reshape(1, LANES)
    k_idx = np.arange(LANES)[:, None]
    c_idx = np.arange(SSM_WIDTH)[None, :]
    e3 = jnp.asarray((k_idx < 3 * PIECE) & (k_idx % PIECE == c_idx // SSM_HEAD_DIM), BF16)
    return dict(
        pre_norm_w=pre_norm_w.reshape(1, D_MODEL), w_cat=w_cat, dt_bias=pad_heads(dt_bias),
        conv_w=conv_w, conv_b=conv_b.reshape(1, CONV_CH), a_row=pad_heads(-jnp.exp(a_log.astype(F32)) * LOG2_E),
        e3=e3, d_x=jnp.repeat(d_skip.astype(F32), SSM_HEAD_DIM).reshape(1, SSM_WIDTH), sinks=sinks.astype(F32),
        w_attn_o=w_attn_o.astype(BF16),
        w_ssm_o=(ssm_norm_w.astype(F32)[:, None] * w_ssm_o).astype(BF16),
        w_out=w_out.astype(BF16), post_norm_w=post_norm_w.reshape(1, D_MODEL))


PROMPT_IN_ROWS = 512
PROMPT_ATT_ROWS = 1024
PROMPT_SSD_ROWS = 512
SAMPLE_IN_ROWS = 256


def _largest_tile(rows, cap):
    for unit in (PAIR, CHUNK, SUBLANES):
        best = max((t for t in range(unit, min(rows, cap) + 1, unit) if rows % t == 0), default=0)
        if best:
            return best
    raise ValueError(f"no row tile for {rows} rows")


def _prompt_tiles(seq):
    return dict(tm=_largest_tile(seq, PROMPT_IN_ROWS), tb=_largest_tile(seq, PROMPT_ATT_ROWS),
                tc=_largest_tile(seq, PROMPT_SSD_ROWS))


def kernel(x_prompt, x_sample, cache_k, cache_v, state_conv, state_ssm, pre_norm_w, w_in, conv_w, conv_b, dt_bias,
           a_log, d_skip, sinks, ssm_norm_w, w_attn_o, w_ssm_o, w_out, post_norm_w):
    depth = w_in.shape[0]
    lp = x_prompt.shape[1]
    bs, ls, _ = x_sample.shape
    pos_p = jnp.arange(lp, dtype=F32)
    pos_s = PAST_LEN + jnp.arange(ls, dtype=F32)
    hp = SSM_HEADS * SSM_HEAD_DIM
    yp, ys = x_prompt, x_sample
    outs = [[] for _ in range(8)]
    for layer in range(depth):
        wts = _prep_weights(pre_norm_w[layer], w_in[layer], conv_w[layer], conv_b[layer], dt_bias[layer],
                            a_log[layer], d_skip[layer], sinks[layer], ssm_norm_w[layer], w_attn_o[layer],
                            w_ssm_o[layer], w_out[layer], post_norm_w[layer])
        yp, kp, vp, cp, sp = _layer_prompt(yp, pos_p, wts, **_prompt_tiles(lp))
        ys, ks, vs, cs, ss = _layer_sample(ys, pos_s,
                                           cache_k[layer].reshape(bs, WINDOW, KV_WIDTH),
                                           cache_v[layer].reshape(bs, WINDOW, KV_WIDTH),
                                           state_conv[layer], state_ssm[layer].reshape(bs, hp, SSM_STATE), wts,
                                           tm=_largest_tile(bs * ls, SAMPLE_IN_ROWS))
        for lst, val in zip(outs, (kp, vp, cp, sp, ks, vs, cs, ss)):
            lst.append(val)
    return (yp, ys) + tuple(jnp.stack(lst) for lst in outs)
```

```python
import functools

import jax
import jax.numpy as jnp
import numpy as np
from jax import lax
from jax.experimental import pallas as pl
from jax.experimental.pallas import tpu as pltpu

F32 = jnp.float32
BF16 = jnp.bfloat16

D_MODEL = 1024
PAST_LEN = 2048
CHUNK = 64
N_Q_HEADS = 16
N_KV_HEADS = 4
HEAD_DIM = 64
Q_PER_KV = N_Q_HEADS // N_KV_HEADS
ATT_WIDTH = N_Q_HEADS * HEAD_DIM
KV_WIDTH = N_KV_HEADS * HEAD_DIM
WINDOW = 128
ROT_DIM = HEAD_DIM // 4
ROPE_THETA = 500000.0
SSM_WIDTH = 2 * D_MODEL
SSM_HEAD_DIM = 64
SSM_HEADS = SSM_WIDTH // SSM_HEAD_DIM
SSM_GROUPS = 4
SSM_STATE = 128
GROUP_WIDTH = SSM_WIDTH // SSM_GROUPS
BC_WIDTH = SSM_GROUPS * SSM_STATE
CONV_W = 4
CONV_CH = SSM_WIDTH + 2 * BC_WIDTH
EPS = 1e-6
LOG2_E = 1.4426950408889634

LANES = 128
SUBLANES = 8
VMEM_LIMIT = 58 * 1024 * 1024

N_MIXED = CONV_CH // LANES
N_MIXED_Z = SSM_WIDTH // LANES
OFF_MIXED = 0
OFF_GATE_S = OFF_MIXED + 2 * LANES * N_MIXED
OFF_KV = OFF_GATE_S + D_MODEL
OFF_DT = OFF_KV + 2 * KV_WIDTH
OFF_Q = OFF_DT + LANES
OFF_G = OFF_Q + ATT_WIDTH
IN_WIDTH_PADDED = OFF_G + ATT_WIDTH
MM_COLS = 256
assert N_MIXED - N_MIXED_Z == D_MODEL // LANES and MM_COLS == 2 * LANES


def _sigmoid(x):
    return 0.5 * jnp.tanh(0.5 * x) + 0.5


def _silu(x):
    half = 0.5 * x
    return half * jnp.tanh(half) + half


def _softplus(x):
    return jnp.maximum(x, 0.0) + jnp.log1p(jnp.exp(-jnp.abs(x)))


def _compiler_params(semantics):
    return pltpu.CompilerParams(dimension_semantics=semantics, vmem_limit_bytes=VMEM_LIMIT)


def _resident(shape):
    zeros = (0,) * len(shape)
    return pl.BlockSpec(shape, lambda *_: zeros, pipeline_mode=pl.Buffered(1))


def _inproj_kernel(x_ref, nw_ref, w_ref, dtb_ref, q_ref, kv_ref, g_ref, z_ref, xbc_ref, gates_ref, dt_ref):
    x = x_ref[...]
    ms = jnp.mean(x * x, axis=-1, keepdims=True)
    h = (x * lax.rsqrt(ms + EPS) * nw_ref[...]).astype(BF16)

    def section(out_ref, off, width, act):
        step = min(MM_COLS, width)
        for n0 in range(0, width, step):
            r = jnp.dot(h, w_ref[:, off + n0:off + n0 + step], preferred_element_type=F32)
            out_ref[:, n0:n0 + step] = act(r)

    ident = lambda r: r
    section(q_ref, OFF_Q, ATT_WIDTH, ident)
    section(kv_ref, OFF_KV, 2 * KV_WIDTH, ident)
    section(g_ref, OFF_G, ATT_WIDTH, _silu)
    for j in range(N_MIXED):
        r = jnp.dot(h, w_ref[:, OFF_MIXED + j * MM_COLS:OFF_MIXED + (j + 1) * MM_COLS], preferred_element_type=F32)
        xbc_ref[:, j * LANES:(j + 1) * LANES] = r[:, 0:LANES]
        if j < N_MIXED_Z:
            z_ref[:, j * LANES:(j + 1) * LANES] = _silu(r[:, LANES:MM_COLS])
        else:
            gates_ref[:, (j - N_MIXED_Z) * LANES:(j - N_MIXED_Z + 1) * LANES] = _sigmoid(r[:, LANES:MM_COLS])
    for n0 in range(0, D_MODEL, MM_COLS):
        r = jnp.dot(h, w_ref[:, OFF_GATE_S + n0:OFF_GATE_S + n0 + MM_COLS], preferred_element_type=F32)
        gates_ref[:, D_MODEL + n0:D_MODEL + n0 + MM_COLS] = _sigmoid(r)
    lane = lax.broadcasted_iota(jnp.int32, (1, LANES), 1)
    section(dt_ref, OFF_DT, LANES,
            lambda r: jnp.where(lane < SSM_HEADS, _softplus(r + dtb_ref[...]), 0.0))


def _inproj(x2d, nw, w_cat, dtb, tm):
    m = x2d.shape[0]
    widths = (ATT_WIDTH, 2 * KV_WIDTH, ATT_WIDTH, SSM_WIDTH, CONV_CH, 2 * D_MODEL, LANES)
    return pl.pallas_call(
        _inproj_kernel,
        grid=(m // tm,),
        in_specs=[pl.BlockSpec((tm, D_MODEL), lambda i: (i, 0)),
                  _resident((1, D_MODEL)),
                  _resident((D_MODEL, IN_WIDTH_PADDED)),
                  _resident((1, LANES))],
        out_specs=[pl.BlockSpec((tm, w), lambda i: (i, 0)) for w in widths],
        out_shape=[jax.ShapeDtypeStruct((m, w), F32) for w in widths],
        compiler_params=_compiler_params(("arbitrary",)),
        name="inproj",
    )(x2d, nw, w_cat, dtb)


def _attn_kernel(q_ref, kv_ref, g_ref, ck_ref, cv_ref, cos_ref, s1_ref, s2_ref, sink_ref, wo_ref,
                 o_ref, nk_ref, nv_ref, qbuf, kbuf, vbuf, att, *, tb, ch, nblk, mask_first, n_keep):
    blk = pl.program_id(1)

    @pl.when(blk == 0)
    def _():
        kbuf[0:WINDOW, :] = ck_ref[0]
        vbuf[0:WINDOW, :] = cv_ref[0]

    cos = cos_ref[...]
    s1 = s1_ref[...]
    s2 = s2_ref[...]

    def rope(x):
        return x * cos + pltpu.roll(x, LANES - ROT_DIM // 2, 1) * s1 + pltpu.roll(x, ROT_DIM // 2, 1) * s2

    scale = HEAD_DIM ** -0.5
    for s in range(ATT_WIDTH // LANES):
        sl = slice(s * LANES, (s + 1) * LANES)
        qbuf[:, sl] = (rope(q_ref[:, sl]) * scale).astype(BF16)
    for s in range(KV_WIDTH // LANES):
        sl = slice(s * LANES, (s + 1) * LANES)
        kbuf[WINDOW:WINDOW + tb, sl] = rope(kv_ref[:, sl])
    vbuf[WINDOW:WINDOW + tb, :] = kv_ref[:, KV_WIDTH:2 * KV_WIDTH]

    nkeys = WINDOW + ch
    for c in range(tb // ch):
        r0 = c * ch
        if mask_first:
            col_chunk = lax.broadcasted_iota(jnp.int32, (1, nkeys), 1) // CHUNK
            valid = (blk * (tb // ch) + c - WINDOW // CHUNK + col_chunk) >= 0
        for h in range(N_KV_HEADS):
            hs = slice(h * HEAD_DIM, (h + 1) * HEAD_DIM)
            qs = jnp.concatenate(
                [qbuf[r0:r0 + ch, (h * Q_PER_KV + g) * HEAD_DIM:(h * Q_PER_KV + g + 1) * HEAD_DIM]
                 for g in range(Q_PER_KV)], axis=0)
            kb = kbuf[r0:r0 + nkeys, hs].astype(BF16)
            vb = vbuf[r0:r0 + nkeys, hs].astype(BF16)
            s = lax.dot_general(qs, kb, (((1,), (1,)), ((), ())), preferred_element_type=F32)
            if mask_first:
                s = jnp.where(valid, s, -jnp.inf)
            sk = sink_ref[h][:, 0:1]
            m = jnp.maximum(jnp.max(s, axis=-1, keepdims=True), sk)
            p = jnp.exp(s - m)
            denom = jnp.sum(p, axis=-1, keepdims=True) + jnp.exp(sk - m)
            o = jnp.dot(p.astype(BF16), vb, preferred_element_type=F32) / denom
            for g in range(Q_PER_KV):
                head = h * Q_PER_KV + g
                att[r0:r0 + ch, head * HEAD_DIM:(head + 1) * HEAD_DIM] = o[g * ch:(g + 1) * ch, :]

    a = (att[...] * g_ref[...]).astype(BF16)
    o_ref[...] = jnp.dot(a, wo_ref[...], preferred_element_type=F32)

    @pl.when(blk == nblk - 1)
    def _():
        nk_ref[0] = kbuf[WINDOW + tb - n_keep:WINDOW + tb, :]
        nv_ref[0] = vbuf[WINDOW + tb - n_keep:WINDOW + tb, :]

    if nblk > 1:
        kbuf[0:WINDOW, :] = kbuf[tb:tb + WINDOW, :]
        vbuf[0:WINDOW, :] = vbuf[tb:tb + WINDOW, :]


def _attention(q, kv, g, cache_k, cache_v, tables, sink_cols, wo, *, bsz, seq, tb, ch, mask_first):
    nblk = seq // tb
    n_keep = min(WINDOW, seq)
    cos, s1, s2 = tables
    row = lambda b, j: (b * nblk + j, 0)
    kern = functools.partial(_attn_kernel, tb=tb, ch=ch, nblk=nblk, mask_first=mask_first, n_keep=n_keep)
    return pl.pallas_call(
        kern,
        grid=(bsz, nblk),
        in_specs=[pl.BlockSpec((tb, ATT_WIDTH), row),
                  pl.BlockSpec((tb, 2 * KV_WIDTH), row),
                  pl.BlockSpec((tb, ATT_WIDTH), row),
                  pl.BlockSpec((1, WINDOW, KV_WIDTH), lambda b, j: (b, 0, 0)),
                  pl.BlockSpec((1, WINDOW, KV_WIDTH), lambda b, j: (b, 0, 0)),
                  pl.BlockSpec((tb, LANES), lambda b, j: (j, 0)),
                  pl.BlockSpec((tb, LANES), lambda b, j: (j, 0)),
                  pl.BlockSpec((tb, LANES), lambda b, j: (j, 0)),
                  _resident((N_KV_HEADS, Q_PER_KV * ch, LANES)),
                  _resident((ATT_WIDTH, D_MODEL))],
        out_specs=[pl.BlockSpec((tb, D_MODEL), row),
                   pl.BlockSpec((1, n_keep, KV_WIDTH), lambda b, j: (b, 0, 0)),
                   pl.BlockSpec((1, n_keep, KV_WIDTH), lambda b, j: (b, 0, 0))],
        out_shape=[jax.ShapeDtypeStruct((bsz * seq, D_MODEL), F32),
                   jax.ShapeDtypeStruct((bsz, n_keep, KV_WIDTH), F32),
                   jax.ShapeDtypeStruct((bsz, n_keep, KV_WIDTH), F32)],
        scratch_shapes=[pltpu.VMEM((tb, ATT_WIDTH), BF16),
                        pltpu.VMEM((WINDOW + tb, KV_WIDTH), F32),
                        pltpu.VMEM((WINDOW + tb, KV_WIDTH), F32),
                        pltpu.VMEM((tb, ATT_WIDTH), F32)],
        compiler_params=_compiler_params(("arbitrary", "arbitrary")),
        name="attention",
    )(q, kv, g, cache_k, cache_v, cos, s1, s2, sink_cols, wo)


PN_K = OFF_KV
PN_V = PN_K + KV_WIDTH
PT_Q = 0
PT_G = PT_Q + ATT_WIDTH
PT_V = PT_G + ATT_WIDTH
PT_WIDTH = PT_V + KV_WIDTH
HALF_ROT = ROT_DIM // 2
PAIR = 2 * CHUNK
KEY_WIN = WINDOW + PAIR


def _inproj_t_kernel(x_ref, nw_ref, w_ref, dtb_ref, cosq_ref, sinq_ref, cosk_ref, s1k_ref, s2k_ref,
                     cw_ref, cb_ref,
                     qt_ref, gt_ref, vt_ref, k_ref, v_ref, z_ref, xc_ref, gates_ref, dt_ref, ctail_ref,
                     cprev, wt_ref, *, tm, nper):
    step = pl.program_id(0)
    seq_start = step % nper == 0
    cur = step % 2

    @pl.when(step == 0)
    def _():
        for dst, src, width in ((PT_Q, OFF_Q, ATT_WIDTH), (PT_G, OFF_G, ATT_WIDTH), (PT_V, PN_V, KV_WIDTH)):
            for n0 in range(0, width, MM_COLS):
                wt_ref[dst + n0:dst + n0 + MM_COLS, :] = w_ref[:, src + n0:src + n0 + MM_COLS].T

    x = x_ref[...]
    ms = jnp.mean(x * x, axis=-1, keepdims=True)
    h = (x * lax.rsqrt(ms + EPS) * nw_ref[...]).astype(BF16)

    def mm_t(off, rows):
        return lax.dot_general(wt_ref[off:off + rows, :], h, (((1,), (1,)), ((), ())),
                               preferred_element_type=F32)

    cosq = cosq_ref[...]
    sinq = sinq_ref[...]
    scale = HEAD_DIM ** -0.5 * LOG2_E


    def q_tail(n0, r):
        r = r * scale
        for a in range(MM_COLS // HEAD_DIM):
            base = a * HEAD_DIM
            x1 = r[base:base + HALF_ROT]
            x2 = r[base + HALF_ROT:base + ROT_DIM]
            rot = jnp.concatenate([x1 * cosq - x2 * sinq, x2 * cosq + x1 * sinq], axis=0)
            qt_ref[n0 + base:n0 + base + ROT_DIM, :] = rot.astype(BF16)
            qt_ref[n0 + base + ROT_DIM:n0 + base + HEAD_DIM, :] = r[base + ROT_DIM:base + HEAD_DIM].astype(BF16)

    def g_tail(n0, r):
        gt_ref[n0:n0 + MM_COLS, :] = _silu(r).astype(BF16)

    def vt_tail(r):
        vt_ref[...] = r.astype(BF16)

    def mm_n(off, width):
        return jnp.dot(h, w_ref[:, off:off + width], preferred_element_type=F32)

    def nat_tail(out_ref, n0, width, act, r):
        out_ref[:, n0:n0 + width] = act(r).astype(out_ref.dtype)

    def k_tail(k_all):
        for s in range(KV_WIDTH // LANES):
            kk = k_all[:, s * LANES:(s + 1) * LANES]
            k_ref[:, s * LANES:(s + 1) * LANES] = (
                kk * cosk_ref[...] + pltpu.roll(kk, LANES - HALF_ROT, 1) * s1k_ref[...]
                + pltpu.roll(kk, HALF_ROT, 1) * s2k_ref[...])

    row8 = lax.broadcasted_iota(jnp.int32, (SUBLANES, LANES), 0)

    def mixed_tail(j, r2):
        cols = slice(j * LANES, (j + 1) * LANES)
        r = r2[:, 0:LANES]
        prev = jnp.where(seq_start, 0.0, cprev[cur, :, cols])
        acc = cb_ref[:, cols] + r * cw_ref[CONV_W - 1:CONV_W, cols]
        for k in range(1, CONV_W):
            rolled = pltpu.roll(r, k, 0)
            top = jnp.where(row8 < k, pltpu.roll(prev, k, 0), rolled[0:SUBLANES])
            shifted = jnp.concatenate([top, rolled[SUBLANES:tm]], axis=0)
            acc = acc + shifted * cw_ref[CONV_W - 1 - k:CONV_W - k, cols]
        xc_ref[:, cols] = _silu(acc).astype(xc_ref.dtype)
        cprev[1 - cur, :, cols] = r[tm - SUBLANES:tm]
        ctail_ref[0, :, cols] = r[tm - SUBLANES:tm]
        other = r2[:, LANES:MM_COLS]
        if j < N_MIXED_Z:
            z_ref[:, cols] = _silu(other).astype(z_ref.dtype)
        else:
            gates_ref[:, (j - N_MIXED_Z) * LANES:(j - N_MIXED_Z + 1) * LANES] = _sigmoid(other).astype(
                gates_ref.dtype)

    def gate_s_tail(n0, r):
        gates_ref[:, D_MODEL + n0:D_MODEL + n0 + MM_COLS] = _sigmoid(r).astype(gates_ref.dtype)

    lane = lax.broadcasted_iota(jnp.int32, (1, LANES), 1)
    ident = lambda r: r
    dt_act = lambda r: jnp.where(lane < SSM_HEADS, _softplus(r + dtb_ref[...]), 0.0)
    P = functools.partial
    mixed = [(P(mm_n, OFF_MIXED + j * MM_COLS, MM_COLS), P(mixed_tail, j)) for j in range(N_MIXED)]
    rest = ([(P(mm_t, PT_Q + n0, MM_COLS), P(q_tail, n0)) for n0 in range(0, ATT_WIDTH, MM_COLS)]
            + [(P(mm_t, PT_G + n0, MM_COLS), P(g_tail, n0)) for n0 in range(0, ATT_WIDTH, MM_COLS)]
            + [(P(mm_n, OFF_GATE_S + n0, MM_COLS), P(gate_s_tail, n0)) for n0 in range(0, D_MODEL, MM_COLS)]
            + [(P(mm_t, PT_V, KV_WIDTH), vt_tail),
               (P(mm_n, PN_K, KV_WIDTH), k_tail),
               (P(mm_n, PN_V, KV_WIDTH), P(nat_tail, v_ref, 0, KV_WIDTH, ident)),
               (P(mm_n, OFF_DT, LANES), P(nat_tail, dt_ref, 0, LANES, dt_act))])
    order = []
    for idx in range(max(len(mixed), len(rest))):
        order += mixed[idx:idx + 1] + rest[idx:idx + 1]
    pending = order[0][0]()
    for idx, (_, tail) in enumerate(order):
        nxt = order[idx + 1][0]() if idx + 1 < len(order) else None
        tail(pending)
        pending = nxt


def _inproj_t(x2d, nw, w_cat, dtb, qtabs, ktabs, cw, cb, seq, tm):
    m = x2d.shape[0]
    nper = seq // tm
    row = lambda i: (i, 0)
    col = lambda i: (0, i)
    nat_widths = (KV_WIDTH, KV_WIDTH, SSM_WIDTH, CONV_CH, 2 * D_MODEL, LANES)
    nat_dtypes = (F32, F32, BF16, BF16, BF16, F32)
    return pl.pallas_call(
        functools.partial(_inproj_t_kernel, tm=tm, nper=nper),
        grid=(m // tm,),
        in_specs=[pl.BlockSpec((tm, D_MODEL), row),
                  _resident((1, D_MODEL)),
                  _resident((D_MODEL, IN_WIDTH_PADDED)),
                  _resident((1, LANES)),
                  pl.BlockSpec((HALF_ROT, tm), lambda i: (0, i % nper)),
                  pl.BlockSpec((HALF_ROT, tm), lambda i: (0, i % nper)),
                  pl.BlockSpec((tm, LANES), lambda i: (i % nper, 0)),
                  pl.BlockSpec((tm, LANES), lambda i: (i % nper, 0)),
                  pl.BlockSpec((tm, LANES), lambda i: (i % nper, 0)),
                  _resident((CONV_W, CONV_CH)),
                  _resident((1, CONV_CH))],
        out_specs=[pl.BlockSpec((ATT_WIDTH, tm), col),
                   pl.BlockSpec((ATT_WIDTH, tm), col),
                   pl.BlockSpec((KV_WIDTH, tm), col)]
                  + [pl.BlockSpec((tm, w), row) for w in nat_widths]
                  + [pl.BlockSpec((1, SUBLANES, CONV_CH), lambda i: (i // nper, 0, 0))],
        out_shape=[jax.ShapeDtypeStruct((ATT_WIDTH, m), BF16),
                   jax.ShapeDtypeStruct((ATT_WIDTH, m), BF16),
                   jax.ShapeDtypeStruct((KV_WIDTH, m), BF16)]
                  + [jax.ShapeDtypeStruct((m, w), d) for w, d in zip(nat_widths, nat_dtypes)]
                  + [jax.ShapeDtypeStruct((m // seq, SUBLANES, CONV_CH), F32)],
        scratch_shapes=[pltpu.VMEM((2, SUBLANES, CONV_CH), F32), pltpu.VMEM((PT_WIDTH, D_MODEL), BF16)],
        compiler_params=_compiler_params(("arbitrary",)),
        name="inproj_t",
    )(x2d, nw, w_cat, dtb, *qtabs, *ktabs, cw, cb)


def _attn_t_kernel(qt_ref, k_ref, vt_ref, gt_ref, bias_ref, sink_ref, wo_ref, o_ref, kbuf, vtbuf, att_t, *, tb):
    blk = pl.program_id(1)
    cur = blk % 2
    prv = 1 - cur

    @pl.when(blk == 0)
    def _():
        kbuf[1, tb - WINDOW:tb, :] = jnp.zeros((WINDOW, KV_WIDTH), BF16)
        vtbuf[1, :, tb - WINDOW:tb] = jnp.zeros((KV_WIDTH, WINDOW), BF16)

    kbuf[cur] = k_ref[...].astype(BF16)
    vtbuf[cur] = vt_ref[...]
    seq_start = jnp.where(blk == 0, -jnp.inf, 0.0)

    batch_dims = (((2,), (1,)), ((0,), (0,)))
    for pr in range(tb // PAIR):
        c0 = pr * PAIR
        k_wins, vt_wins, qs = [], [], []
        for h in range(N_KV_HEADS):
            hs = slice(h * HEAD_DIM, (h + 1) * HEAD_DIM)
            if pr == 0:
                k_wins.append(jnp.concatenate([kbuf[prv, tb - WINDOW:tb, hs], kbuf[cur, 0:PAIR, hs]], axis=0))
                vt_wins.append(jnp.concatenate([vtbuf[prv, hs, tb - WINDOW:tb], vtbuf[cur, hs, 0:PAIR]], axis=1))
            else:
                k_wins.append(kbuf[cur, c0 - WINDOW:c0 + PAIR, hs])
                vt_wins.append(vtbuf[cur, hs, c0 - WINDOW:c0 + PAIR])
            qs.append(jnp.concatenate(
                [qt_ref[(h * Q_PER_KV + g) * HEAD_DIM:(h * Q_PER_KV + g + 1) * HEAD_DIM, c0:c0 + PAIR]
                 for g in range(Q_PER_KV)], axis=1))
        s = lax.dot_general(jnp.stack(k_wins), jnp.stack(qs), batch_dims,
                            preferred_element_type=F32)
        lo = bias_ref[0:CHUNK] + seq_start if pr == 0 else bias_ref[0:CHUNK]
        mid = [s[:, CHUNK:WINDOW] + seq_start, s[:, WINDOW:WINDOW + CHUNK]] if pr == 0 else [s[:, CHUNK:WINDOW + CHUNK]]
        s = jnp.concatenate([s[:, 0:CHUNK] + lo[None]] + mid
                            + [s[:, WINDOW + CHUNK:KEY_WIN] + bias_ref[WINDOW + CHUNK:KEY_WIN][None]], axis=1)
        sk = sink_ref[...][:, None, :]
        m = jnp.maximum(jnp.max(s, axis=1, keepdims=True), sk)
        p = jnp.exp2(s - m)
        denom = jnp.sum(p, axis=1, keepdims=True) + jnp.exp2(sk - m)
        o = lax.dot_general(jnp.stack(vt_wins), p.astype(BF16), batch_dims, preferred_element_type=F32) / denom
        for h in range(N_KV_HEADS):
            for g in range(Q_PER_KV):
                rows = slice((h * Q_PER_KV + g) * HEAD_DIM, (h * Q_PER_KV + g + 1) * HEAD_DIM)
                att_t[rows, c0:c0 + PAIR] = (
                    o[h, :, g * PAIR:(g + 1) * PAIR] * gt_ref[rows, c0:c0 + PAIR].astype(F32)).astype(BF16)

    o_ref[...] = lax.dot_general(att_t[...], wo_ref[...], (((0,), (0,)), ((), ())), preferred_element_type=F32)


def _attention_t(qt, k, vt, gt, bias, sink_rows, wo, *, bsz, seq, tb):
    nblk = seq // tb
    row = lambda b, j: (b * nblk + j, 0)
    col = lambda b, j: (0, b * nblk + j)
    return pl.pallas_call(
        functools.partial(_attn_t_kernel, tb=tb),
        grid=(bsz, nblk),
        in_specs=[pl.BlockSpec((ATT_WIDTH, tb), col),
                  pl.BlockSpec((tb, KV_WIDTH), row),
                  pl.BlockSpec((KV_WIDTH, tb), col),
                  pl.BlockSpec((ATT_WIDTH, tb), col),
                  _resident((KEY_WIN, Q_PER_KV * PAIR)),
                  _resident((N_KV_HEADS, Q_PER_KV * PAIR)),
                  _resident((ATT_WIDTH, D_MODEL))],
        out_specs=pl.BlockSpec((tb, D_MODEL), row),
        out_shape=jax.ShapeDtypeStruct((bsz * seq, D_MODEL), F32),
        scratch_shapes=[pltpu.VMEM((2, tb, KV_WIDTH), BF16),
                        pltpu.VMEM((2, KV_WIDTH, tb), BF16),
                        pltpu.VMEM((ATT_WIDTH, tb), BF16)],
        compiler_params=_compiler_params(("arbitrary", "arbitrary")),
        name="attention_t",
    )(qt, k, vt, gt, bias, sink_rows, wo)


def _window_bias():
    key_chunk = np.arange(KEY_WIN)[:, None] // CHUNK
    q_chunk = (np.arange(Q_PER_KV * PAIR)[None, :] % PAIR) // CHUNK
    ok = (key_chunk >= q_chunk) & (key_chunk <= q_chunk + WINDOW // CHUNK)
    return jnp.asarray(np.where(ok, 0.0, -np.inf), F32)


SSD_Q = 64
PIECE = 32
QUAD = 4 * SSM_HEAD_DIM


def _split3(x):
    hi = x.astype(BF16).astype(F32)
    r1 = x - hi
    mid = r1.astype(BF16).astype(F32)
    lo = (r1 - mid).astype(BF16).astype(F32)
    return (hi + pltpu.roll(mid, PIECE, 1) + pltpu.roll(lo, 2 * PIECE, 1)).astype(BF16)


def _ssd_kernel(*refs, tc, out_row, conv_in_kernel):
    if conv_in_kernel:
        (xbc_ref, z_ref, dt_ref, x_ref, ba_ref, gates_ref, h0_ref, cs0_ref, cw_ref, cb_ref,
         a_ref, e3_ref, dx_ref, wso_ref, wout_ref, pnw_ref, eye_ref, tril_ref, qmask_ref,
         y_ref, nssm_ref, nconv_ref, ht, ybuf, dtx_buf, cumx_buf, xdtb_buf, xw_buf, m4_buf, xpad, xc) = refs
    else:
        (xc, z_ref, dt_ref, x_ref, ba_ref, gates_ref, h0_ref,
         a_ref, e3_ref, dx_ref, wso_ref, wout_ref, pnw_ref, eye_ref, tril_ref, qmask_ref,
         y_ref, nssm_ref, ht, ybuf, dtx_buf, cumx_buf, xdtb_buf, xw_buf, m4_buf) = refs
    blk = pl.program_id(1)
    nblk = pl.num_programs(1)

    @pl.when(blk == 0)
    def _():
        ht[...] = h0_ref[0].T

    if conv_in_kernel:
        pad0 = SUBLANES - (CONV_W - 1)
        xpad[pad0:SUBLANES, :] = cs0_ref[0]
        xpad[SUBLANES:SUBLANES + tc, :] = xbc_ref[...]
        acc = cb_ref[...] + xpad[pad0:pad0 + tc, :] * cw_ref[0:1, :]
        for tap in range(1, CONV_W):
            acc = acc + xpad[pad0 + tap:pad0 + tap + tc, :] * cw_ref[tap:tap + 1, :]
        xc[...] = _silu(acc)
        nconv_ref[0] = xpad[SUBLANES + out_row - (CONV_W - 1):SUBLANES + out_row, :]

    n_chunks = tc // SSD_Q
    lane = lax.broadcasted_iota(jnp.int32, (tc, LANES), 1)
    ri = lax.broadcasted_iota(jnp.int32, (n_chunks, SSD_Q, SSD_Q), 1)
    rj = lax.broadcasted_iota(jnp.int32, (n_chunks, SSD_Q, SSD_Q), 2)
    tri = (ri >= rj).astype(BF16)
    dx = dx_ref[...]

    def expand(v):
        return jnp.dot(_split3(v), e3_ref[...], preferred_element_type=F32)

    dt_all = dt_ref[...]
    da3 = _split3(dt_all * a_ref[...]).reshape(n_chunks, SSD_Q, LANES)
    cum3 = lax.dot_general(tri, da3, (((2,), (1,)), ((0,), (0,))),
                           preferred_element_type=F32).reshape(tc, LANES)
    cum_all = jnp.where(lane < PIECE,
                        cum3 + pltpu.roll(cum3, LANES - PIECE, 1) + pltpu.roll(cum3, LANES - 2 * PIECE, 1), 0.0)
    dtx_buf[...] = expand(dt_all)
    cumx_buf[...] = expand(cum_all)
    c_parts, b_parts = [], []
    for c in range(n_chunks):
        rws = slice(c * SSD_Q, (c + 1) * SSD_Q)
        for g in range(SSM_GROUPS):
            bg = xc[rws, SSM_WIDTH + g * SSM_STATE:SSM_WIDTH + (g + 1) * SSM_STATE].astype(BF16)
            c_parts.append(xc[rws, SSM_WIDTH + BC_WIDTH + g * SSM_STATE:
                              SSM_WIDTH + BC_WIDTH + (g + 1) * SSM_STATE].astype(BF16))
            b_parts.append(jnp.concatenate([bg] * 4, axis=0))
    cb_all = lax.dot_general(jnp.stack(c_parts), jnp.stack(b_parts), (((2,), (2,)), ((0,), (0,))),
                             preferred_element_type=F32)
    for c in range(n_chunks):
        rws = slice(c * SSD_Q, (c + 1) * SSD_Q)
        cum_x = cumx_buf[rws, :]
        xdt = xc[rws, 0:SSM_WIDTH].astype(F32) * dtx_buf[rws, :]
        xdtb_buf[rws, :] = xdt.astype(BF16)
        xw_buf[rws, :] = (xdt * jnp.exp2(cum_x[SSD_Q - 1:SSD_Q, :] - cum_x)).astype(BF16)
        cum_j = jnp.sum(cum_x * eye_ref[...], axis=0, keepdims=True)
        decay = jnp.exp2(cum_x - cum_j + tril_ref[...])
        for g in range(SSM_GROUPS):
            for half in range(GROUP_WIDTH // QUAD):
                qs = slice(g * GROUP_WIDTH + half * QUAD, g * GROUP_WIDTH + (half + 1) * QUAD)
                m4_buf[rws, qs] = (cb_all[c * SSM_GROUPS + g] * decay[:, qs]).astype(BF16)

    def chunk(c, carry):
        r0 = pl.multiple_of(c * SSD_Q, SSD_Q)
        rows = pl.ds(r0, SSD_Q)
        xs = xc[rows, 0:SSM_WIDTH].astype(F32)
        bm = xc[rows, SSM_WIDTH:SSM_WIDTH + BC_WIDTH].astype(BF16)
        cm = xc[rows, SSM_WIDTH + BC_WIDTH:CONV_CH].astype(BF16)
        ecum_x = jnp.exp2(cumx_buf[rows, :])

        y_parts = []
        for g in range(SSM_GROUPS):
            cg = cm[:, g * SSM_STATE:(g + 1) * SSM_STATE]
            bg = bm[:, g * SSM_STATE:(g + 1) * SSM_STATE]
            gs = slice(g * GROUP_WIDTH, (g + 1) * GROUP_WIDTH)
            y_state = jnp.dot(cg, ht[:, gs].astype(BF16), preferred_element_type=F32) * ecum_x[:, gs]
            for half in range(GROUP_WIDTH // QUAD):
                qs = slice(g * GROUP_WIDTH + half * QUAD, g * GROUP_WIDTH + (half + 1) * QUAD)
                x4 = xdtb_buf[rows, qs]
                bd = jnp.concatenate([x4 * qmask_ref[a:a + 1, :] for a in range(4)], axis=0)
                y_parts.append(jnp.dot(m4_buf[rows, qs], bd, preferred_element_type=F32)
                               + y_state[:, half * QUAD:(half + 1) * QUAD])
            ht[:, gs] = (ht[:, gs] * ecum_x[SSD_Q - 1:SSD_Q, gs]
                         + jnp.dot(bg.T, xw_buf[rows, gs], preferred_element_type=F32))
        y = jnp.concatenate(y_parts, axis=1) + xs * dx
        yz = y * z_ref[rows, :].astype(F32)
        for g in range(SSM_GROUPS):
            gs = slice(g * GROUP_WIDTH, (g + 1) * GROUP_WIDTH)
            yg = yz[:, gs]
            ms = jnp.mean(yg * yg, axis=-1, keepdims=True)
            ybuf[rows, gs] = (yg * lax.rsqrt(ms + EPS)).astype(BF16)
        return carry

    lax.fori_loop(0, n_chunks, chunk, 0, unroll=min(4, n_chunks))
    branch_s = jnp.dot(ybuf[...], wso_ref[...], preferred_element_type=F32)
    merged = (gates_ref[:, 0:D_MODEL].astype(F32) * ba_ref[...]
              + gates_ref[:, D_MODEL:2 * D_MODEL].astype(F32) * branch_s)
    out = jnp.dot(merged.astype(BF16), wout_ref[...], preferred_element_type=F32)
    ms = jnp.mean(out * out, axis=-1, keepdims=True)
    y_ref[...] = x_ref[...] + out * lax.rsqrt(ms + EPS) * pnw_ref[...]

    @pl.when(blk == nblk - 1)
    def _():
        nssm_ref[0] = ht[...].T


def _ssd_masks():
    i = np.arange(SSD_Q)[:, None]
    j = np.arange(SSM_WIDTH)[None, :] % SSD_Q
    eye = jnp.asarray(i == j, F32)
    tril = jnp.asarray(np.where(i >= j, 0.0, -np.inf), F32)
    qmask = jnp.asarray(np.arange(QUAD)[None, :] // SSM_HEAD_DIM == np.arange(4)[:, None], BF16)
    return eye, tril, qmask


def _ssd(xc, zs, dt, x2d, branch_a, gates, ssm0, conv0, wts, *, bsz, seq, tc, l_real, conv_in_kernel):
    nblk = seq // tc
    assert not conv_in_kernel or nblk == 1
    row = lambda b, j: (b * nblk + j, 0)
    per_seq = lambda b, j: (b, 0, 0)
    hp = SSM_HEADS * SSM_HEAD_DIM
    kern = functools.partial(_ssd_kernel, tc=tc, out_row=l_real, conv_in_kernel=conv_in_kernel)
    operands = [xc, zs, dt, x2d, branch_a, gates, ssm0]
    in_specs = [pl.BlockSpec((tc, CONV_CH), row),
                pl.BlockSpec((tc, SSM_WIDTH), row),
                pl.BlockSpec((tc, LANES), row),
                pl.BlockSpec((tc, D_MODEL), row),
                pl.BlockSpec((tc, D_MODEL), row),
                pl.BlockSpec((tc, 2 * D_MODEL), row),
                pl.BlockSpec((1, hp, SSM_STATE), per_seq)]
    out_specs = [pl.BlockSpec((tc, D_MODEL), row), pl.BlockSpec((1, hp, SSM_STATE), per_seq)]
    out_shape = [jax.ShapeDtypeStruct((bsz * seq, D_MODEL), F32),
                 jax.ShapeDtypeStruct((bsz, hp, SSM_STATE), F32)]
    scratch = [pltpu.VMEM((SSM_STATE, SSM_WIDTH), F32), pltpu.VMEM((tc, SSM_WIDTH), BF16),
               pltpu.VMEM((tc, SSM_WIDTH), F32), pltpu.VMEM((tc, SSM_WIDTH), F32),
               pltpu.VMEM((tc, SSM_WIDTH), BF16), pltpu.VMEM((tc, SSM_WIDTH), BF16),
               pltpu.VMEM((tc, SSM_WIDTH), BF16)]
    if conv_in_kernel:
        operands += [conv0, wts["conv_w"], wts["conv_b"]]
        in_specs += [pl.BlockSpec((1, CONV_W - 1, CONV_CH), per_seq),
                     _resident((CONV_W, CONV_CH)), _resident((1, CONV_CH))]
        out_specs.append(pl.BlockSpec((1, CONV_W - 1, CONV_CH), per_seq))
        out_shape.append(jax.ShapeDtypeStruct((bsz, CONV_W - 1, CONV_CH), F32))
        scratch += [pltpu.VMEM((SUBLANES + tc, CONV_CH), F32), pltpu.VMEM((tc, CONV_CH), F32)]
    consts = [wts["a_row"], wts["e3"], wts["d_x"], wts["w_ssm_o"], wts["w_out"],
              wts["post_norm_w"], *_ssd_masks()]
    operands += consts
    in_specs += [_resident(c.shape) for c in consts]
    return pl.pallas_call(
        kern,
        grid=(bsz, nblk),
        in_specs=in_specs,
        out_specs=out_specs,
        out_shape=out_shape,
        scratch_shapes=scratch,
        compiler_params=_compiler_params(("arbitrary", "arbitrary")),
        name="ssd",
    )(*operands)


def _rope_angles(pos):
    half = ROT_DIM // 2
    inv = ROPE_THETA ** (-(jnp.arange(half, dtype=F32) * 2.0 / ROT_DIM))
    ang = pos.astype(F32)[:, None] * inv[None, :]
    return jnp.cos(ang), jnp.sin(ang)


def _rope_tables(pos):
    cos, sin = _rope_angles(pos)
    half = ROT_DIM // 2
    n = pos.shape[0]
    rest = HEAD_DIM - ROT_DIM
    c = jnp.concatenate([cos, cos, jnp.ones((n, rest), F32)], axis=1)
    s1 = jnp.concatenate([-sin, jnp.zeros((n, half + rest), F32)], axis=1)
    s2 = jnp.concatenate([jnp.zeros((n, half), F32), sin, jnp.zeros((n, rest), F32)], axis=1)
    rep = LANES // HEAD_DIM
    return tuple(jnp.tile(t, (1, rep)) for t in (c, s1, s2))


def _layer_prompt(x, pos, wts, *, tm, tb, tc):
    bsz, seq, _ = x.shape
    x2d = x.reshape(bsz * seq, D_MODEL)
    cos, sin = _rope_angles(pos)
    qt, gt, vt, k, v, zs, xc, gates, dt, conv_tail = _inproj_t(
        x2d, wts["pre_norm_w"], wts["w_cat"], wts["dt_bias"], (cos.T, sin.T), _rope_tables(pos),
        wts["conv_w"], wts["conv_b"], seq, tm)
    sink_rows = jnp.repeat(wts["sinks"].reshape(N_KV_HEADS, Q_PER_KV), PAIR, axis=1) * LOG2_E
    branch_a = _attention_t(qt, k, vt, gt, _window_bias(), sink_rows, wts["w_attn_o"], bsz=bsz, seq=seq, tb=tb)
    n_keep = min(WINDOW, seq)
    keep = lambda t: t.reshape(bsz, seq, KV_WIDTH)[:, seq - n_keep:].reshape(bsz, n_keep, N_KV_HEADS, HEAD_DIM)
    zero_ssm = jnp.zeros((bsz, SSM_HEADS * SSM_HEAD_DIM, SSM_STATE), F32)
    y, new_ssm = _ssd(xc, zs, dt, x2d, branch_a, gates, zero_ssm, None, wts,
                      bsz=bsz, seq=seq, tc=tc, l_real=seq, conv_in_kernel=False)
    return (y.reshape(bsz, seq, D_MODEL), keep(k), keep(v), conv_tail[:, SUBLANES - (CONV_W - 1):],
            new_ssm.reshape(bsz, SSM_HEADS, SSM_HEAD_DIM, SSM_STATE))


def _layer_sample(x, pos, cache_k, cache_v, conv0, ssm0, wts, *, tm):
    bsz, seq, _ = x.shape
    x2d = x.reshape(bsz * seq, D_MODEL)
    q, kv, g, zs, xbc, gates, dt = _inproj(x2d, wts["pre_norm_w"], wts["w_cat"], wts["dt_bias"], tm)
    sink_cols = jnp.broadcast_to(
        jnp.repeat(wts["sinks"].reshape(N_KV_HEADS, Q_PER_KV), seq, axis=1)[:, :, None],
        (N_KV_HEADS, Q_PER_KV * seq, LANES))
    branch_a, new_k, new_v = _attention(
        q, kv, g, cache_k, cache_v, _rope_tables(pos), sink_cols, wts["w_attn_o"],
        bsz=bsz, seq=seq, tb=seq, ch=seq, mask_first=False)
    seq_p = -(-seq // SSD_Q) * SSD_Q
    padrows = lambda t: jnp.pad(t.reshape(bsz, seq, -1), ((0, 0), (0, seq_p - seq), (0, 0))).reshape(
        bsz * seq_p, -1)
    y, new_ssm, new_conv = _ssd(
        padrows(xbc), padrows(zs), padrows(dt), padrows(x2d), padrows(branch_a), padrows(gates), ssm0, conv0, wts,
        bsz=bsz, seq=seq_p, tc=seq_p, l_real=seq, conv_in_kernel=True)
    n_keep = new_k.shape[1]
    return (y.reshape(bsz, seq_p, D_MODEL)[:, :seq],
            new_k.reshape(bsz, n_keep, N_KV_HEADS, HEAD_DIM), new_v.reshape(bsz, n_keep, N_KV_HEADS, HEAD_DIM),
            new_conv, new_ssm.reshape(bsz, SSM_HEADS, SSM_HEAD_DIM, SSM_STATE))


def _prep_weights(pre_norm_w, w_in, conv_w, conv_b, dt_bias, a_log, d_skip, sinks, ssm_norm_w, w_attn_o,
                  w_ssm_o, w_out, post_norm_w):
    g_off = ATT_WIDTH + 2 * KV_WIDTH
    z_off = g_off + ATT_WIDTH
    dt_off = z_off + SSM_WIDTH + CONV_CH
    xbc_off = z_off + SSM_WIDTH
    ga_off = dt_off + SSM_HEADS
    gs_off = ga_off + D_MODEL
    tile = lambda off, j: w_in[:, off + j * LANES:off + (j + 1) * LANES]
    mixed = []
    for j in range(N_MIXED):
        mixed += [tile(xbc_off, j), tile(z_off, j) if j < N_MIXED_Z else tile(ga_off, j - N_MIXED_Z)]
    w_cat = jnp.concatenate(
        mixed + [w_in[:, gs_off:], w_in[:, ATT_WIDTH:g_off],
         w_in[:, dt_off:dt_off + SSM_HEADS], jnp.zeros((D_MODEL, LANES - SSM_HEADS), w_in.dtype),
         w_in[:, :ATT_WIDTH], w_in[:, g_off:z_off]], axis=1).astype(BF16)
    pad_heads = lambda v: jnp.pad(v.astype(F32), (0, LANES - SSM_HEADS)).reshape(1, LANES)
    k_idx = np.arange(LANES)[:, None]
    c_idx = np.arange(SSM_WIDTH)[None, :]
    e3 = jnp.asarray((k_idx < 3 * PIECE) & (k_idx % PIECE == c_idx // SSM_HEAD_DIM), BF16)
    return dict(
        pre_norm_w=pre_norm_w.reshape(1, D_MODEL), w_cat=w_cat, dt_bias=pad_heads(dt_bias),
        conv_w=conv_w, conv_b=conv_b.reshape(1, CONV_CH), a_row=pad_heads(-jnp.exp(a_log.astype(F32)) * LOG2_E),
        e3=e3, d_x=jnp.repeat(d_skip.astype(F32), SSM_HEAD_DIM).reshape(1, SSM_WIDTH), sinks=sinks.astype(F32),
        w_attn_o=w_attn_o.astype(BF16),
        w_ssm_o=(ssm_norm_w.astype(F32)[:, None] * w_ssm_o).astype(BF16),
        w_out=w_out.astype(BF16), post_norm_w=post_norm_w.reshape(1, D_MODEL))


PROMPT_IN_ROWS = 512
PROMPT_ATT_ROWS = 1024
PROMPT_SSD_ROWS = 512
SAMPLE_IN_ROWS = 256


def _largest_tile(rows, cap):
    for unit in (PAIR, CHUNK, SUBLANES):
        best = max((t for t in range(unit, min(rows, cap) + 1, unit) if rows % t == 0), default=0)
        if best:
            return best
    raise ValueError(f"no row tile for {rows} rows")


def _prompt_tiles(seq):
    return dict(tm=_largest_tile(seq, PROMPT_IN_ROWS), tb=_largest_tile(seq, PROMPT_ATT_ROWS),
                tc=_largest_tile(seq, PROMPT_SSD_ROWS))


def kernel(x_prompt, x_sample, cache_k, cache_v, state_conv, state_ssm, pre_norm_w, w_in, conv_w, conv_b, dt_bias,
           a_log, d_skip, sinks, ssm_norm_w, w_attn_o, w_ssm_o, w_out, post_norm_w):
    depth = w_in.shape[0]
    lp = x_prompt.shape[1]
    bs, ls, _ = x_sample.shape
    pos_p = jnp.arange(lp, dtype=F32)
    pos_s = PAST_LEN + jnp.arange(ls, dtype=F32)
    hp = SSM_HEADS * SSM_HEAD_DIM
    yp, ys = x_prompt, x_sample
    outs = [[] for _ in range(8)]
    for layer in range(depth):
        wts = _prep_weights(pre_norm_w[layer], w_in[layer], conv_w[layer], conv_b[layer], dt_bias[layer],
                            a_log[layer], d_skip[layer], sinks[layer], ssm_norm_w[layer], w_attn_o[layer],
                            w_ssm_o[layer], w_out[layer], post_norm_w[layer])
        yp, kp, vp, cp, sp = _layer_prompt(yp, pos_p, wts, **_prompt_tiles(lp))
        ys, ks, vs, cs, ss = _layer_sample(ys, pos_s,
                                           cache_k[layer].reshape(bs, WINDOW, KV_WIDTH),
                                           cache_v[layer].reshape(bs, WINDOW, KV_WIDTH),
                                           state_conv[layer], state_ssm[layer].reshape(bs, hp, SSM_STATE), wts,
                                           tm=_largest_tile(bs * ls, SAMPLE_IN_ROWS))
        for lst, val in zip(outs, (kp, vp, cp, sp, ks, vs, cs, ss)):
            lst.append(val)
    return (yp, ys) + tuple(jnp.stack(lst) for lst in outs)
```

```python
import functools

import jax
import jax.numpy as jnp
import numpy as np
from jax import lax
from jax.experimental import pallas as pl
from jax.experimental.pallas import tpu as pltpu

F32 = jnp.float32
BF16 = jnp.bfloat16

D_MODEL = 1024
PAST_LEN = 2048
CHUNK = 64
N_Q_HEADS = 16
N_KV_HEADS = 4
HEAD_DIM = 64
Q_PER_KV = N_Q_HEADS // N_KV_HEADS
ATT_WIDTH = N_Q_HEADS * HEAD_DIM
KV_WIDTH = N_KV_HEADS * HEAD_DIM
WINDOW = 128
ROT_DIM = HEAD_DIM // 4
ROPE_THETA = 500000.0
SSM_WIDTH = 2 * D_MODEL
SSM_HEAD_DIM = 64
SSM_HEADS = SSM_WIDTH // SSM_HEAD_DIM
SSM_GROUPS = 4
SSM_STATE = 128
GROUP_WIDTH = SSM_WIDTH // SSM_GROUPS
BC_WIDTH = SSM_GROUPS * SSM_STATE
CONV_W = 4
CONV_CH = SSM_WIDTH + 2 * BC_WIDTH
EPS = 1e-6
LOG2_E = 1.4426950408889634

LANES = 128
SUBLANES = 8
VMEM_LIMIT = 58 * 1024 * 1024

N_MIXED = CONV_CH // LANES
N_MIXED_Z = SSM_WIDTH // LANES
OFF_MIXED = 0
OFF_GATE_S = OFF_MIXED + 2 * LANES * N_MIXED
OFF_KV = OFF_GATE_S + D_MODEL
OFF_DT = OFF_KV + 2 * KV_WIDTH
OFF_Q = OFF_DT + LANES
OFF_G = OFF_Q + ATT_WIDTH
IN_WIDTH_PADDED = OFF_G + ATT_WIDTH
MM_COLS = 256
assert N_MIXED - N_MIXED_Z == D_MODEL // LANES and MM_COLS == 2 * LANES


def _sigmoid(x):
    return 0.5 * jnp.tanh(0.5 * x) + 0.5


def _silu(x):
    half = 0.5 * x
    return half * jnp.tanh(half) + half


def _softplus(x):
    return jnp.maximum(x, 0.0) + jnp.log1p(jnp.exp(-jnp.abs(x)))


def _compiler_params(semantics):
    return pltpu.CompilerParams(dimension_semantics=semantics, vmem_limit_bytes=VMEM_LIMIT)


def _resident(shape):
    zeros = (0,) * len(shape)
    return pl.BlockSpec(shape, lambda *_: zeros, pipeline_mode=pl.Buffered(1))


def _inproj_kernel(x_ref, nw_ref, w_ref, dtb_ref, q_ref, kv_ref, g_ref, z_ref, xbc_ref, gates_ref, dt_ref):
    x = x_ref[...]
    ms = jnp.mean(x * x, axis=-1, keepdims=True)
    h = (x * lax.rsqrt(ms + EPS) * nw_ref[...]).astype(BF16)

    def section(out_ref, off, width, act):
        step = min(MM_COLS, width)
        for n0 in range(0, width, step):
            r = jnp.dot(h, w_ref[:, off + n0:off + n0 + step], preferred_element_type=F32)
            out_ref[:, n0:n0 + step] = act(r)

    ident = lambda r: r
    section(q_ref, OFF_Q, ATT_WIDTH, ident)
    section(kv_ref, OFF_KV, 2 * KV_WIDTH, ident)
    section(g_ref, OFF_G, ATT_WIDTH, _silu)
    for j in range(N_MIXED):
        r = jnp.dot(h, w_ref[:, OFF_MIXED + j * MM_COLS:OFF_MIXED + (j + 1) * MM_COLS], preferred_element_type=F32)
        xbc_ref[:, j * LANES:(j + 1) * LANES] = r[:, 0:LANES]
        if j < N_MIXED_Z:
            z_ref[:, j * LANES:(j + 1) * LANES] = _silu(r[:, LANES:MM_COLS])
        else:
            gates_ref[:, (j - N_MIXED_Z) * LANES:(j - N_MIXED_Z + 1) * LANES] = _sigmoid(r[:, LANES:MM_COLS])
    for n0 in range(0, D_MODEL, MM_COLS):
        r = jnp.dot(h, w_ref[:, OFF_GATE_S + n0:OFF_GATE_S + n0 + MM_COLS], preferred_element_type=F32)
        gates_ref[:, D_MODEL + n0:D_MODEL + n0 + MM_COLS] = _sigmoid(r)
    lane = lax.broadcasted_iota(jnp.int32, (1, LANES), 1)
    section(dt_ref, OFF_DT, LANES,
            lambda r: jnp.where(lane < SSM_HEADS, _softplus(r + dtb_ref[...]), 0.0))


def _inproj(x2d, nw, w_cat, dtb, tm):
    m = x2d.shape[0]
    widths = (ATT_WIDTH, 2 * KV_WIDTH, ATT_WIDTH, SSM_WIDTH, CONV_CH, 2 * D_MODEL, LANES)
    return pl.pallas_call(
        _inproj_kernel,
        grid=(m // tm,),
        in_specs=[pl.BlockSpec((tm, D_MODEL), lambda i: (i, 0)),
                  _resident((1, D_MODEL)),
                  _resident((D_MODEL, IN_WIDTH_PADDED)),
                  _resident((1, LANES))],
        out_specs=[pl.BlockSpec((tm, w), lambda i: (i, 0)) for w in widths],
        out_shape=[jax.ShapeDtypeStruct((m, w), F32) for w in widths],
        compiler_params=_compiler_params(("arbitrary",)),
        name="inproj",
    )(x2d, nw, w_cat, dtb)


def _attn_kernel(q_ref, kv_ref, g_ref, ck_ref, cv_ref, cos_ref, s1_ref, s2_ref, sink_ref, wo_ref,
                 o_ref, nk_ref, nv_ref, qbuf, kbuf, vbuf, att, *, tb, ch, nblk, mask_first, n_keep):
    blk = pl.program_id(1)

    @pl.when(blk == 0)
    def _():
        kbuf[0:WINDOW, :] = ck_ref[0]
        vbuf[0:WINDOW, :] = cv_ref[0]

    cos = cos_ref[...]
    s1 = s1_ref[...]
    s2 = s2_ref[...]

    def rope(x):
        return x * cos + pltpu.roll(x, LANES - ROT_DIM // 2, 1) * s1 + pltpu.roll(x, ROT_DIM // 2, 1) * s2

    scale = HEAD_DIM ** -0.5
    for s in range(ATT_WIDTH // LANES):
        sl = slice(s * LANES, (s + 1) * LANES)
        qbuf[:, sl] = (rope(q_ref[:, sl]) * scale).astype(BF16)
    for s in range(KV_WIDTH // LANES):
        sl = slice(s * LANES, (s + 1) * LANES)
        kbuf[WINDOW:WINDOW + tb, sl] = rope(kv_ref[:, sl])
    vbuf[WINDOW:WINDOW + tb, :] = kv_ref[:, KV_WIDTH:2 * KV_WIDTH]

    nkeys = WINDOW + ch
    for c in range(tb // ch):
        r0 = c * ch
        if mask_first:
            col_chunk = lax.broadcasted_iota(jnp.int32, (1, nkeys), 1) // CHUNK
            valid = (blk * (tb // ch) + c - WINDOW // CHUNK + col_chunk) >= 0
        for h in range(N_KV_HEADS):
            hs = slice(h * HEAD_DIM, (h + 1) * HEAD_DIM)
            qs = jnp.concatenate(
                [qbuf[r0:r0 + ch, (h * Q_PER_KV + g) * HEAD_DIM:(h * Q_PER_KV + g + 1) * HEAD_DIM]
                 for g in range(Q_PER_KV)], axis=0)
            kb = kbuf[r0:r0 + nkeys, hs].astype(BF16)
            vb = vbuf[r0:r0 + nkeys, hs].astype(BF16)
            s = lax.dot_general(qs, kb, (((1,), (1,)), ((), ())), preferred_element_type=F32)
            if mask_first:
                s = jnp.where(valid, s, -jnp.inf)
            sk = sink_ref[h][:, 0:1]
            m = jnp.maximum(jnp.max(s, axis=-1, keepdims=True), sk)
            p = jnp.exp(s - m)
            denom = jnp.sum(p, axis=-1, keepdims=True) + jnp.exp(sk - m)
            o = jnp.dot(p.astype(BF16), vb, preferred_element_type=F32) / denom
            for g in range(Q_PER_KV):
                head = h * Q_PER_KV + g
                att[r0:r0 + ch, head * HEAD_DIM:(head + 1) * HEAD_DIM] = o[g * ch:(g + 1) * ch, :]

    a = (att[...] * g_ref[...]).astype(BF16)
    o_ref[...] = jnp.dot(a, wo_ref[...], preferred_element_type=F32)

    @pl.when(blk == nblk - 1)
    def _():
        nk_ref[0] = kbuf[WINDOW + tb - n_keep:WINDOW + tb, :]
        nv_ref[0] = vbuf[WINDOW + tb - n_keep:WINDOW + tb, :]

    if nblk > 1:
        kbuf[0:WINDOW, :] = kbuf[tb:tb + WINDOW, :]
        vbuf[0:WINDOW, :] = vbuf[tb:tb + WINDOW, :]


def _attention(q, kv, g, cache_k, cache_v, tables, sink_cols, wo, *, bsz, seq, tb, ch, mask_first):
    nblk = seq // tb
    n_keep = min(WINDOW, seq)
    cos, s1, s2 = tables
    row = lambda b, j: (b * nblk + j, 0)
    kern = functools.partial(_attn_kernel, tb=tb, ch=ch, nblk=nblk, mask_first=mask_first, n_keep=n_keep)
    return pl.pallas_call(
        kern,
        grid=(bsz, nblk),
        in_specs=[pl.BlockSpec((tb, ATT_WIDTH), row),
                  pl.BlockSpec((tb, 2 * KV_WIDTH), row),
                  pl.BlockSpec((tb, ATT_WIDTH), row),
                  pl.BlockSpec((1, WINDOW, KV_WIDTH), lambda b, j: (b, 0, 0)),
                  pl.BlockSpec((1, WINDOW, KV_WIDTH), lambda b, j: (b, 0, 0)),
                  pl.BlockSpec((tb, LANES), lambda b, j: (j, 0)),
                  pl.BlockSpec((tb, LANES), lambda b, j: (j, 0)),
                  pl.BlockSpec((tb, LANES), lambda b, j: (j, 0)),
                  _resident((N_KV_HEADS, Q_PER_KV * ch, LANES)),
                  _resident((ATT_WIDTH, D_MODEL))],
        out_specs=[pl.BlockSpec((tb, D_MODEL), row),
                   pl.BlockSpec((1, n_keep, KV_WIDTH), lambda b, j: (b, 0, 0)),
                   pl.BlockSpec((1, n_keep, KV_WIDTH), lambda b, j: (b, 0, 0))],
        out_shape=[jax.ShapeDtypeStruct((bsz * seq, D_MODEL), F32),
                   jax.ShapeDtypeStruct((bsz, n_keep, KV_WIDTH), F32),
                   jax.ShapeDtypeStruct((bsz, n_keep, KV_WIDTH), F32)],
        scratch_shapes=[pltpu.VMEM((tb, ATT_WIDTH), BF16),
                        pltpu.VMEM((WINDOW + tb, KV_WIDTH), F32),
                        pltpu.VMEM((WINDOW + tb, KV_WIDTH), F32),
                        pltpu.VMEM((tb, ATT_WIDTH), F32)],
        compiler_params=_compiler_params(("arbitrary", "arbitrary")),
        name="attention",
    )(q, kv, g, cache_k, cache_v, cos, s1, s2, sink_cols, wo)


PN_K = OFF_KV
PN_V = PN_K + KV_WIDTH
PT_Q = 0
PT_G = PT_Q + ATT_WIDTH
PT_V = PT_G + ATT_WIDTH
PT_WIDTH = PT_V + KV_WIDTH
HALF_ROT = ROT_DIM // 2
PAIR = 2 * CHUNK
KEY_WIN = WINDOW + PAIR


def _inproj_t_kernel(x_ref, nw_ref, w_ref, dtb_ref, cosq_ref, sinq_ref, cosk_ref, s1k_ref, s2k_ref,
                     cw_ref, cb_ref,
                     qt_ref, gt_ref, vt_ref, k_ref, v_ref, z_ref, xc_ref, gates_ref, dt_ref, ctail_ref,
                     cprev, wt_ref, *, tm, nper):
    step = pl.program_id(0)
    seq_start = step % nper == 0
    cur = step % 2

    @pl.when(step == 0)
    def _():
        for dst, src, width in ((PT_Q, OFF_Q, ATT_WIDTH), (PT_G, OFF_G, ATT_WIDTH), (PT_V, PN_V, KV_WIDTH)):
            for n0 in range(0, width, MM_COLS):
                wt_ref[dst + n0:dst + n0 + MM_COLS, :] = w_ref[:, src + n0:src + n0 + MM_COLS].T

    x = x_ref[...]
    ms = jnp.mean(x * x, axis=-1, keepdims=True)
    h = (x * lax.rsqrt(ms + EPS) * nw_ref[...]).astype(BF16)

    def mm_t(off, rows):
        return lax.dot_general(wt_ref[off:off + rows, :], h, (((1,), (1,)), ((), ())),
                               preferred_element_type=F32)

    cosq = cosq_ref[...]
    sinq = sinq_ref[...]
    scale = HEAD_DIM ** -0.5 * LOG2_E


    def q_tail(n0, r):
        r = r * scale
        for a in range(MM_COLS // HEAD_DIM):
            base = a * HEAD_DIM
            x1 = r[base:base + HALF_ROT]
            x2 = r[base + HALF_ROT:base + ROT_DIM]
            rot = jnp.concatenate([x1 * cosq - x2 * sinq, x2 * cosq + x1 * sinq], axis=0)
            qt_ref[n0 + base:n0 + base + ROT_DIM, :] = rot.astype(BF16)
            qt_ref[n0 + base + ROT_DIM:n0 + base + HEAD_DIM, :] = r[base + ROT_DIM:base + HEAD_DIM].astype(BF16)

    def g_tail(n0, r):
        gt_ref[n0:n0 + MM_COLS, :] = _silu(r).astype(BF16)

    def vt_tail(r):
        vt_ref[...] = r.astype(BF16)

    def mm_n(off, width):
        return jnp.dot(h, w_ref[:, off:off + width], preferred_element_type=F32)

    def nat_tail(out_ref, n0, width, act, r):
        out_ref[:, n0:n0 + width] = act(r).astype(out_ref.dtype)

    def k_tail(k_all):
        for s in range(KV_WIDTH // LANES):
            kk = k_all[:, s * LANES:(s + 1) * LANES]
            k_ref[:, s * LANES:(s + 1) * LANES] = (
                kk * cosk_ref[...] + pltpu.roll(kk, LANES - HALF_ROT, 1) * s1k_ref[...]
                + pltpu.roll(kk, HALF_ROT, 1) * s2k_ref[...])

    row8 = lax.broadcasted_iota(jnp.int32, (SUBLANES, LANES), 0)

    def mixed_tail(j, r2):
        cols = slice(j * LANES, (j + 1) * LANES)
        r = r2[:, 0:LANES]
        prev = jnp.where(seq_start, 0.0, cprev[cur, :, cols])
        acc = cb_ref[:, cols] + r * cw_ref[CONV_W - 1:CONV_W, cols]
        for k in range(1, CONV_W):
            rolled = pltpu.roll(r, k, 0)
            top = jnp.where(row8 < k, pltpu.roll(prev, k, 0), rolled[0:SUBLANES])
            shifted = jnp.concatenate([top, rolled[SUBLANES:tm]], axis=0)
            acc = acc + shifted * cw_ref[CONV_W - 1 - k:CONV_W - k, cols]
        xc_ref[:, cols] = _silu(acc).astype(xc_ref.dtype)
        cprev[1 - cur, :, cols] = r[tm - SUBLANES:tm]
        ctail_ref[0, :, cols] = r[tm - SUBLANES:tm]
        other = r2[:, LANES:MM_COLS]
        if j < N_MIXED_Z:
            z_ref[:, cols] = _silu(other).astype(z_ref.dtype)
        else:
            gates_ref[:, (j - N_MIXED_Z) * LANES:(j - N_MIXED_Z + 1) * LANES] = _sigmoid(other).astype(
                gates_ref.dtype)

    def gate_s_tail(n0, r):
        gates_ref[:, D_MODEL + n0:D_MODEL + n0 + MM_COLS] = _sigmoid(r).astype(gates_ref.dtype)

    lane = lax.broadcasted_iota(jnp.int32, (1, LANES), 1)
    ident = lambda r: r
    dt_act = lambda r: jnp.where(lane < SSM_HEADS, _softplus(r + dtb_ref[...]), 0.0)
    P = functools.partial
    mixed = [(P(mm_n, OFF_MIXED + j * MM_COLS, MM_COLS), P(mixed_tail, j)) for j in range(N_MIXED)]
    rest = ([(P(mm_t, PT_Q + n0, MM_COLS), P(q_tail, n0)) for n0 in range(0, ATT_WIDTH, MM_COLS)]
            + [(P(mm_t, PT_G + n0, MM_COLS), P(g_tail, n0)) for n0 in range(0, ATT_WIDTH, MM_COLS)]
            + [(P(mm_n, OFF_GATE_S + n0, MM_COLS), P(gate_s_tail, n0)) for n0 in range(0, D_MODEL, MM_COLS)]
            + [(P(mm_t, PT_V, KV_WIDTH), vt_tail),
               (P(mm_n, PN_K, KV_WIDTH), k_tail),
               (P(mm_n, PN_V, KV_WIDTH), P(nat_tail, v_ref, 0, KV_WIDTH, ident)),
               (P(mm_n, OFF_DT, LANES), P(nat_tail, dt_ref, 0, LANES, dt_act))])
    order = []
    for idx in range(max(len(mixed), len(rest))):
        order += mixed[idx:idx + 1] + rest[idx:idx + 1]
    pending = order[0][0]()
    for idx, (_, tail) in enumerate(order):
        nxt = order[idx + 1][0]() if idx + 1 < len(order) else None
        tail(pending)
        pending = nxt


def _inproj_t(x2d, nw, w_cat, dtb, qtabs, ktabs, cw, cb, seq, tm):
    m = x2d.shape[0]
    nper = seq // tm
    row = lambda i: (i, 0)
    col = lambda i: (0, i)
    nat_widths = (KV_WIDTH, KV_WIDTH, SSM_WIDTH, CONV_CH, 2 * D_MODEL, LANES)
    nat_dtypes = (F32, F32, BF16, BF16, BF16, F32)
    return pl.pallas_call(
        functools.partial(_inproj_t_kernel, tm=tm, nper=nper),
        grid=(m // tm,),
        in_specs=[pl.BlockSpec((tm, D_MODEL), row),
                  _resident((1, D_MODEL)),
                  _resident((D_MODEL, IN_WIDTH_PADDED)),
                  _resident((1, LANES)),
                  pl.BlockSpec((HALF_ROT, tm), lambda i: (0, i % nper)),
                  pl.BlockSpec((HALF_ROT, tm), lambda i: (0, i % nper)),
                  pl.BlockSpec((tm, LANES), lambda i: (i % nper, 0)),
                  pl.BlockSpec((tm, LANES), lambda i: (i % nper, 0)),
                  pl.BlockSpec((tm, LANES), lambda i: (i % nper, 0)),
                  _resident((CONV_W, CONV_CH)),
                  _resident((1, CONV_CH))],
        out_specs=[pl.BlockSpec((ATT_WIDTH, tm), col),
                   pl.BlockSpec((ATT_WIDTH, tm), col),
                   pl.BlockSpec((KV_WIDTH, tm), col)]
                  + [pl.BlockSpec((tm, w), row) for w in nat_widths]
                  + [pl.BlockSpec((1, SUBLANES, CONV_CH), lambda i: (i // nper, 0, 0))],
        out_shape=[jax.ShapeDtypeStruct((ATT_WIDTH, m), BF16),
                   jax.ShapeDtypeStruct((ATT_WIDTH, m), BF16),
                   jax.ShapeDtypeStruct((KV_WIDTH, m), BF16)]
                  + [jax.ShapeDtypeStruct((m, w), d) for w, d in zip(nat_widths, nat_dtypes)]
                  + [jax.ShapeDtypeStruct((m // seq, SUBLANES, CONV_CH), F32)],
        scratch_shapes=[pltpu.VMEM((2, SUBLANES, CONV_CH), F32), pltpu.VMEM((PT_WIDTH, D_MODEL), BF16)],
        compiler_params=_compiler_params(("arbitrary",)),
        name="inproj_t",
    )(x2d, nw, w_cat, dtb, *qtabs, *ktabs, cw, cb)


def _attn_t_kernel(qt_ref, k_ref, vt_ref, gt_ref, bias_ref, sink_ref, wo_ref, o_ref, kbuf, vtbuf, att_t, *, tb):
    blk = pl.program_id(1)
    cur = blk % 2
    prv = 1 - cur

    @pl.when(blk == 0)
    def _():
        kbuf[1, tb - WINDOW:tb, :] = jnp.zeros((WINDOW, KV_WIDTH), BF16)
        vtbuf[1, :, tb - WINDOW:tb] = jnp.zeros((KV_WIDTH, WINDOW), BF16)

    kbuf[cur] = k_ref[...].astype(BF16)
    vtbuf[cur] = vt_ref[...]
    seq_start = jnp.where(blk == 0, -jnp.inf, 0.0)

    batch_dims = (((2,), (1,)), ((0,), (0,)))
    for pr in range(tb // PAIR):
        c0 = pr * PAIR
        k_wins, vt_wins, qs = [], [], []
        for h in range(N_KV_HEADS):
            hs = slice(h * HEAD_DIM, (h + 1) * HEAD_DIM)
            if pr == 0:
                k_wins.append(jnp.concatenate([kbuf[prv, tb - WINDOW:tb, hs], kbuf[cur, 0:PAIR, hs]], axis=0))
                vt_wins.append(jnp.concatenate([vtbuf[prv, hs, tb - WINDOW:tb], vtbuf[cur, hs, 0:PAIR]], axis=1))
            else:
                k_wins.append(kbuf[cur, c0 - WINDOW:c0 + PAIR, hs])
                vt_wins.append(vtbuf[cur, hs, c0 - WINDOW:c0 + PAIR])
            qs.append(jnp.concatenate(
                [qt_ref[(h * Q_PER_KV + g) * HEAD_DIM:(h * Q_PER_KV + g + 1) * HEAD_DIM, c0:c0 + PAIR]
                 for g in range(Q_PER_KV)], axis=1))
        s = lax.dot_general(jnp.stack(k_wins), jnp.stack(qs), batch_dims,
                            preferred_element_type=F32)
        lo = bias_ref[0:CHUNK] + seq_start if pr == 0 else bias_ref[0:CHUNK]
        mid = [s[:, CHUNK:WINDOW] + seq_start, s[:, WINDOW:WINDOW + CHUNK]] if pr == 0 else [s[:, CHUNK:WINDOW + CHUNK]]
        s = jnp.concatenate([s[:, 0:CHUNK] + lo[None]] + mid
                            + [s[:, WINDOW + CHUNK:KEY_WIN] + bias_ref[WINDOW + CHUNK:KEY_WIN][None]], axis=1)
        sk = sink_ref[...][:, None, :]
        m = jnp.maximum(jnp.max(s, axis=1, keepdims=True), sk)
        p = jnp.exp2(s - m)
        denom = jnp.sum(p, axis=1, keepdims=True) + jnp.exp2(sk - m)
        o = lax.dot_general(jnp.stack(vt_wins), p.astype(BF16), batch_dims, preferred_element_type=F32) / denom
        for h in range(N_KV_HEADS):
            for g in range(Q_PER_KV):
                rows = slice((h * Q_PER_KV + g) * HEAD_DIM, (h * Q_PER_KV + g + 1) * HEAD_DIM)
                att_t[rows, c0:c0 + PAIR] = (
                    o[h, :, g * PAIR:(g + 1) * PAIR] * gt_ref[rows, c0:c0 + PAIR].astype(F32)).astype(BF16)

    o_ref[...] = lax.dot_general(att_t[...], wo_ref[...], (((0,), (0,)), ((), ())), preferred_element_type=F32)


def _attention_t(qt, k, vt, gt, bias, sink_rows, wo, *, bsz, seq, tb):
    nblk = seq // tb
    row = lambda b, j: (b * nblk + j, 0)
    col = lambda b, j: (0, b * nblk + j)
    return pl.pallas_call(
        functools.partial(_attn_t_kernel, tb=tb),
        grid=(bsz, nblk),
        in_specs=[pl.BlockSpec((ATT_WIDTH, tb), col),
                  pl.BlockSpec((tb, KV_WIDTH), row),
                  pl.BlockSpec((KV_WIDTH, tb), col),
                  pl.BlockSpec((ATT_WIDTH, tb), col),
                  _resident((KEY_WIN, Q_PER_KV * PAIR)),
                  _resident((N_KV_HEADS, Q_PER_KV * PAIR)),
                  _resident((ATT_WIDTH, D_MODEL))],
        out_specs=pl.BlockSpec((tb, D_MODEL), row),
        out_shape=jax.ShapeDtypeStruct((bsz * seq, D_MODEL), F32),
        scratch_shapes=[pltpu.VMEM((2, tb, KV_WIDTH), BF16),
                        pltpu.VMEM((2, KV_WIDTH, tb), BF16),
                        pltpu.VMEM((ATT_WIDTH, tb), BF16)],
        compiler_params=_compiler_params(("arbitrary", "arbitrary")),
        name="attention_t",
    )(qt, k, vt, gt, bias, sink_rows, wo)


def _window_bias():
    key_chunk = np.arange(KEY_WIN)[:, None] // CHUNK
    q_chunk = (np.arange(Q_PER_KV * PAIR)[None, :] % PAIR) // CHUNK
    ok = (key_chunk >= q_chunk) & (key_chunk <= q_chunk + WINDOW // CHUNK)
    return jnp.asarray(np.where(ok, 0.0, -np.inf), F32)


SSD_Q = 64
PIECE = 32
QUAD = 4 * SSM_HEAD_DIM


def _split3(x):
    hi = x.astype(BF16).astype(F32)
    r1 = x - hi
    mid = r1.astype(BF16).astype(F32)
    lo = (r1 - mid).astype(BF16).astype(F32)
    return (hi + pltpu.roll(mid, PIECE, 1) + pltpu.roll(lo, 2 * PIECE, 1)).astype(BF16)


def _ssd_kernel(*refs, tc, out_row, conv_in_kernel):
    if conv_in_kernel:
        (xbc_ref, z_ref, dt_ref, x_ref, ba_ref, gates_ref, h0_ref, cs0_ref, cw_ref, cb_ref,
         a_ref, e3_ref, dx_ref, wso_ref, wout_ref, pnw_ref, eye_ref, tril_ref, qmask_ref,
         y_ref, nssm_ref, nconv_ref, ht, ybuf, dtx_buf, cumx_buf, xdtb_buf, xw_buf, m4_buf, xpad, xc) = refs
    else:
        (xc, z_ref, dt_ref, x_ref, ba_ref, gates_ref, h0_ref,
         a_ref, e3_ref, dx_ref, wso_ref, wout_ref, pnw_ref, eye_ref, tril_ref, qmask_ref,
         y_ref, nssm_ref, ht, ybuf, dtx_buf, cumx_buf, xdtb_buf, xw_buf, m4_buf) = refs
    blk = pl.program_id(1)
    nblk = pl.num_programs(1)

    @pl.when(blk == 0)
    def _():
        ht[...] = h0_ref[0].T

    if conv_in_kernel:
        pad0 = SUBLANES - (CONV_W - 1)
        xpad[pad0:SUBLANES, :] = cs0_ref[0]
        xpad[SUBLANES:SUBLANES + tc, :] = xbc_ref[...]
        acc = cb_ref[...] + xpad[pad0:pad0 + tc, :] * cw_ref[0:1, :]
        for tap in range(1, CONV_W):
            acc = acc + xpad[pad0 + tap:pad0 + tap + tc, :] * cw_ref[tap:tap + 1, :]
        xc[...] = _silu(acc)
        nconv_ref[0] = xpad[SUBLANES + out_row - (CONV_W - 1):SUBLANES + out_row, :]

    n_chunks = tc // SSD_Q
    lane = lax.broadcasted_iota(jnp.int32, (tc, LANES), 1)
    ri = lax.broadcasted_iota(jnp.int32, (n_chunks, SSD_Q, SSD_Q), 1)
    rj = lax.broadcasted_iota(jnp.int32, (n_chunks, SSD_Q, SSD_Q), 2)
    tri = (ri >= rj).astype(BF16)
    dx = dx_ref[...]

    def expand(v):
        return jnp.dot(_split3(v), e3_ref[...], preferred_element_type=F32)

    dt_all = dt_ref[...]
    da3 = _split3(dt_all * a_ref[...]).reshape(n_chunks, SSD_Q, LANES)
    cum3 = lax.dot_general(tri, da3, (((2,), (1,)), ((0,), (0,))),
                           preferred_element_type=F32).reshape(tc, LANES)
    cum_all = jnp.where(lane < PIECE,
                        cum3 + pltpu.roll(cum3, LANES - PIECE, 1) + pltpu.roll(cum3, LANES - 2 * PIECE, 1), 0.0)
    dtx_buf[...] = expand(dt_all)
    cumx_buf[...] = expand(cum_all)
    c_parts, b_parts = [], []
    for c in range(n_chunks):
        rws = slice(c * SSD_Q, (c + 1) * SSD_Q)
        for g in range(SSM_GROUPS):
            bg = xc[rws, SSM_WIDTH + g * SSM_STATE:SSM_WIDTH + (g + 1) * SSM_STATE].astype(BF16)
            c_parts.append(xc[rws, SSM_WIDTH + BC_WIDTH + g * SSM_STATE:
                              SSM_WIDTH + BC_WIDTH + (g + 1) * SSM_STATE].astype(BF16))
            b_parts.append(jnp.concatenate([bg] * 4, axis=0))
    cb_all = lax.dot_general(jnp.stack(c_parts), jnp.stack(b_parts), (((2,), (2,)), ((0,), (0,))),
                             preferred_element_type=F32)
    for c in range(n_chunks):
        rws = slice(c * SSD_Q, (c + 1) * SSD_Q)
        cum_x = cumx_buf[rws, :]
        xdt = xc[rws, 0:SSM_WIDTH].astype(F32) * dtx_buf[rws, :]
        xdtb_buf[rws, :] = xdt.astype(BF16)
        xw_buf[rws, :] = (xdt * jnp.exp2(cum_x[SSD_Q - 1:SSD_Q, :] - cum_x)).astype(BF16)
        cum_j = jnp.sum(cum_x * eye_ref[...], axis=0, keepdims=True)
        decay = jnp.exp2(cum_x - cum_j + tril_ref[...])
        for g in range(SSM_GROUPS):
            for half in range(GROUP_WIDTH // QUAD):
                qs = slice(g * GROUP_WIDTH + half * QUAD, g * GROUP_WIDTH + (half + 1) * QUAD)
                m4_buf[rws, qs] = (cb_all[c * SSM_GROUPS + g] * decay[:, qs]).astype(BF16)

    def chunk(c, carry):
        r0 = pl.multiple_of(c * SSD_Q, SSD_Q)
        rows = pl.ds(r0, SSD_Q)
        xs = xc[rows, 0:SSM_WIDTH].astype(F32)
        bm = xc[rows, SSM_WIDTH:SSM_WIDTH + BC_WIDTH].astype(BF16)
        cm = xc[rows, SSM_WIDTH + BC_WIDTH:CONV_CH].astype(BF16)
        ecum_x = jnp.exp2(cumx_buf[rows, :])

        y_parts = []
        for g in range(SSM_GROUPS):
            cg = cm[:, g * SSM_STATE:(g + 1) * SSM_STATE]
            bg = bm[:, g * SSM_STATE:(g + 1) * SSM_STATE]
            gs = slice(g * GROUP_WIDTH, (g + 1) * GROUP_WIDTH)
            y_state = jnp.dot(cg, ht[:, gs].astype(BF16), preferred_element_type=F32) * ecum_x[:, gs]
            for half in range(GROUP_WIDTH // QUAD):
                qs = slice(g * GROUP_WIDTH + half * QUAD, g * GROUP_WIDTH + (half + 1) * QUAD)
                x4 = xdtb_buf[rows, qs]
                bd = jnp.concatenate([x4 * qmask_ref[a:a + 1, :] for a in range(4)], axis=0)
                y_parts.append(jnp.dot(m4_buf[rows, qs], bd, preferred_element_type=F32)
                               + y_state[:, half * QUAD:(half + 1) * QUAD])
            ht[:, gs] = (ht[:, gs] * ecum_x[SSD_Q - 1:SSD_Q, gs]
                         + jnp.dot(bg.T, xw_buf[rows, gs], preferred_element_type=F32))
        y = jnp.concatenate(y_parts, axis=1) + xs * dx
        yz = y * z_ref[rows, :].astype(F32)
        for g in range(SSM_GROUPS):
            gs = slice(g * GROUP_WIDTH, (g + 1) * GROUP_WIDTH)
            yg = yz[:, gs]
            ms = jnp.mean(yg * yg, axis=-1, keepdims=True)
            ybuf[rows, gs] = (yg * lax.rsqrt(ms + EPS)).astype(BF16)
        return carry

    lax.fori_loop(0, n_chunks, chunk, 0, unroll=min(8, n_chunks))
    branch_s = jnp.dot(ybuf[...], wso_ref[...], preferred_element_type=F32)
    merged = (gates_ref[:, 0:D_MODEL].astype(F32) * ba_ref[...]
              + gates_ref[:, D_MODEL:2 * D_MODEL].astype(F32) * branch_s)
    out = jnp.dot(merged.astype(BF16), wout_ref[...], preferred_element_type=F32)
    ms = jnp.mean(out * out, axis=-1, keepdims=True)
    y_ref[...] = x_ref[...] + out * lax.rsqrt(ms + EPS) * pnw_ref[...]

    @pl.when(blk == nblk - 1)
    def _():
        nssm_ref[0] = ht[...].T


def _ssd_masks():
    i = np.arange(SSD_Q)[:, None]
    j = np.arange(SSM_WIDTH)[None, :] % SSD_Q
    eye = jnp.asarray(i == j, F32)
    tril = jnp.asarray(np.where(i >= j, 0.0, -np.inf), F32)
    qmask = jnp.asarray(np.arange(QUAD)[None, :] // SSM_HEAD_DIM == np.arange(4)[:, None], BF16)
    return eye, tril, qmask


def _ssd(xc, zs, dt, x2d, branch_a, gates, ssm0, conv0, wts, *, bsz, seq, tc, l_real, conv_in_kernel):
    nblk = seq // tc
    assert not conv_in_kernel or nblk == 1
    row = lambda b, j: (b * nblk + j, 0)
    per_seq = lambda b, j: (b, 0, 0)
    hp = SSM_HEADS * SSM_HEAD_DIM
    kern = functools.partial(_ssd_kernel, tc=tc, out_row=l_real, conv_in_kernel=conv_in_kernel)
    operands = [xc, zs, dt, x2d, branch_a, gates, ssm0]
    in_specs = [pl.BlockSpec((tc, CONV_CH), row),
                pl.BlockSpec((tc, SSM_WIDTH), row),
                pl.BlockSpec((tc, LANES), row),
                pl.BlockSpec((tc, D_MODEL), row),
                pl.BlockSpec((tc, D_MODEL), row),
                pl.BlockSpec((tc, 2 * D_MODEL), row),
                pl.BlockSpec((1, hp, SSM_STATE), per_seq)]
    out_specs = [pl.BlockSpec((tc, D_MODEL), row), pl.BlockSpec((1, hp, SSM_STATE), per_seq)]
    out_shape = [jax.ShapeDtypeStruct((bsz * seq, D_MODEL), F32),
                 jax.ShapeDtypeStruct((bsz, hp, SSM_STATE), F32)]
    scratch = [pltpu.VMEM((SSM_STATE, SSM_WIDTH), F32), pltpu.VMEM((tc, SSM_WIDTH), BF16),
               pltpu.VMEM((tc, SSM_WIDTH), F32), pltpu.VMEM((tc, SSM_WIDTH), F32),
               pltpu.VMEM((tc, SSM_WIDTH), BF16), pltpu.VMEM((tc, SSM_WIDTH), BF16),
               pltpu.VMEM((tc, SSM_WIDTH), BF16)]
    if conv_in_kernel:
        operands += [conv0, wts["conv_w"], wts["conv_b"]]
        in_specs += [pl.BlockSpec((1, CONV_W - 1, CONV_CH), per_seq),
                     _resident((CONV_W, CONV_CH)), _resident((1, CONV_CH))]
        out_specs.append(pl.BlockSpec((1, CONV_W - 1, CONV_CH), per_seq))
        out_shape.append(jax.ShapeDtypeStruct((bsz, CONV_W - 1, CONV_CH), F32))
        scratch += [pltpu.VMEM((SUBLANES + tc, CONV_CH), F32), pltpu.VMEM((tc, CONV_CH), F32)]
    consts = [wts["a_row"], wts["e3"], wts["d_x"], wts["w_ssm_o"], wts["w_out"],
              wts["post_norm_w"], *_ssd_masks()]
    operands += consts
    in_specs += [_resident(c.shape) for c in consts]
    return pl.pallas_call(
        kern,
        grid=(bsz, nblk),
        in_specs=in_specs,
        out_specs=out_specs,
        out_shape=out_shape,
        scratch_shapes=scratch,
        compiler_params=_compiler_params(("arbitrary", "arbitrary")),
        name="ssd",
    )(*operands)


def _rope_angles(pos):
    half = ROT_DIM // 2
    inv = ROPE_THETA ** (-(jnp.arange(half, dtype=F32) * 2.0 / ROT_DIM))
    ang = pos.astype(F32)[:, None] * inv[None, :]
    return jnp.cos(ang), jnp.sin(ang)


def _rope_tables(pos):
    cos, sin = _rope_angles(pos)
    half = ROT_DIM // 2
    n = pos.shape[0]
    rest = HEAD_DIM - ROT_DIM
    c = jnp.concatenate([cos, cos, jnp.ones((n, rest), F32)], axis=1)
    s1 = jnp.concatenate([-sin, jnp.zeros((n, half + rest), F32)], axis=1)
    s2 = jnp.concatenate([jnp.zeros((n, half), F32), sin, jnp.zeros((n, rest), F32)], axis=1)
    rep = LANES // HEAD_DIM
    return tuple(jnp.tile(t, (1, rep)) for t in (c, s1, s2))


def _layer_prompt(x, pos, wts, *, tm, tb, tc):
    bsz, seq, _ = x.shape
    x2d = x.reshape(bsz * seq, D_MODEL)
    cos, sin = _rope_angles(pos)
    qt, gt, vt, k, v, zs, xc, gates, dt, conv_tail = _inproj_t(
        x2d, wts["pre_norm_w"], wts["w_cat"], wts["dt_bias"], (cos.T, sin.T), _rope_tables(pos),
        wts["conv_w"], wts["conv_b"], seq, tm)
    sink_rows = jnp.repeat(wts["sinks"].reshape(N_KV_HEADS, Q_PER_KV), PAIR, axis=1) * LOG2_E
    branch_a = _attention_t(qt, k, vt, gt, _window_bias(), sink_rows, wts["w_attn_o"], bsz=bsz, seq=seq, tb=tb)
    n_keep = min(WINDOW, seq)
    keep = lambda t: t.reshape(bsz, seq, KV_WIDTH)[:, seq - n_keep:].reshape(bsz, n_keep, N_KV_HEADS, HEAD_DIM)
    zero_ssm = jnp.zeros((bsz, SSM_HEADS * SSM_HEAD_DIM, SSM_STATE), F32)
    y, new_ssm = _ssd(xc, zs, dt, x2d, branch_a, gates, zero_ssm, None, wts,
                      bsz=bsz, seq=seq, tc=tc, l_real=seq, conv_in_kernel=False)
    return (y.reshape(bsz, seq, D_MODEL), keep(k), keep(v), conv_tail[:, SUBLANES - (CONV_W - 1):],
            new_ssm.reshape(bsz, SSM_HEADS, SSM_HEAD_DIM, SSM_STATE))


def _layer_sample(x, pos, cache_k, cache_v, conv0, ssm0, wts, *, tm):
    bsz, seq, _ = x.shape
    x2d = x.reshape(bsz * seq, D_MODEL)
    q, kv, g, zs, xbc, gates, dt = _inproj(x2d, wts["pre_norm_w"], wts["w_cat"], wts["dt_bias"], tm)
    sink_cols = jnp.broadcast_to(
        jnp.repeat(wts["sinks"].reshape(N_KV_HEADS, Q_PER_KV), seq, axis=1)[:, :, None],
        (N_KV_HEADS, Q_PER_KV * seq, LANES))
    branch_a, new_k, new_v = _attention(
        q, kv, g, cache_k, cache_v, _rope_tables(pos), sink_cols, wts["w_attn_o"],
        bsz=bsz, seq=seq, tb=seq, ch=seq, mask_first=False)
    seq_p = -(-seq // SSD_Q) * SSD_Q
    padrows = lambda t: jnp.pad(t.reshape(bsz, seq, -1), ((0, 0), (0, seq_p - seq), (0, 0))).reshape(
        bsz * seq_p, -1)
    y, new_ssm, new_conv = _ssd(
        padrows(xbc), padrows(zs), padrows(dt), padrows(x2d), padrows(branch_a), padrows(gates), ssm0, conv0, wts,
        bsz=bsz, seq=seq_p, tc=seq_p, l_real=seq, conv_in_kernel=True)
    n_keep = new_k.shape[1]
    return (y.reshape(bsz, seq_p, D_MODEL)[:, :seq],
            new_k.reshape(bsz, n_keep, N_KV_HEADS, HEAD_DIM), new_v.reshape(bsz, n_keep, N_KV_HEADS, HEAD_DIM),
            new_conv, new_ssm.reshape(bsz, SSM_HEADS, SSM_HEAD_DIM, SSM_STATE))


def _prep_weights(pre_norm_w, w_in, conv_w, conv_b, dt_bias, a_log, d_skip, sinks, ssm_norm_w, w_attn_o,
                  w_ssm_o, w_out, post_norm_w):
    g_off = ATT_WIDTH + 2 * KV_WIDTH
    z_off = g_off + ATT_WIDTH
    dt_off = z_off + SSM_WIDTH + CONV_CH
    xbc_off = z_off + SSM_WIDTH
    ga_off = dt_off + SSM_HEADS
    gs_off = ga_off + D_MODEL
    tile = lambda off, j: w_in[:, off + j * LANES:off + (j + 1) * LANES]
    mixed = []
    for j in range(N_MIXED):
        mixed += [tile(xbc_off, j), tile(z_off, j) if j < N_MIXED_Z else tile(ga_off, j - N_MIXED_Z)]
    w_cat = jnp.concatenate(
        mixed + [w_in[:, gs_off:], w_in[:, ATT_WIDTH:g_off],
         w_in[:, dt_off:dt_off + SSM_HEADS], jnp.zeros((D_MODEL, LANES - SSM_HEADS), w_in.dtype),
         w_in[:, :ATT_WIDTH], w_in[:, g_off:z_off]], axis=1).astype(BF16)
    pad_heads = lambda v: jnp.pad(v.astype(F32), (0, LANES - SSM_HEADS)).reshape(1, LANES)
    k_idx = np.arange(LANES)[:, None]
    c_idx = np.arange(SSM_WIDTH)[None, :]
    e3 = jnp.asarray((k_idx < 3 * PIECE) & (k_idx % PIECE == c_idx // SSM_HEAD_DIM), BF16)
    return dict(
        pre_norm_w=pre_norm_w.reshape(1, D_MODEL), w_cat=w_cat, dt_bias=pad_heads(dt_bias),
        conv_w=conv_w, conv_b=conv_b.reshape(1, CONV_CH), a_row=pad_heads(-jnp.exp(a_log.astype(F32)) * LOG2_E),
        e3=e3, d_x=jnp.repeat(d_skip.astype(F32), SSM_HEAD_DIM).reshape(1, SSM_WIDTH), sinks=sinks.astype(F32),
        w_attn_o=w_attn_o.astype(BF16),
        w_ssm_o=(ssm_norm_w.astype(F32)[:, None] * w_ssm_o).astype(BF16),
        w_out=w_out.astype(BF16), post_norm_w=post_norm_w.reshape(1, D_MODEL))


PROMPT_IN_ROWS = 512
PROMPT_ATT_ROWS = 1024
PROMPT_SSD_ROWS = 512
SAMPLE_IN_ROWS = 256


def _largest_tile(rows, cap):
    for unit in (PAIR, CHUNK, SUBLANES):
        best = max((t for t in range(unit, min(rows, cap) + 1, unit) if rows % t == 0), default=0)
        if best:
            return best
    raise ValueError(f"no row tile for {rows} rows")


def _prompt_tiles(seq):
    return dict(tm=_largest_tile(seq, PROMPT_IN_ROWS), tb=_largest_tile(seq, PROMPT_ATT_ROWS),
                tc=_largest_tile(seq, PROMPT_SSD_ROWS))


def kernel(x_prompt, x_sample, cache_k, cache_v, state_conv, state_ssm, pre_norm_w, w_in, conv_w, conv_b, dt_bias,
           a_log, d_skip, sinks, ssm_norm_w, w_attn_o, w_ssm_o, w_out, post_norm_w):
    depth = w_in.shape[0]
    lp = x_prompt.shape[1]
    bs, ls, _ = x_sample.shape
    pos_p = jnp.arange(lp, dtype=F32)
    pos_s = PAST_LEN + jnp.arange(ls, dtype=F32)
    hp = SSM_HEADS * SSM_HEAD_DIM
    yp, ys = x_prompt, x_sample
    outs = [[] for _ in range(8)]
    for layer in range(depth):
        wts = _prep_weights(pre_norm_w[layer], w_in[layer], conv_w[layer], conv_b[layer], dt_bias[layer],
                            a_log[layer], d_skip[layer], sinks[layer], ssm_norm_w[layer], w_attn_o[layer],
                            w_ssm_o[layer], w_out[layer], post_norm_w[layer])
        yp, kp, vp, cp, sp = _layer_prompt(yp, pos_p, wts, **_prompt_tiles(lp))
        ys, ks, vs, cs, ss = _layer_sample(ys, pos_s,
                                           cache_k[layer].reshape(bs, WINDOW, KV_WIDTH),
                                           cache_v[layer].reshape(bs, WINDOW, KV_WIDTH),
                                           state_conv[layer], state_ssm[layer].reshape(bs, hp, SSM_STATE), wts,
                                           tm=_largest_tile(bs * ls, SAMPLE_IN_ROWS))
        for lst, val in zip(outs, (kp, vp, cp, sp, ks, vs, cs, ss)):
            lst.append(val)
    return (yp, ys) + tuple(jnp.stack(lst) for lst in outs)
```

```python
import functools

import jax
import jax.numpy as jnp
import numpy as np
from jax import lax
from jax.experimental import pallas as pl
from jax.experimental.pallas import tpu as pltpu

F32 = jnp.float32
BF16 = jnp.bfloat16

D_MODEL = 1024
PAST_LEN = 2048
CHUNK = 64
N_Q_HEADS = 16
N_KV_HEADS = 4
HEAD_DIM = 64
Q_PER_KV = N_Q_HEADS // N_KV_HEADS
ATT_WIDTH = N_Q_HEADS * HEAD_DIM
KV_WIDTH = N_KV_HEADS * HEAD_DIM
WINDOW = 128
ROT_DIM = HEAD_DIM // 4
ROPE_THETA = 500000.0
SSM_WIDTH = 2 * D_MODEL
SSM_HEAD_DIM = 64
SSM_HEADS = SSM_WIDTH // SSM_HEAD_DIM
SSM_GROUPS = 4
SSM_STATE = 128
GROUP_WIDTH = SSM_WIDTH // SSM_GROUPS
BC_WIDTH = SSM_GROUPS * SSM_STATE
CONV_W = 4
CONV_CH = SSM_WIDTH + 2 * BC_WIDTH
EPS = 1e-6
LOG2_E = 1.4426950408889634

LANES = 128
SUBLANES = 8
VMEM_LIMIT = 58 * 1024 * 1024

N_MIXED = CONV_CH // LANES
N_MIXED_Z = SSM_WIDTH // LANES
OFF_MIXED = 0
OFF_GATE_S = OFF_MIXED + 2 * LANES * N_MIXED
OFF_KV = OFF_GATE_S + D_MODEL
OFF_DT = OFF_KV + 2 * KV_WIDTH
OFF_Q = OFF_DT + LANES
OFF_G = OFF_Q + ATT_WIDTH
IN_WIDTH_PADDED = OFF_G + ATT_WIDTH
MM_COLS = 256
assert N_MIXED - N_MIXED_Z == D_MODEL // LANES and MM_COLS == 2 * LANES


def _sigmoid(x):
    return 0.5 * jnp.tanh(0.5 * x) + 0.5


def _silu(x):
    half = 0.5 * x
    return half * jnp.tanh(half) + half


def _softplus(x):
    return jnp.maximum(x, 0.0) + jnp.log1p(jnp.exp(-jnp.abs(x)))


def _compiler_params(semantics):
    return pltpu.CompilerParams(dimension_semantics=semantics, vmem_limit_bytes=VMEM_LIMIT)


def _resident(shape):
    zeros = (0,) * len(shape)
    return pl.BlockSpec(shape, lambda *_: zeros, pipeline_mode=pl.Buffered(1))


def _inproj_kernel(x_ref, nw_ref, w_ref, dtb_ref, q_ref, kv_ref, g_ref, z_ref, xbc_ref, gates_ref, dt_ref):
    x = x_ref[...]
    ms = jnp.mean(x * x, axis=-1, keepdims=True)
    h = (x * lax.rsqrt(ms + EPS) * nw_ref[...]).astype(BF16)

    def section(out_ref, off, width, act):
        step = min(MM_COLS, width)
        for n0 in range(0, width, step):
            r = jnp.dot(h, w_ref[:, off + n0:off + n0 + step], preferred_element_type=F32)
            out_ref[:, n0:n0 + step] = act(r)

    ident = lambda r: r
    section(q_ref, OFF_Q, ATT_WIDTH, ident)
    section(kv_ref, OFF_KV, 2 * KV_WIDTH, ident)
    section(g_ref, OFF_G, ATT_WIDTH, _silu)
    for j in range(N_MIXED):
        r = jnp.dot(h, w_ref[:, OFF_MIXED + j * MM_COLS:OFF_MIXED + (j + 1) * MM_COLS], preferred_element_type=F32)
        xbc_ref[:, j * LANES:(j + 1) * LANES] = r[:, 0:LANES]
        if j < N_MIXED_Z:
            z_ref[:, j * LANES:(j + 1) * LANES] = _silu(r[:, LANES:MM_COLS])
        else:
            gates_ref[:, (j - N_MIXED_Z) * LANES:(j - N_MIXED_Z + 1) * LANES] = _sigmoid(r[:, LANES:MM_COLS])
    for n0 in range(0, D_MODEL, MM_COLS):
        r = jnp.dot(h, w_ref[:, OFF_GATE_S + n0:OFF_GATE_S + n0 + MM_COLS], preferred_element_type=F32)
        gates_ref[:, D_MODEL + n0:D_MODEL + n0 + MM_COLS] = _sigmoid(r)
    lane = lax.broadcasted_iota(jnp.int32, (1, LANES), 1)
    section(dt_ref, OFF_DT, LANES,
            lambda r: jnp.where(lane < SSM_HEADS, _softplus(r + dtb_ref[...]), 0.0))


def _inproj(x2d, nw, w_cat, dtb, tm):
    m = x2d.shape[0]
    widths = (ATT_WIDTH, 2 * KV_WIDTH, ATT_WIDTH, SSM_WIDTH, CONV_CH, 2 * D_MODEL, LANES)
    return pl.pallas_call(
        _inproj_kernel,
        grid=(m // tm,),
        in_specs=[pl.BlockSpec((tm, D_MODEL), lambda i: (i, 0)),
                  _resident((1, D_MODEL)),
                  _resident((D_MODEL, IN_WIDTH_PADDED)),
                  _resident((1, LANES))],
        out_specs=[pl.BlockSpec((tm, w), lambda i: (i, 0)) for w in widths],
        out_shape=[jax.ShapeDtypeStruct((m, w), F32) for w in widths],
        compiler_params=_compiler_params(("arbitrary",)),
        name="inproj",
    )(x2d, nw, w_cat, dtb)


def _attn_kernel(q_ref, kv_ref, g_ref, ck_ref, cv_ref, cos_ref, s1_ref, s2_ref, sink_ref, wo_ref,
                 o_ref, nk_ref, nv_ref, qbuf, kbuf, vbuf, att, *, tb, ch, nblk, mask_first, n_keep):
    blk = pl.program_id(1)

    @pl.when(blk == 0)
    def _():
        kbuf[0:WINDOW, :] = ck_ref[0]
        vbuf[0:WINDOW, :] = cv_ref[0]

    cos = cos_ref[...]
    s1 = s1_ref[...]
    s2 = s2_ref[...]

    def rope(x):
        return x * cos + pltpu.roll(x, LANES - ROT_DIM // 2, 1) * s1 + pltpu.roll(x, ROT_DIM // 2, 1) * s2

    scale = HEAD_DIM ** -0.5
    for s in range(ATT_WIDTH // LANES):
        sl = slice(s * LANES, (s + 1) * LANES)
        qbuf[:, sl] = (rope(q_ref[:, sl]) * scale).astype(BF16)
    for s in range(KV_WIDTH // LANES):
        sl = slice(s * LANES, (s + 1) * LANES)
        kbuf[WINDOW:WINDOW + tb, sl] = rope(kv_ref[:, sl])
    vbuf[WINDOW:WINDOW + tb, :] = kv_ref[:, KV_WIDTH:2 * KV_WIDTH]

    nkeys = WINDOW + ch
    for c in range(tb // ch):
        r0 = c * ch
        if mask_first:
            col_chunk = lax.broadcasted_iota(jnp.int32, (1, nkeys), 1) // CHUNK
            valid = (blk * (tb // ch) + c - WINDOW // CHUNK + col_chunk) >= 0
        for h in range(N_KV_HEADS):
            hs = slice(h * HEAD_DIM, (h + 1) * HEAD_DIM)
            qs = jnp.concatenate(
                [qbuf[r0:r0 + ch, (h * Q_PER_KV + g) * HEAD_DIM:(h * Q_PER_KV + g + 1) * HEAD_DIM]
                 for g in range(Q_PER_KV)], axis=0)
            kb = kbuf[r0:r0 + nkeys, hs].astype(BF16)
            vb = vbuf[r0:r0 + nkeys, hs].astype(BF16)
            s = lax.dot_general(qs, kb, (((1,), (1,)), ((), ())), preferred_element_type=F32)
            if mask_first:
                s = jnp.where(valid, s, -jnp.inf)
            sk = sink_ref[h][:, 0:1]
            m = jnp.maximum(jnp.max(s, axis=-1, keepdims=True), sk)
            p = jnp.exp(s - m)
            denom = jnp.sum(p, axis=-1, keepdims=True) + jnp.exp(sk - m)
            o = jnp.dot(p.astype(BF16), vb, preferred_element_type=F32) / denom
            for g in range(Q_PER_KV):
                head = h * Q_PER_KV + g
                att[r0:r0 + ch, head * HEAD_DIM:(head + 1) * HEAD_DIM] = o[g * ch:(g + 1) * ch, :]

    a = (att[...] * g_ref[...]).astype(BF16)
    o_ref[...] = jnp.dot(a, wo_ref[...], preferred_element_type=F32)

    @pl.when(blk == nblk - 1)
    def _():
        nk_ref[0] = kbuf[WINDOW + tb - n_keep:WINDOW + tb, :]
        nv_ref[0] = vbuf[WINDOW + tb - n_keep:WINDOW + tb, :]

    if nblk > 1:
        kbuf[0:WINDOW, :] = kbuf[tb:tb + WINDOW, :]
        vbuf[0:WINDOW, :] = vbuf[tb:tb + WINDOW, :]


def _attention(q, kv, g, cache_k, cache_v, tables, sink_cols, wo, *, bsz, seq, tb, ch, mask_first):
    nblk = seq // tb
    n_keep = min(WINDOW, seq)
    cos, s1, s2 = tables
    row = lambda b, j: (b * nblk + j, 0)
    kern = functools.partial(_attn_kernel, tb=tb, ch=ch, nblk=nblk, mask_first=mask_first, n_keep=n_keep)
    return pl.pallas_call(
        kern,
        grid=(bsz, nblk),
        in_specs=[pl.BlockSpec((tb, ATT_WIDTH), row),
                  pl.BlockSpec((tb, 2 * KV_WIDTH), row),
                  pl.BlockSpec((tb, ATT_WIDTH), row),
                  pl.BlockSpec((1, WINDOW, KV_WIDTH), lambda b, j: (b, 0, 0)),
                  pl.BlockSpec((1, WINDOW, KV_WIDTH), lambda b, j: (b, 0, 0)),
                  pl.BlockSpec((tb, LANES), lambda b, j: (j, 0)),
                  pl.BlockSpec((tb, LANES), lambda b, j: (j, 0)),
                  pl.BlockSpec((tb, LANES), lambda b, j: (j, 0)),
                  _resident((N_KV_HEADS, Q_PER_KV * ch, LANES)),
                  _resident((ATT_WIDTH, D_MODEL))],
        out_specs=[pl.BlockSpec((tb, D_MODEL), row),
                   pl.BlockSpec((1, n_keep, KV_WIDTH), lambda b, j: (b, 0, 0)),
                   pl.BlockSpec((1, n_keep, KV_WIDTH), lambda b, j: (b, 0, 0))],
        out_shape=[jax.ShapeDtypeStruct((bsz * seq, D_MODEL), F32),
                   jax.ShapeDtypeStruct((bsz, n_keep, KV_WIDTH), F32),
                   jax.ShapeDtypeStruct((bsz, n_keep, KV_WIDTH), F32)],
        scratch_shapes=[pltpu.VMEM((tb, ATT_WIDTH), BF16),
                        pltpu.VMEM((WINDOW + tb, KV_WIDTH), F32),
                        pltpu.VMEM((WINDOW + tb, KV_WIDTH), F32),
                        pltpu.VMEM((tb, ATT_WIDTH), F32)],
        compiler_params=_compiler_params(("arbitrary", "arbitrary")),
        name="attention",
    )(q, kv, g, cache_k, cache_v, cos, s1, s2, sink_cols, wo)


PN_K = OFF_KV
PN_V = PN_K + KV_WIDTH
PT_Q = 0
PT_G = PT_Q + ATT_WIDTH
PT_V = PT_G + ATT_WIDTH
PT_WIDTH = PT_V + KV_WIDTH
HALF_ROT = ROT_DIM // 2
PAIR = 2 * CHUNK
KEY_WIN = WINDOW + PAIR
PAIRS_PER_STAGE = 2


def _inproj_t_kernel(x_ref, nw_ref, w_ref, dtb_ref, cosq_ref, sinq_ref, cosk_ref, s1k_ref, s2k_ref,
                     cw_ref, cb_ref,
                     qt_ref, gt_ref, vt_ref, k_ref, v_ref, z_ref, xc_ref, gates_ref, dt_ref, ctail_ref,
                     cprev, wt_ref, *, tm, nper):
    step = pl.program_id(0)
    seq_start = step % nper == 0
    cur = step % 2

    @pl.when(step == 0)
    def _():
        for dst, src, width in ((PT_Q, OFF_Q, ATT_WIDTH), (PT_G, OFF_G, ATT_WIDTH), (PT_V, PN_V, KV_WIDTH)):
            for n0 in range(0, width, MM_COLS):
                wt_ref[dst + n0:dst + n0 + MM_COLS, :] = w_ref[:, src + n0:src + n0 + MM_COLS].T

    x = x_ref[...]
    ms = jnp.mean(x * x, axis=-1, keepdims=True)
    h = (x * lax.rsqrt(ms + EPS) * nw_ref[...]).astype(BF16)

    def mm_t(off, rows):
        return lax.dot_general(wt_ref[off:off + rows, :], h, (((1,), (1,)), ((), ())),
                               preferred_element_type=F32)

    cosq = cosq_ref[...]
    sinq = sinq_ref[...]
    scale = HEAD_DIM ** -0.5 * LOG2_E


    def q_tail(n0, r):
        r = r * scale
        for a in range(MM_COLS // HEAD_DIM):
            base = a * HEAD_DIM
            x1 = r[base:base + HALF_ROT]
            x2 = r[base + HALF_ROT:base + ROT_DIM]
            rot = jnp.concatenate([x1 * cosq - x2 * sinq, x2 * cosq + x1 * sinq], axis=0)
            qt_ref[n0 + base:n0 + base + ROT_DIM, :] = rot.astype(BF16)
            qt_ref[n0 + base + ROT_DIM:n0 + base + HEAD_DIM, :] = r[base + ROT_DIM:base + HEAD_DIM].astype(BF16)

    def g_tail(n0, r):
        gt_ref[n0:n0 + MM_COLS, :] = _silu(r).astype(BF16)

    def vt_tail(r):
        vt_ref[...] = r.astype(BF16)

    def mm_n(off, width):
        return jnp.dot(h, w_ref[:, off:off + width], preferred_element_type=F32)

    def nat_tail(out_ref, n0, width, act, r):
        out_ref[:, n0:n0 + width] = act(r).astype(out_ref.dtype)

    def k_tail(k_all):
        for s in range(KV_WIDTH // LANES):
            kk = k_all[:, s * LANES:(s + 1) * LANES]
            k_ref[:, s * LANES:(s + 1) * LANES] = (
                kk * cosk_ref[...] + pltpu.roll(kk, LANES - HALF_ROT, 1) * s1k_ref[...]
                + pltpu.roll(kk, HALF_ROT, 1) * s2k_ref[...])

    row8 = lax.broadcasted_iota(jnp.int32, (SUBLANES, LANES), 0)

    def mixed_tail(j, r2):
        cols = slice(j * LANES, (j + 1) * LANES)
        r = r2[:, 0:LANES]
        prev = jnp.where(seq_start, 0.0, cprev[cur, :, cols])
        acc = cb_ref[:, cols] + r * cw_ref[CONV_W - 1:CONV_W, cols]
        for k in range(1, CONV_W):
            rolled = pltpu.roll(r, k, 0)
            top = jnp.where(row8 < k, pltpu.roll(prev, k, 0), rolled[0:SUBLANES])
            shifted = jnp.concatenate([top, rolled[SUBLANES:tm]], axis=0)
            acc = acc + shifted * cw_ref[CONV_W - 1 - k:CONV_W - k, cols]
        xc_ref[:, cols] = _silu(acc).astype(xc_ref.dtype)
        cprev[1 - cur, :, cols] = r[tm - SUBLANES:tm]
        ctail_ref[0, :, cols] = r[tm - SUBLANES:tm]
        other = r2[:, LANES:MM_COLS]
        if j < N_MIXED_Z:
            z_ref[:, cols] = _silu(other).astype(z_ref.dtype)
        else:
            gates_ref[:, (j - N_MIXED_Z) * LANES:(j - N_MIXED_Z + 1) * LANES] = _sigmoid(other).astype(
                gates_ref.dtype)

    def gate_s_tail(n0, r):
        gates_ref[:, D_MODEL + n0:D_MODEL + n0 + MM_COLS] = _sigmoid(r).astype(gates_ref.dtype)

    lane = lax.broadcasted_iota(jnp.int32, (1, LANES), 1)
    ident = lambda r: r
    dt_act = lambda r: jnp.where(lane < SSM_HEADS, _softplus(r + dtb_ref[...]), 0.0)
    P = functools.partial
    mixed = [(P(mm_n, OFF_MIXED + j * MM_COLS, MM_COLS), P(mixed_tail, j)) for j in range(N_MIXED)]
    rest = ([(P(mm_t, PT_Q + n0, MM_COLS), P(q_tail, n0)) for n0 in range(0, ATT_WIDTH, MM_COLS)]
            + [(P(mm_t, PT_G + n0, MM_COLS), P(g_tail, n0)) for n0 in range(0, ATT_WIDTH, MM_COLS)]
            + [(P(mm_n, OFF_GATE_S + n0, MM_COLS), P(gate_s_tail, n0)) for n0 in range(0, D_MODEL, MM_COLS)]
            + [(P(mm_t, PT_V, KV_WIDTH), vt_tail),
               (P(mm_n, PN_K, KV_WIDTH), k_tail),
               (P(mm_n, PN_V, KV_WIDTH), P(nat_tail, v_ref, 0, KV_WIDTH, ident)),
               (P(mm_n, OFF_DT, LANES), P(nat_tail, dt_ref, 0, LANES, dt_act))])
    order = []
    for idx in range(max(len(mixed), len(rest))):
        order += mixed[idx:idx + 1] + rest[idx:idx + 1]
    pending = order[0][0]()
    for idx, (_, tail) in enumerate(order):
        nxt = order[idx + 1][0]() if idx + 1 < len(order) else None
        tail(pending)
        pending = nxt


def _inproj_t(x2d, nw, w_cat, dtb, qtabs, ktabs, cw, cb, seq, tm):
    m = x2d.shape[0]
    nper = seq // tm
    row = lambda i: (i, 0)
    col = lambda i: (0, i)
    nat_widths = (KV_WIDTH, KV_WIDTH, SSM_WIDTH, CONV_CH, 2 * D_MODEL, LANES)
    nat_dtypes = (F32, F32, BF16, BF16, BF16, F32)
    return pl.pallas_call(
        functools.partial(_inproj_t_kernel, tm=tm, nper=nper),
        grid=(m // tm,),
        in_specs=[pl.BlockSpec((tm, D_MODEL), row),
                  _resident((1, D_MODEL)),
                  _resident((D_MODEL, IN_WIDTH_PADDED)),
                  _resident((1, LANES)),
                  pl.BlockSpec((HALF_ROT, tm), lambda i: (0, i % nper)),
                  pl.BlockSpec((HALF_ROT, tm), lambda i: (0, i % nper)),
                  pl.BlockSpec((tm, LANES), lambda i: (i % nper, 0)),
                  pl.BlockSpec((tm, LANES), lambda i: (i % nper, 0)),
                  pl.BlockSpec((tm, LANES), lambda i: (i % nper, 0)),
                  _resident((CONV_W, CONV_CH)),
                  _resident((1, CONV_CH))],
        out_specs=[pl.BlockSpec((ATT_WIDTH, tm), col),
                   pl.BlockSpec((ATT_WIDTH, tm), col),
                   pl.BlockSpec((KV_WIDTH, tm), col)]
                  + [pl.BlockSpec((tm, w), row) for w in nat_widths]
                  + [pl.BlockSpec((1, SUBLANES, CONV_CH), lambda i: (i // nper, 0, 0))],
        out_shape=[jax.ShapeDtypeStruct((ATT_WIDTH, m), BF16),
                   jax.ShapeDtypeStruct((ATT_WIDTH, m), BF16),
                   jax.ShapeDtypeStruct((KV_WIDTH, m), BF16)]
                  + [jax.ShapeDtypeStruct((m, w), d) for w, d in zip(nat_widths, nat_dtypes)]
                  + [jax.ShapeDtypeStruct((m // seq, SUBLANES, CONV_CH), F32)],
        scratch_shapes=[pltpu.VMEM((2, SUBLANES, CONV_CH), F32), pltpu.VMEM((PT_WIDTH, D_MODEL), BF16)],
        compiler_params=_compiler_params(("arbitrary",)),
        name="inproj_t",
    )(x2d, nw, w_cat, dtb, *qtabs, *ktabs, cw, cb)


def _attn_t_kernel(qt_ref, k_ref, vt_ref, gt_ref, bias_ref, sink_ref, wo_ref, o_ref, kbuf, vtbuf, att_t, *, tb):
    blk = pl.program_id(1)
    cur = blk % 2
    prv = 1 - cur

    @pl.when(blk == 0)
    def _():
        kbuf[1, tb - WINDOW:tb, :] = jnp.zeros((WINDOW, KV_WIDTH), BF16)
        vtbuf[1, :, tb - WINDOW:tb] = jnp.zeros((KV_WIDTH, WINDOW), BF16)

    kbuf[cur] = k_ref[...].astype(BF16)
    vtbuf[cur] = vt_ref[...]
    seq_start = jnp.where(blk == 0, -jnp.inf, 0.0)

    batch_dims = (((2,), (1,)), ((0,), (0,)))
    def attend(pairs):
        first = pairs[0] == 0
        k_wins, vt_wins, qs = [], [], []
        for pr in pairs:
            c0 = pr * PAIR
            for h in range(N_KV_HEADS):
                hs = slice(h * HEAD_DIM, (h + 1) * HEAD_DIM)
                if pr == 0:
                    k_wins.append(jnp.concatenate([kbuf[prv, tb - WINDOW:tb, hs], kbuf[cur, 0:PAIR, hs]], axis=0))
                    vt_wins.append(jnp.concatenate([vtbuf[prv, hs, tb - WINDOW:tb], vtbuf[cur, hs, 0:PAIR]],
                                                   axis=1))
                else:
                    k_wins.append(kbuf[cur, c0 - WINDOW:c0 + PAIR, hs])
                    vt_wins.append(vtbuf[cur, hs, c0 - WINDOW:c0 + PAIR])
                qs.append(jnp.concatenate(
                    [qt_ref[(h * Q_PER_KV + g) * HEAD_DIM:(h * Q_PER_KV + g + 1) * HEAD_DIM, c0:c0 + PAIR]
                     for g in range(Q_PER_KV)], axis=1))
        s = lax.dot_general(jnp.stack(k_wins), jnp.stack(qs), batch_dims,
                            preferred_element_type=F32)
        lo = bias_ref[0:CHUNK] + seq_start if first else bias_ref[0:CHUNK]
        mid = [s[:, CHUNK:WINDOW] + seq_start, s[:, WINDOW:WINDOW + CHUNK]] if first else [s[:, CHUNK:WINDOW + CHUNK]]
        s = jnp.concatenate([s[:, 0:CHUNK] + lo[None]] + mid
                            + [s[:, WINDOW + CHUNK:KEY_WIN] + bias_ref[WINDOW + CHUNK:KEY_WIN][None]], axis=1)
        sk = jnp.concatenate([sink_ref[...]] * len(pairs), axis=0)[:, None, :]
        m = jnp.maximum(jnp.max(s, axis=1, keepdims=True), sk)
        p = jnp.exp2(s - m)
        denom = jnp.sum(p, axis=1, keepdims=True) + jnp.exp2(sk - m)
        o = lax.dot_general(jnp.stack(vt_wins), p.astype(BF16), batch_dims, preferred_element_type=F32) / denom
        for i, pr in enumerate(pairs):
            c0 = pr * PAIR
            for h in range(N_KV_HEADS):
                for g in range(Q_PER_KV):
                    rows = slice((h * Q_PER_KV + g) * HEAD_DIM, (h * Q_PER_KV + g + 1) * HEAD_DIM)
                    att_t[rows, c0:c0 + PAIR] = (o[i * N_KV_HEADS + h, :, g * PAIR:(g + 1) * PAIR]
                                                 * gt_ref[rows, c0:c0 + PAIR].astype(F32)).astype(BF16)

    n_pairs = tb // PAIR
    attend([0])
    for pr in range(1, n_pairs, PAIRS_PER_STAGE):
        attend(list(range(pr, min(pr + PAIRS_PER_STAGE, n_pairs))))

    o_ref[...] = lax.dot_general(att_t[...], wo_ref[...], (((0,), (0,)), ((), ())), preferred_element_type=F32)


def _attention_t(qt, k, vt, gt, bias, sink_rows, wo, *, bsz, seq, tb):
    nblk = seq // tb
    row = lambda b, j: (b * nblk + j, 0)
    col = lambda b, j: (0, b * nblk + j)
    return pl.pallas_call(
        functools.partial(_attn_t_kernel, tb=tb),
        grid=(bsz, nblk),
        in_specs=[pl.BlockSpec((ATT_WIDTH, tb), col),
                  pl.BlockSpec((tb, KV_WIDTH), row),
                  pl.BlockSpec((KV_WIDTH, tb), col),
                  pl.BlockSpec((ATT_WIDTH, tb), col),
                  _resident((KEY_WIN, Q_PER_KV * PAIR)),
                  _resident((N_KV_HEADS, Q_PER_KV * PAIR)),
                  _resident((ATT_WIDTH, D_MODEL))],
        out_specs=pl.BlockSpec((tb, D_MODEL), row),
        out_shape=jax.ShapeDtypeStruct((bsz * seq, D_MODEL), F32),
        scratch_shapes=[pltpu.VMEM((2, tb, KV_WIDTH), BF16),
                        pltpu.VMEM((2, KV_WIDTH, tb), BF16),
                        pltpu.VMEM((ATT_WIDTH, tb), BF16)],
        compiler_params=_compiler_params(("arbitrary", "arbitrary")),
        name="attention_t",
    )(qt, k, vt, gt, bias, sink_rows, wo)


def _window_bias():
    key_chunk = np.arange(KEY_WIN)[:, None] // CHUNK
    q_chunk = (np.arange(Q_PER_KV * PAIR)[None, :] % PAIR) // CHUNK
    ok = (key_chunk >= q_chunk) & (key_chunk <= q_chunk + WINDOW // CHUNK)
    return jnp.asarray(np.where(ok, 0.0, -np.inf), F32)


SSD_Q = 64
PIECE = 32
QUAD = 4 * SSM_HEAD_DIM


def _split3(x):
    hi = x.astype(BF16).astype(F32)
    r1 = x - hi
    mid = r1.astype(BF16).astype(F32)
    lo = (r1 - mid).astype(BF16).astype(F32)
    return (hi + pltpu.roll(mid, PIECE, 1) + pltpu.roll(lo, 2 * PIECE, 1)).astype(BF16)


def _ssd_kernel(*refs, tc, out_row, conv_in_kernel):
    if conv_in_kernel:
        (xbc_ref, z_ref, dt_ref, x_ref, ba_ref, gates_ref, h0_ref, cs0_ref, cw_ref, cb_ref,
         a_ref, e3_ref, dx_ref, wso_ref, wout_ref, pnw_ref, eye_ref, tril_ref, qmask_ref,
         y_ref, nssm_ref, nconv_ref, ht, ybuf, dtx_buf, cumx_buf, xdtb_buf, xw_buf, m4_buf, xpad, xc) = refs
    else:
        (xc, z_ref, dt_ref, x_ref, ba_ref, gates_ref, h0_ref,
         a_ref, e3_ref, dx_ref, wso_ref, wout_ref, pnw_ref, eye_ref, tril_ref, qmask_ref,
         y_ref, nssm_ref, ht, ybuf, dtx_buf, cumx_buf, xdtb_buf, xw_buf, m4_buf) = refs
    blk = pl.program_id(1)
    nblk = pl.num_programs(1)

    @pl.when(blk == 0)
    def _():
        ht[...] = h0_ref[0].T

    if conv_in_kernel:
        pad0 = SUBLANES - (CONV_W - 1)
        xpad[pad0:SUBLANES, :] = cs0_ref[0]
        xpad[SUBLANES:SUBLANES + tc, :] = xbc_ref[...]
        acc = cb_ref[...] + xpad[pad0:pad0 + tc, :] * cw_ref[0:1, :]
        for tap in range(1, CONV_W):
            acc = acc + xpad[pad0 + tap:pad0 + tap + tc, :] * cw_ref[tap:tap + 1, :]
        xc[...] = _silu(acc)
        nconv_ref[0] = xpad[SUBLANES + out_row - (CONV_W - 1):SUBLANES + out_row, :]

    n_chunks = tc // SSD_Q
    lane = lax.broadcasted_iota(jnp.int32, (tc, LANES), 1)
    ri = lax.broadcasted_iota(jnp.int32, (n_chunks, SSD_Q, SSD_Q), 1)
    rj = lax.broadcasted_iota(jnp.int32, (n_chunks, SSD_Q, SSD_Q), 2)
    tri = (ri >= rj).astype(BF16)
    dx = dx_ref[...]

    def expand(v):
        return jnp.dot(_split3(v), e3_ref[...], preferred_element_type=F32)

    dt_all = dt_ref[...]
    da3 = _split3(dt_all * a_ref[...]).reshape(n_chunks, SSD_Q, LANES)
    cum3 = lax.dot_general(tri, da3, (((2,), (1,)), ((0,), (0,))),
                           preferred_element_type=F32).reshape(tc, LANES)
    cum_all = jnp.where(lane < PIECE,
                        cum3 + pltpu.roll(cum3, LANES - PIECE, 1) + pltpu.roll(cum3, LANES - 2 * PIECE, 1), 0.0)
    dtx_buf[...] = expand(dt_all)
    cumx_buf[...] = expand(cum_all)
    c_parts, b_parts = [], []
    for c in range(n_chunks):
        rws = slice(c * SSD_Q, (c + 1) * SSD_Q)
        for g in range(SSM_GROUPS):
            bg = xc[rws, SSM_WIDTH + g * SSM_STATE:SSM_WIDTH + (g + 1) * SSM_STATE].astype(BF16)
            c_parts.append(xc[rws, SSM_WIDTH + BC_WIDTH + g * SSM_STATE:
                              SSM_WIDTH + BC_WIDTH + (g + 1) * SSM_STATE].astype(BF16))
            b_parts.append(jnp.concatenate([bg] * 4, axis=0))
    cb_all = lax.dot_general(jnp.stack(c_parts), jnp.stack(b_parts), (((2,), (2,)), ((0,), (0,))),
                             preferred_element_type=F32)
    for c in range(n_chunks):
        rws = slice(c * SSD_Q, (c + 1) * SSD_Q)
        cum_x = cumx_buf[rws, :]
        xdt = xc[rws, 0:SSM_WIDTH].astype(F32) * dtx_buf[rws, :]
        xdtb_buf[rws, :] = xdt.astype(BF16)
        xw_buf[rws, :] = (xdt * jnp.exp2(cum_x[SSD_Q - 1:SSD_Q, :] - cum_x)).astype(BF16)
        cum_j = jnp.sum(cum_x * eye_ref[...], axis=0, keepdims=True)
        decay = jnp.exp2(cum_x - cum_j + tril_ref[...])
        for g in range(SSM_GROUPS):
            for half in range(GROUP_WIDTH // QUAD):
                qs = slice(g * GROUP_WIDTH + half * QUAD, g * GROUP_WIDTH + (half + 1) * QUAD)
                m4_buf[rws, qs] = (cb_all[c * SSM_GROUPS + g] * decay[:, qs]).astype(BF16)

    def chunk(c, carry):
        r0 = pl.multiple_of(c * SSD_Q, SSD_Q)
        rows = pl.ds(r0, SSD_Q)
        xs = xc[rows, 0:SSM_WIDTH].astype(F32)
        bm = xc[rows, SSM_WIDTH:SSM_WIDTH + BC_WIDTH].astype(BF16)
        cm = xc[rows, SSM_WIDTH + BC_WIDTH:CONV_CH].astype(BF16)
        ecum_x = jnp.exp2(cumx_buf[rows, :])

        y_parts = []
        for g in range(SSM_GROUPS):
            cg = cm[:, g * SSM_STATE:(g + 1) * SSM_STATE]
            bg = bm[:, g * SSM_STATE:(g + 1) * SSM_STATE]
            gs = slice(g * GROUP_WIDTH, (g + 1) * GROUP_WIDTH)
            y_state = jnp.dot(cg, ht[:, gs].astype(BF16), preferred_element_type=F32) * ecum_x[:, gs]
            for half in range(GROUP_WIDTH // QUAD):
                qs = slice(g * GROUP_WIDTH + half * QUAD, g * GROUP_WIDTH + (half + 1) * QUAD)
                x4 = xdtb_buf[rows, qs]
                bd = jnp.concatenate([x4 * qmask_ref[a:a + 1, :] for a in range(4)], axis=0)
                y_parts.append(jnp.dot(m4_buf[rows, qs], bd, preferred_element_type=F32)
                               + y_state[:, half * QUAD:(half + 1) * QUAD])
            ht[:, gs] = (ht[:, gs] * ecum_x[SSD_Q - 1:SSD_Q, gs]
                         + jnp.dot(bg.T, xw_buf[rows, gs], preferred_element_type=F32))
        y = jnp.concatenate(y_parts, axis=1) + xs * dx
        yz = y * z_ref[rows, :].astype(F32)
        for g in range(SSM_GROUPS):
            gs = slice(g * GROUP_WIDTH, (g + 1) * GROUP_WIDTH)
            yg = yz[:, gs]
            ms = jnp.mean(yg * yg, axis=-1, keepdims=True)
            ybuf[rows, gs] = (yg * lax.rsqrt(ms + EPS)).astype(BF16)
        return carry

    lax.fori_loop(0, n_chunks, chunk, 0, unroll=min(8, n_chunks))
    branch_s = jnp.dot(ybuf[...], wso_ref[...], preferred_element_type=F32)
    merged = (gates_ref[:, 0:D_MODEL].astype(F32) * ba_ref[...]
              + gates_ref[:, D_MODEL:2 * D_MODEL].astype(F32) * branch_s)
    out = jnp.dot(merged.astype(BF16), wout_ref[...], preferred_element_type=F32)
    ms = jnp.mean(out * out, axis=-1, keepdims=True)
    y_ref[...] = x_ref[...] + out * lax.rsqrt(ms + EPS) * pnw_ref[...]

    @pl.when(blk == nblk - 1)
    def _():
        nssm_ref[0] = ht[...].T


def _ssd_masks():
    i = np.arange(SSD_Q)[:, None]
    j = np.arange(SSM_WIDTH)[None, :] % SSD_Q
    eye = jnp.asarray(i == j, F32)
    tril = jnp.asarray(np.where(i >= j, 0.0, -np.inf), F32)
    qmask = jnp.asarray(np.arange(QUAD)[None, :] // SSM_HEAD_DIM == np.arange(4)[:, None], BF16)
    return eye, tril, qmask


def _ssd(xc, zs, dt, x2d, branch_a, gates, ssm0, conv0, wts, *, bsz, seq, tc, l_real, conv_in_kernel):
    nblk = seq // tc
    assert not conv_in_kernel or nblk == 1
    row = lambda b, j: (b * nblk + j, 0)
    per_seq = lambda b, j: (b, 0, 0)
    hp = SSM_HEADS * SSM_HEAD_DIM
    kern = functools.partial(_ssd_kernel, tc=tc, out_row=l_real, conv_in_kernel=conv_in_kernel)
    operands = [xc, zs, dt, x2d, branch_a, gates, ssm0]
    in_specs = [pl.BlockSpec((tc, CONV_CH), row),
                pl.BlockSpec((tc, SSM_WIDTH), row),
                pl.BlockSpec((tc, LANES), row),
                pl.BlockSpec((tc, D_MODEL), row),
                pl.BlockSpec((tc, D_MODEL), row),
                pl.BlockSpec((tc, 2 * D_MODEL), row),
                pl.BlockSpec((1, hp, SSM_STATE), per_seq)]
    out_specs = [pl.BlockSpec((tc, D_MODEL), row), pl.BlockSpec((1, hp, SSM_STATE), per_seq)]
    out_shape = [jax.ShapeDtypeStruct((bsz * seq, D_MODEL), F32),
                 jax.ShapeDtypeStruct((bsz, hp, SSM_STATE), F32)]
    scratch = [pltpu.VMEM((SSM_STATE, SSM_WIDTH), F32), pltpu.VMEM((tc, SSM_WIDTH), BF16),
               pltpu.VMEM((tc, SSM_WIDTH), F32), pltpu.VMEM((tc, SSM_WIDTH), F32),
               pltpu.VMEM((tc, SSM_WIDTH), BF16), pltpu.VMEM((tc, SSM_WIDTH), BF16),
               pltpu.VMEM((tc, SSM_WIDTH), BF16)]
    if conv_in_kernel:
        operands += [conv0, wts["conv_w"], wts["conv_b"]]
        in_specs += [pl.BlockSpec((1, CONV_W - 1, CONV_CH), per_seq),
                     _resident((CONV_W, CONV_CH)), _resident((1, CONV_CH))]
        out_specs.append(pl.BlockSpec((1, CONV_W - 1, CONV_CH), per_seq))
        out_shape.append(jax.ShapeDtypeStruct((bsz, CONV_W - 1, CONV_CH), F32))
        scratch += [pltpu.VMEM((SUBLANES + tc, CONV_CH), F32), pltpu.VMEM((tc, CONV_CH), F32)]
    consts = [wts["a_row"], wts["e3"], wts["d_x"], wts["w_ssm_o"], wts["w_out"],
              wts["post_norm_w"], *_ssd_masks()]
    operands += consts
    in_specs += [_resident(c.shape) for c in consts]
    return pl.pallas_call(
        kern,
        grid=(bsz, nblk),
        in_specs=in_specs,
        out_specs=out_specs,
        out_shape=out_shape,
        scratch_shapes=scratch,
        compiler_params=_compiler_params(("arbitrary", "arbitrary")),
        name="ssd",
    )(*operands)


def _rope_angles(pos):
    half = ROT_DIM // 2
    inv = ROPE_THETA ** (-(jnp.arange(half, dtype=F32) * 2.0 / ROT_DIM))
    ang = pos.astype(F32)[:, None] * inv[None, :]
    return jnp.cos(ang), jnp.sin(ang)


def _rope_tables(pos):
    cos, sin = _rope_angles(pos)
    half = ROT_DIM // 2
    n = pos.shape[0]
    rest = HEAD_DIM - ROT_DIM
    c = jnp.concatenate([cos, cos, jnp.ones((n, rest), F32)], axis=1)
    s1 = jnp.concatenate([-sin, jnp.zeros((n, half + rest), F32)], axis=1)
    s2 = jnp.concatenate([jnp.zeros((n, half), F32), sin, jnp.zeros((n, rest), F32)], axis=1)
    rep = LANES // HEAD_DIM
    return tuple(jnp.tile(t, (1, rep)) for t in (c, s1, s2))


def _layer_prompt(x, pos, wts, *, tm, tb, tc):
    bsz, seq, _ = x.shape
    x2d = x.reshape(bsz * seq, D_MODEL)
    cos, sin = _rope_angles(pos)
    qt, gt, vt, k, v, zs, xc, gates, dt, conv_tail = _inproj_t(
        x2d, wts["pre_norm_w"], wts["w_cat"], wts["dt_bias"], (cos.T, sin.T), _rope_tables(pos),
        wts["conv_w"], wts["conv_b"], seq, tm)
    sink_rows = jnp.repeat(wts["sinks"].reshape(N_KV_HEADS, Q_PER_KV), PAIR, axis=1) * LOG2_E
    branch_a = _attention_t(qt, k, vt, gt, _window_bias(), sink_rows, wts["w_attn_o"], bsz=bsz, seq=seq, tb=tb)
    n_keep = min(WINDOW, seq)
    keep = lambda t: t.reshape(bsz, seq, KV_WIDTH)[:, seq - n_keep:].reshape(bsz, n_keep, N_KV_HEADS, HEAD_DIM)
    zero_ssm = jnp.zeros((bsz, SSM_HEADS * SSM_HEAD_DIM, SSM_STATE), F32)
    y, new_ssm = _ssd(xc, zs, dt, x2d, branch_a, gates, zero_ssm, None, wts,
                      bsz=bsz, seq=seq, tc=tc, l_real=seq, conv_in_kernel=False)
    return (y.reshape(bsz, seq, D_MODEL), keep(k), keep(v), conv_tail[:, SUBLANES - (CONV_W - 1):],
            new_ssm.reshape(bsz, SSM_HEADS, SSM_HEAD_DIM, SSM_STATE))


def _layer_sample(x, pos, cache_k, cache_v, conv0, ssm0, wts, *, tm):
    bsz, seq, _ = x.shape
    x2d = x.reshape(bsz * seq, D_MODEL)
    q, kv, g, zs, xbc, gates, dt = _inproj(x2d, wts["pre_norm_w"], wts["w_cat"], wts["dt_bias"], tm)
    sink_cols = jnp.broadcast_to(
        jnp.repeat(wts["sinks"].reshape(N_KV_HEADS, Q_PER_KV), seq, axis=1)[:, :, None],
        (N_KV_HEADS, Q_PER_KV * seq, LANES))
    branch_a, new_k, new_v = _attention(
        q, kv, g, cache_k, cache_v, _rope_tables(pos), sink_cols, wts["w_attn_o"],
        bsz=bsz, seq=seq, tb=seq, ch=seq, mask_first=False)
    seq_p = -(-seq // SSD_Q) * SSD_Q
    padrows = lambda t: jnp.pad(t.reshape(bsz, seq, -1), ((0, 0), (0, seq_p - seq), (0, 0))).reshape(
        bsz * seq_p, -1)
    y, new_ssm, new_conv = _ssd(
        padrows(xbc), padrows(zs), padrows(dt), padrows(x2d), padrows(branch_a), padrows(gates), ssm0, conv0, wts,
        bsz=bsz, seq=seq_p, tc=seq_p, l_real=seq, conv_in_kernel=True)
    n_keep = new_k.shape[1]
    return (y.reshape(bsz, seq_p, D_MODEL)[:, :seq],
            new_k.reshape(bsz, n_keep, N_KV_HEADS, HEAD_DIM), new_v.reshape(bsz, n_keep, N_KV_HEADS, HEAD_DIM),
            new_conv, new_ssm.reshape(bsz, SSM_HEADS, SSM_HEAD_DIM, SSM_STATE))


def _prep_weights(pre_norm_w, w_in, conv_w, conv_b, dt_bias, a_log, d_skip, sinks, ssm_norm_w, w_attn_o,
                  w_ssm_o, w_out, post_norm_w):
    g_off = ATT_WIDTH + 2 * KV_WIDTH
    z_off = g_off + ATT_WIDTH
    dt_off = z_off + SSM_WIDTH + CONV_CH
    xbc_off = z_off + SSM_WIDTH
    ga_off = dt_off + SSM_HEADS
    gs_off = ga_off + D_MODEL
    tile = lambda off, j: w_in[:, off + j * LANES:off + (j + 1) * LANES]
    mixed = []
    for j in range(N_MIXED):
        mixed += [tile(xbc_off, j), tile(z_off, j) if j < N_MIXED_Z else tile(ga_off, j - N_MIXED_Z)]
    w_cat = jnp.concatenate(
        mixed + [w_in[:, gs_off:], w_in[:, ATT_WIDTH:g_off],
         w_in[:, dt_off:dt_off + SSM_HEADS], jnp.zeros((D_MODEL, LANES - SSM_HEADS), w_in.dtype),
         w_in[:, :ATT_WIDTH], w_in[:, g_off:z_off]], axis=1).astype(BF16)
    pad_heads = lambda v: jnp.pad(v.astype(F32), (0, LANES - SSM_HEADS)).reshape(1, LANES)
    k_idx = np.arange(LANES)[:, None]
    c_idx = np.arange(SSM_WIDTH)[None, :]
    e3 = jnp.asarray((k_idx < 3 * PIECE) & (k_idx % PIECE == c_idx // SSM_HEAD_DIM), BF16)
    return dict(
        pre_norm_w=pre_norm_w.reshape(1, D_MODEL), w_cat=w_cat, dt_bias=pad_heads(dt_bias),
        conv_w=conv_w, conv_b=conv_b.reshape(1, CONV_CH), a_row=pad_heads(-jnp.exp(a_log.astype(F32)) * LOG2_E),
        e3=e3, d_x=jnp.repeat(d_skip.astype(F32), SSM_HEAD_DIM).reshape(1, SSM_WIDTH), sinks=sinks.astype(F32),
        w_attn_o=w_attn_o.astype(BF16),
        w_ssm_o=(ssm_norm_w.astype(F32)[:, None] * w_ssm_o).astype(BF16),
        w_out=w_out.astype(BF16), post_norm_w=post_norm_w.reshape(1, D_MODEL))


PROMPT_IN_ROWS = 512
PROMPT_ATT_ROWS = 1024
PROMPT_SSD_ROWS = 512
SAMPLE_IN_ROWS = 256


def _largest_tile(rows, cap):
    for unit in (PAIR, CHUNK, SUBLANES):
        best = max((t for t in range(unit, min(rows, cap) + 1, unit) if rows % t == 0), default=0)
        if best:
            return best
    raise ValueError(f"no row tile for {rows} rows")


def _prompt_tiles(seq):
    return dict(tm=_largest_tile(seq, PROMPT_IN_ROWS), tb=_largest_tile(seq, PROMPT_ATT_ROWS),
                tc=_largest_tile(seq, PROMPT_SSD_ROWS))


def kernel(x_prompt, x_sample, cache_k, cache_v, state_conv, state_ssm, pre_norm_w, w_in, conv_w, conv_b, dt_bias,
           a_log, d_skip, sinks, ssm_norm_w, w_attn_o, w_ssm_o, w_out, post_norm_w):
    depth = w_in.shape[0]
    lp = x_prompt.shape[1]
    bs, ls, _ = x_sample.shape
    pos_p = jnp.arange(lp, dtype=F32)
    pos_s = PAST_LEN + jnp.arange(ls, dtype=F32)
    hp = SSM_HEADS * SSM_HEAD_DIM
    yp, ys = x_prompt, x_sample
    outs = [[] for _ in range(8)]
    for layer in range(depth):
        wts = _prep_weights(pre_norm_w[layer], w_in[layer], conv_w[layer], conv_b[layer], dt_bias[layer],
                            a_log[layer], d_skip[layer], sinks[layer], ssm_norm_w[layer], w_attn_o[layer],
                            w_ssm_o[layer], w_out[layer], post_norm_w[layer])
        yp, kp, vp, cp, sp = _layer_prompt(yp, pos_p, wts, **_prompt_tiles(lp))
        ys, ks, vs, cs, ss = _layer_sample(ys, pos_s,
                                           cache_k[layer].reshape(bs, WINDOW, KV_WIDTH),
                                           cache_v[layer].reshape(bs, WINDOW, KV_WIDTH),
                                           state_conv[layer], state_ssm[layer].reshape(bs, hp, SSM_STATE), wts,
                                           tm=_largest_tile(bs * ls, SAMPLE_IN_ROWS))
        for lst, val in zip(outs, (kp, vp, cp, sp, ks, vs, cs, ss)):
            lst.append(val)
    return (yp, ys) + tuple(jnp.stack(lst) for lst in outs)
```
